```python
import math
import jax
import jax.numpy as jnp
from jax import lax
import numpy as np

D_MODEL = 1024
BATCH = 4
SEQ = 4096
DEPTH = 2
DEC_BATCH = 32
DEC_SEQ = 1
PAST_LEN = 8192
PAGE_SIZE = 128

HEAD_DIM = 64
NSA_WIDTH = D_MODEL // 2
NSA_HEADS = NSA_WIDTH // HEAD_DIM
NSA_KV_HEADS = 2
NSA_QPG = NSA_HEADS // NSA_KV_HEADS
KV_WIDTH = NSA_KV_HEADS * HEAD_DIM
SCALE = HEAD_DIM ** -0.5
L_CMP = 32
L_SEL = 64
N_SEL = 16
WINDOW = 512
Q_BLOCK = 128
SEL_Q_BLOCK = 64
N_BUCKETS = 32
REL_MAX_DIST = 128
RET_WIDTH = D_MODEL // 4
RET_DK = 64
RET_DV = 64
RET_HEADS = RET_WIDTH // RET_DV
RET_CHUNK = 128
ROPE_BASE = 10000.0
CONV_CH = D_MODEL // 4
CONV_W = 3
MIX_WIDTH = NSA_WIDTH + RET_WIDTH + CONV_CH
PEER_HEADS = 8
PEER_DK = 256
N_KEYS = 128
N_EXPERTS = N_KEYS * N_KEYS
PEER_TOPK = 16
PEER_CHUNK = 256
LN_EPS = 1e-5
SPLIT_SIZES = (NSA_WIDTH, 6 * KV_WIDTH, 3 * NSA_HEADS, RET_HEADS * RET_DK, RET_HEADS * RET_DK, RET_WIDTH, RET_WIDTH, CONV_CH, CONV_CH, CONV_CH)
N_IN = sum(SPLIT_SIZES)

kernel_name = 'hymba_nsa_retnet_conv_peer_step'


def layer_norm(x, g, b):
    xf = x.astype(jnp.float32)
    mu = jnp.mean(xf, -1, keepdims=True)
    var = jnp.mean(jnp.square(xf - mu), -1, keepdims=True)
    return ((xf - mu) * lax.rsqrt(var + LN_EPS) * g + b).astype(x.dtype)


def masked_softmax(logits, mask):
    l = jnp.where(mask, logits.astype(jnp.float32), -jnp.inf)
    m = jnp.max(l, axis=-1, keepdims=True)
    m = jnp.where(jnp.isfinite(m), m, 0.0)
    e = jnp.where(mask, jnp.exp(l - m), 0.0)
    return e / jnp.maximum(jnp.sum(e, -1, keepdims=True), 1e-30)


def t5_bucket(dist):
    n = jnp.maximum(dist, 0)
    exact = N_BUCKETS // 2
    nf = jnp.maximum(n, exact).astype(jnp.float32)
    big = exact + (jnp.log(nf / exact) / math.log(REL_MAX_DIST / exact) * (N_BUCKETS - exact)).astype(jnp.int32)
    return jnp.where(n < exact, n, jnp.minimum(big, N_BUCKETS - 1))


def rotary(x, pos):
    half = x.shape[-1] // 2
    inv = ROPE_BASE ** (-jnp.arange(half, dtype=jnp.float32) / half)
    ang = pos.astype(jnp.float32)[:, None] * inv[None, :]
    cos = jnp.cos(ang)[None, :, None, :]
    sin = jnp.sin(ang)[None, :, None, :]
    x1, x2 = x[..., :half], x[..., half:]
    return jnp.concatenate([x1 * cos - x2 * sin, x1 * sin + x2 * cos], axis=-1)


def split_proj(p):
    outs, start = [], 0
    for size in SPLIT_SIZES:
        outs.append(p[..., start:start + size])
        start += size
    return outs


def pad_len(a, mult):
    n = (-a.shape[1]) % mult
    return jnp.pad(a, [(0, 0), (0, n)] + [(0, 0)] * (a.ndim - 2))


def compress(rows, w):
    b, lp, g, d = rows.shape
    blk = rows.reshape(b, lp // L_CMP, L_CMP, g, d).transpose(0, 1, 3, 2, 4).reshape(b, lp // L_CMP, g, L_CMP * d)
    return blk @ w


def cmp_branch(q, q_pos, kc, vc, rel_table):
    t, ncb = q.shape[1], kc.shape[1]
    end = jnp.arange(ncb, dtype=jnp.int32) * L_CMP + (L_CMP - 1)
    dist = q_pos[:, None] - end[None, :]
    bias = rel_table[t5_bucket(dist)].reshape(t, ncb, NSA_KV_HEADS, NSA_QPG).transpose(0, 2, 3, 1)
    logits = jnp.einsum('btgqd,bngd->btgqn', q, kc) * SCALE + bias
    p = masked_softmax(logits, (dist >= 0)[:, None, None, :])
    return jnp.einsum('btgqn,bngd->btgqd', p.astype(vc.dtype), vc), p


def select_blocks(p, q_pos, nsb):
    b, t = p.shape[:2]
    imp = jnp.sum(p, axis=3).reshape(b, t, NSA_KV_HEADS, nsb, L_SEL // L_CMP).sum(-1)
    cur = q_pos // L_SEL
    j = jnp.arange(nsb, dtype=jnp.int32)[None, :]
    forced = (j == 0) | (j == cur[:, None]) | (j == cur[:, None] - 1)
    visible = j <= cur[:, None]
    imp = jnp.where(forced[:, None, :], jnp.inf, imp)
    imp = jnp.where(visible[:, None, :], imp, -jnp.inf)
    vals, idx = lax.top_k(imp, min(N_SEL, nsb))
    return idx, vals > -jnp.inf


def sel_branch(q, q_pos, idx, valid, ks, vs, rel_table):
    b, t = q.shape[:2]
    kk = idx.shape[-1]
    ksg = ks.transpose(0, 3, 1, 2, 4)
    vsg = vs.transpose(0, 3, 1, 2, 4)
    bi = jnp.arange(b)[:, None, None, None]
    gi = jnp.arange(NSA_KV_HEADS)[None, None, :, None]
    kg = ksg[bi, gi, idx]
    vg = vsg[bi, gi, idx]
    s_pos = idx[..., None] * L_SEL + jnp.arange(L_SEL, dtype=jnp.int32)
    dist = q_pos[None, :, None, None, None] - s_pos
    mask = valid[..., None] & (dist >= 0)
    tb = rel_table.reshape(N_BUCKETS, NSA_KV_HEADS, NSA_QPG).transpose(1, 0, 2)
    bias = tb[gi[..., None], t5_bucket(dist)]
    logits = jnp.einsum('btgqd,btgksd->btgqks', q, kg) * SCALE + jnp.moveaxis(bias, -1, 3)
    logits = logits.reshape(b, t, NSA_KV_HEADS, NSA_QPG, kk * L_SEL)
    p = masked_softmax(logits, mask.reshape(b, t, NSA_KV_HEADS, 1, kk * L_SEL))
    return jnp.einsum('btgqn,btgnd->btgqd', p.astype(vg.dtype), vg.reshape(b, t, NSA_KV_HEADS, kk * L_SEL, HEAD_DIM))


def nsa_global_branches(q, q_pos, rows4, w_cmp, rel_table):
    rows4 = pad_len(rows4, L_SEL)
    b, lp = rows4.shape[:2]
    nsb = lp // L_SEL
    kc = compress(rows4[:, :, 0], w_cmp[0])
    vc = compress(rows4[:, :, 1], w_cmp[1])
    ks = rows4[:, :, 2].reshape(b, nsb, L_SEL, NSA_KV_HEADS, HEAD_DIM)
    vs = rows4[:, :, 3].reshape(b, nsb, L_SEL, NSA_KV_HEADS, HEAD_DIM)

    def body(args):
        qb, pb = args
        o_c, p = cmp_branch(qb, pb, kc, vc, rel_table)
        idx, valid = select_blocks(p, pb, nsb)
        return o_c, sel_branch(qb, pb, idx, valid, ks, vs, rel_table)

    t = q.shape[1]
    if t > SEL_Q_BLOCK and t % SEL_Q_BLOCK == 0:
        nb = t // SEL_Q_BLOCK
        qb = jnp.moveaxis(q.reshape(b, nb, SEL_Q_BLOCK, NSA_KV_HEADS, NSA_QPG, HEAD_DIM), 1, 0)
        pb = q_pos.reshape(nb, SEL_Q_BLOCK)
        o_c, o_s = lax.map(body, (qb, pb))
        return jnp.moveaxis(o_c, 0, 1).reshape(q.shape), jnp.moveaxis(o_s, 0, 1).reshape(q.shape)
    return body((q, q_pos))


def win_branch_banded(q, kw, vw, rel_table):
    b, s = q.shape[:2]
    nqb = s // Q_BLOCK
    nk = Q_BLOCK + WINDOW
    kp = jnp.pad(kw, ((0, 0), (WINDOW, 0), (0, 0), (0, 0)))
    vp = jnp.pad(vw, ((0, 0), (WINDOW, 0), (0, 0), (0, 0)))
    a = jnp.arange(Q_BLOCK, dtype=jnp.int32)
    j = jnp.arange(nk, dtype=jnp.int32)
    dist = a[:, None] + WINDOW - j[None, :]
    band = (dist >= 0) & (dist < WINDOW)
    bias = rel_table[t5_bucket(dist)].reshape(Q_BLOCK, nk, NSA_KV_HEADS, NSA_QPG).transpose(0, 2, 3, 1)

    def body(args):
        qb, start = args
        kb = lax.dynamic_slice_in_dim(kp, start, nk, axis=1)
        vb = lax.dynamic_slice_in_dim(vp, start, nk, axis=1)
        mask = band & ((start - WINDOW + j) >= 0)[None, :]
        logits = jnp.einsum('bigqd,bjgd->bigqj', qb, kb) * SCALE + bias
        p = masked_softmax(logits, mask[:, None, None, :])
        return jnp.einsum('bigqj,bjgd->bigqd', p.astype(vb.dtype), vb)

    qs = jnp.moveaxis(q.reshape(b, nqb, Q_BLOCK, NSA_KV_HEADS, NSA_QPG, HEAD_DIM), 1, 0)
    starts = jnp.arange(nqb, dtype=jnp.int32) * Q_BLOCK
    o = lax.map(body, (qs, starts))
    return jnp.moveaxis(o, 0, 1).reshape(q.shape)


def win_branch_dense(q, q_pos, kw, vw, k_pos, rel_table):
    t, l = q.shape[1], kw.shape[1]
    dist = q_pos[:, None] - k_pos[None, :]
    mask = (dist >= 0) & (dist < WINDOW)
    bias = rel_table[t5_bucket(dist)].reshape(t, l, NSA_KV_HEADS, NSA_QPG).transpose(0, 2, 3, 1)
    logits = jnp.einsum('btgqd,blgd->btgql', q, kw) * SCALE + bias
    p = masked_softmax(logits, mask[:, None, None, :])
    return jnp.einsum('btgql,blgd->btgqd', p.astype(vw.dtype), vw)


def retention(q, k, v, s0, chunk):
    b, t = q.shape[:2]
    nc = t // chunk
    lg = jnp.log(1.0 - 2.0 ** (-5.0 - jnp.arange(RET_HEADS, dtype=jnp.float32)))
    i = jnp.arange(chunk, dtype=jnp.float32)
    diff = i[:, None] - i[None, :]
    dmat = jnp.where(diff >= 0, jnp.exp(jnp.maximum(diff, 0.0)[None] * lg[:, None, None]), 0.0)
    q_dec = jnp.exp((i + 1.0)[:, None] * lg[None, :])[None, :, :, None]
    k_dec = jnp.exp((chunk - 1.0 - i)[:, None] * lg[None, :])[None, :, :, None]
    s_dec = jnp.exp(chunk * lg)[None, :, None, None]

    def to_chunks(a):
        return jnp.moveaxis(a.astype(jnp.float32).reshape(b, nc, chunk, RET_HEADS, a.shape[-1]), 1, 0)

    def step(s, xs):
        qc, kc, vc = xs
        att = jnp.einsum('bihd,bjhd->bhij', qc, kc) * dmat
        o = jnp.einsum('bhij,bjhe->bihe', att, vc) + jnp.einsum('bihd,bhde->bihe', qc * q_dec, s)
        s = s * s_dec + jnp.einsum('bjhd,bjhe->bhde', kc * k_dec, vc)
        return s, o

    s, o = lax.scan(step, s0.astype(jnp.float32), (to_chunks(q), to_chunks(k), to_chunks(v)))
    return jnp.moveaxis(o, 0, 1).reshape(b, t, RET_HEADS, RET_DV), s


def token_mixers(parts, pos, rel_table, w_cmp, conv_w, conv_b, ret_g, past):
    nq, nkv, ngate, rq, rk, rv, rg, cb, cc, ch = parts
    b, t = nq.shape[:2]
    q = nq.reshape(b, t, NSA_KV_HEADS, NSA_QPG, HEAD_DIM)
    kv = nkv.reshape(b, t, 6, NSA_KV_HEADS, HEAD_DIM)
    if past is None:
        rows4 = kv[:, :, :4]
        o_win = win_branch_banded(q, kv[:, :, 4], kv[:, :, 5], rel_table)
        win_rows = kv[:, t - min(WINDOW, t):, 4:]
        s0 = jnp.zeros((b, RET_HEADS, RET_DK, RET_DV), jnp.float32)
        zbuf = jnp.zeros((b, CONV_W - 1, CONV_CH), ch.dtype)
        chunk = RET_CHUNK
    else:
        past4, win_buf, s0, zbuf, past_len = past
        rows4 = jnp.concatenate([past4.astype(kv.dtype), kv[:, :, :4]], axis=1)
        wrows = jnp.concatenate([win_buf.astype(kv.dtype), kv[:, :, 4:]], axis=1)
        k_pos = past_len - win_buf.shape[1] + jnp.arange(wrows.shape[1], dtype=jnp.int32)
        o_win = win_branch_dense(q, pos, wrows[:, :, 0], wrows[:, :, 1], k_pos, rel_table)
        win_rows = wrows[:, wrows.shape[1] - min(WINDOW, wrows.shape[1]):]
        chunk = t
    o_cmp, o_sel = nsa_global_branches(q, pos, rows4, w_cmp, rel_table)
    g = jax.nn.sigmoid(ngate.astype(jnp.float32)).reshape(b, t, NSA_KV_HEADS, NSA_QPG, 3).astype(o_cmp.dtype)
    o_nsa = (g[..., 0:1] * o_cmp + g[..., 1:2] * o_sel + g[..., 2:3] * o_win).reshape(b, t, NSA_WIDTH)
    rqh = rotary(rq.reshape(b, t, RET_HEADS, RET_DK), pos)
    rkh = rotary(rk.reshape(b, t, RET_HEADS, RET_DK), pos) * (RET_DK ** -0.5)
    rvh = rv.reshape(b, t, RET_HEADS, RET_DV)
    o_r, s_new = retention(rqh, rkh, rvh, s0, chunk)
    mu = jnp.mean(o_r, -1, keepdims=True)
    var = jnp.mean(jnp.square(o_r - mu), -1, keepdims=True)
    on = (o_r - mu) * lax.rsqrt(var + LN_EPS) * ret_g.reshape(RET_HEADS, RET_DV)
    o_ret = (on.reshape(b, t, RET_WIDTH) * jax.nn.silu(rg.astype(jnp.float32))).astype(nq.dtype)
    z = cc * ch
    zp = jnp.concatenate([zbuf.astype(z.dtype), z], axis=1)
    y = conv_b + sum(zp[:, j:j + t] * conv_w[j] for j in range(CONV_W))
    o_conv = (cb * y).astype(nq.dtype)
    mix = jnp.concatenate([o_nsa, o_ret, o_conv], axis=-1)
    return mix, (kv[:, :, :4], win_rows, s_new, zp[:, t:])


def peer_ffn(h, wq, sub_keys, u, v):
    b, t, d = h.shape
    n = b * t
    xf = jnp.pad(h.reshape(n, d), ((0, (-n) % PEER_CHUNK), (0, 0)))

    def chunk_fn(xc):
        c = xc.shape[0]
        q = (xc @ wq).reshape(c, PEER_HEADS, 2, PEER_DK // 2)
        s = jnp.einsum('chpd,hpkd->chpk', q, sub_keys).astype(jnp.float32)
        s1, i1 = lax.top_k(s[:, :, 0], PEER_TOPK)
        s2, i2 = lax.top_k(s[:, :, 1], PEER_TOPK)
        cand = (s1[..., :, None] + s2[..., None, :]).reshape(c, PEER_HEADS, PEER_TOPK * PEER_TOPK)
        cidx = (i1[..., :, None] * N_KEYS + i2[..., None, :]).reshape(c, PEER_HEADS, PEER_TOPK * PEER_TOPK)
        sc, jj = lax.top_k(cand, PEER_TOPK)
        e = jnp.take_along_axis(cidx, jj, axis=-1)
        gate = jax.nn.softmax(sc, axis=-1)
        act = jax.nn.gelu(jnp.einsum('cd,chkd->chk', xc, u[e]).astype(jnp.float32))
        return jnp.einsum('chk,chkd->cd', (gate * act).astype(v.dtype), v[e])

    y = lax.map(chunk_fn, xf.reshape(-1, PEER_CHUNK, d))
    return y.reshape(-1, d)[:n].reshape(b, t, d).astype(h.dtype)


def setup_inputs(seed: int = 0) -> dict:
    key = jax.random.key(seed)
    ks = jax.random.split(key, 32)

    def nrm(k, shape, scale):
        return jax.random.normal(k, shape, jnp.float32) * scale

    n_pages = PAST_LEN // PAGE_SIZE
    used = DEC_BATCH * n_pages
    n_phys = used + max(1, used // 4)
    win_buf = min(WINDOW, PAST_LEN)
    beta = (8.0 * DEPTH) ** -0.25
    perm = jax.random.permutation(ks[0], n_phys)
    page_table = perm[:used].reshape(DEC_BATCH, n_pages).astype(jnp.int32)
    return {
        'x_prompt': nrm(ks[1], (BATCH, SEQ, D_MODEL), 1.0),
        'x_sample': nrm(ks[2], (DEC_BATCH, DEC_SEQ, D_MODEL), 1.0),
        'cache_nsa_kv': nrm(ks[3], (DEPTH, n_phys, PAGE_SIZE, 4, NSA_KV_HEADS, HEAD_DIM), 1.0),
        'cache_win_kv': nrm(ks[4], (DEPTH, DEC_BATCH, win_buf, 2, NSA_KV_HEADS, HEAD_DIM), 1.0),
        'state_ret': nrm(ks[5], (DEPTH, DEC_BATCH, RET_HEADS, RET_DK, RET_DV), 1.0),
        'state_conv': nrm(ks[6], (DEPTH, DEC_BATCH, CONV_W - 1, CONV_CH), 1.0),
        'page_table': page_table,
        'c_prompt': nrm(ks[7], (BATCH, D_MODEL), 1.0),
        'c_sample': nrm(ks[8], (DEC_BATCH, D_MODEL), 1.0),
        'rel_bias': nrm(ks[9], (N_BUCKETS, NSA_HEADS), 0.5),
        'w_ada': nrm(ks[10], (DEPTH, D_MODEL, 6 * D_MODEL), 0.5 * D_MODEL ** -0.5),
        'b_ada': nrm(ks[11], (DEPTH, 6 * D_MODEL), 0.02),
        'w_in': nrm(ks[12], (DEPTH, D_MODEL, N_IN), D_MODEL ** -0.5),
        'w_cmp': nrm(ks[13], (DEPTH, 2, L_CMP * HEAD_DIM, HEAD_DIM), (L_CMP * HEAD_DIM) ** -0.5),
        'conv_w': nrm(ks[14], (DEPTH, CONV_W, CONV_CH), CONV_W ** -0.5),
        'conv_b': nrm(ks[15], (DEPTH, CONV_CH), 0.02),
        'ret_norm_g': 1.0 + nrm(ks[16], (DEPTH, RET_WIDTH), 0.02),
        'w_out': nrm(ks[17], (DEPTH, MIX_WIDTH, D_MODEL), beta * MIX_WIDTH ** -0.5),
        'ln1_g': 1.0 + nrm(ks[18], (DEPTH, D_MODEL), 0.02),
        'ln1_b': nrm(ks[19], (DEPTH, D_MODEL), 0.02),
        'ln2_g': 1.0 + nrm(ks[20], (DEPTH, D_MODEL), 0.02),
        'ln2_b': nrm(ks[21], (DEPTH, D_MODEL), 0.02),
        'peer_wq': nrm(ks[22], (DEPTH, D_MODEL, PEER_HEADS * PEER_DK), D_MODEL ** -0.5),
        'peer_keys': nrm(ks[23], (DEPTH, PEER_HEADS, 2, N_KEYS, PEER_DK // 2), (PEER_DK // 2) ** -0.5),
        'peer_u': nrm(ks[24], (DEPTH, N_EXPERTS, D_MODEL), D_MODEL ** -0.5),
        'peer_v': nrm(ks[25], (DEPTH, N_EXPERTS, D_MODEL), beta * PEER_HEADS ** -0.5),
    }


def reference(x_prompt, x_sample, cache_nsa_kv, cache_win_kv, state_ret, state_conv, page_table, c_prompt, c_sample, rel_bias, w_ada, b_ada, w_in, w_cmp, conv_w, conv_b, ret_norm_g, w_out, ln1_g, ln1_b, ln2_g, ln2_b, peer_wq, peer_keys, peer_u, peer_v):
    alpha = (2.0 * DEPTH) ** 0.25
    n_pages = page_table.shape[1]
    past_len = n_pages * PAGE_SIZE
    sp = x_prompt.shape[1]
    bs, ts = x_sample.shape[:2]
    pos_p = jnp.arange(sp, dtype=jnp.int32)
    pos_s = past_len + jnp.arange(ts, dtype=jnp.int32)

    def run_layer(x, c, l, pos, past):
        m = (jax.nn.silu(c) @ w_ada[l] + b_ada[l]).reshape(c.shape[0], 6, 1, D_MODEL)
        h = x * (1.0 + m[:, 1]) + m[:, 0]
        parts = split_proj(h @ w_in[l])
        mix, st = token_mixers(parts, pos, rel_bias, w_cmp[l], conv_w[l], conv_b[l], ret_norm_g[l], past)
        x = layer_norm(alpha * x + (1.0 + m[:, 2]) * (mix @ w_out[l]), ln1_g[l], ln1_b[l])
        h = x * (1.0 + m[:, 4]) + m[:, 3]
        y = peer_ffn(h, peer_wq[l], peer_keys[l], peer_u[l], peer_v[l])
        x = layer_norm(alpha * x + (1.0 + m[:, 5]) * y, ln2_g[l], ln2_b[l])
        return x, st

    xp, xs = x_prompt, x_sample
    nkv_p, nkv_s, win_p, win_s, ret_p, ret_s, conv_p, conv_s = [], [], [], [], [], [], [], []
    for l in range(DEPTH):
        xp, st_p = run_layer(xp, c_prompt, l, pos_p, None)
        past4 = cache_nsa_kv[l][page_table].reshape(bs, past_len, 4, NSA_KV_HEADS, HEAD_DIM)
        xs, st_s = run_layer(xs, c_sample, l, pos_s, (past4, cache_win_kv[l], state_ret[l], state_conv[l], past_len))
        nkv_p.append(st_p[0])
        win_p.append(st_p[1])
        ret_p.append(st_p[2])
        conv_p.append(st_p[3])
        nkv_s.append(st_s[0])
        win_s.append(st_s[1])
        ret_s.append(st_s[2])
        conv_s.append(st_s[3])
    return (xp, xs, jnp.stack(nkv_p), jnp.stack(nkv_s), jnp.stack(win_p), jnp.stack(win_s), jnp.stack(ret_p), jnp.stack(ret_s), jnp.stack(conv_p), jnp.stack(conv_s))
```

```python
import math
import functools
import jax
import jax.numpy as jnp
from jax import lax
import numpy as np
from jax.experimental import pallas as pl
from jax.experimental.pallas import tpu as pltpu

D_MODEL = 1024
BATCH = 4
SEQ = 4096
DEPTH = 2
DEC_BATCH = 32
DEC_SEQ = 1
PAST_LEN = 8192
PAGE_SIZE = 128

HEAD_DIM = 64
NSA_WIDTH = D_MODEL // 2
NSA_HEADS = NSA_WIDTH // HEAD_DIM
NSA_KV_HEADS = 2
NSA_QPG = NSA_HEADS // NSA_KV_HEADS
KV_WIDTH = NSA_KV_HEADS * HEAD_DIM
SCALE = HEAD_DIM ** -0.5
L_CMP = 32
L_SEL = 64
N_SEL = 16
WINDOW = 512
Q_BLOCK = 128
SEL_Q_BLOCK = 64
N_BUCKETS = 32
REL_MAX_DIST = 128
RET_WIDTH = D_MODEL // 4
RET_DK = 64
RET_DV = 64
RET_HEADS = RET_WIDTH // RET_DV
RET_CHUNK = 128
ROPE_BASE = 10000.0
CONV_CH = D_MODEL // 4
CONV_W = 3
MIX_WIDTH = NSA_WIDTH + RET_WIDTH + CONV_CH
PEER_HEADS = 8
PEER_DK = 256
N_KEYS = 128
N_EXPERTS = N_KEYS * N_KEYS
PEER_TOPK = 16
PEER_CHUNK = 256
LN_EPS = 1e-5
SPLIT_SIZES = (NSA_WIDTH, 6 * KV_WIDTH, 3 * NSA_HEADS, RET_HEADS * RET_DK, RET_HEADS * RET_DK, RET_WIDTH, RET_WIDTH, CONV_CH, CONV_CH, CONV_CH)
N_IN = sum(SPLIT_SIZES)

F32 = jnp.float32
BF16 = jnp.bfloat16
NEG_INF = float('-inf')
MASK_NEG = -1e9
CMP_MASK_NEG = -1e30
NSA_TQ = 256
PEER_TL = 512
PEER_TE = 1024
LANES = 128
VMEM_LIMIT = 56 * 1024 * 1024


def _ln_kernel(x_ref, g_ref, b_ref, o_ref):
    x = x_ref[...]
    mu = jnp.mean(x, -1, keepdims=True)
    xc = x - mu
    var = jnp.mean(xc * xc, -1, keepdims=True)
    o_ref[...] = xc * lax.rsqrt(var + LN_EPS) * g_ref[...] + b_ref[...]


def layer_norm(x, g, b):
    shp = x.shape
    x2 = x.reshape(-1, shp[-1])
    n = x2.shape[0]
    tm = min(n, 512)
    out = pl.pallas_call(
        _ln_kernel,
        out_shape=jax.ShapeDtypeStruct(x2.shape, jnp.float32),
        grid=(n // tm,),
        in_specs=[pl.BlockSpec((tm, shp[-1]), lambda i: (i, 0)),
                  pl.BlockSpec((1, shp[-1]), lambda i: (0, 0)),
                  pl.BlockSpec((1, shp[-1]), lambda i: (0, 0))],
        out_specs=pl.BlockSpec((tm, shp[-1]), lambda i: (i, 0)),
        name="layer_norm",
    )(x2, g.reshape(1, -1), b.reshape(1, -1))
    return out.reshape(shp)


def masked_softmax(logits, mask):
    l = jnp.where(mask, logits.astype(jnp.float32), -jnp.inf)
    m = jnp.max(l, axis=-1, keepdims=True)
    m = jnp.where(jnp.isfinite(m), m, 0.0)
    e = jnp.where(mask, jnp.exp(l - m), 0.0)
    return e / jnp.maximum(jnp.sum(e, -1, keepdims=True), 1e-30)


def t5_bucket(dist):
    n = jnp.maximum(dist, 0)
    exact = N_BUCKETS // 2
    nf = jnp.maximum(n, exact).astype(jnp.float32)
    big = exact + (jnp.log(nf / exact) / math.log(REL_MAX_DIST / exact) * (N_BUCKETS - exact)).astype(jnp.int32)
    return jnp.where(n < exact, n, jnp.minimum(big, N_BUCKETS - 1))


def rotary(x, pos):
    half = x.shape[-1] // 2
    inv = ROPE_BASE ** (-jnp.arange(half, dtype=jnp.float32) / half)
    ang = pos.astype(jnp.float32)[:, None] * inv[None, :]
    cos = jnp.cos(ang)[None, :, None, :]
    sin = jnp.sin(ang)[None, :, None, :]
    x1, x2 = x[..., :half], x[..., half:]
    return jnp.concatenate([x1 * cos - x2 * sin, x1 * sin + x2 * cos], axis=-1)


def split_proj(p):
    outs, start = [], 0
    for size in SPLIT_SIZES:
        outs.append(p[..., start:start + size])
        start += size
    return outs


def pad_len(a, mult):
    n = (-a.shape[1]) % mult
    return jnp.pad(a, [(0, 0), (0, n)] + [(0, 0)] * (a.ndim - 2))


def compress(rows, w):
    b, lp, g, d = rows.shape
    blk = rows.reshape(b, lp // L_CMP, L_CMP, g, d).transpose(0, 1, 3, 2, 4).reshape(b, lp // L_CMP, g, L_CMP * d)
    return blk @ w


def cmp_branch(q, q_pos, kc, vc, rel_table):
    t, ncb = q.shape[1], kc.shape[1]
    end = jnp.arange(ncb, dtype=jnp.int32) * L_CMP + (L_CMP - 1)
    dist = q_pos[:, None] - end[None, :]
    bias = rel_table[t5_bucket(dist)].reshape(t, ncb, NSA_KV_HEADS, NSA_QPG).transpose(0, 2, 3, 1)
    logits = jnp.einsum('btgqd,bngd->btgqn', q, kc) * SCALE + bias
    p = masked_softmax(logits, (dist >= 0)[:, None, None, :])
    return jnp.einsum('btgqn,bngd->btgqd', p.astype(vc.dtype), vc), p


def select_blocks(p, q_pos, nsb):
    b, t = p.shape[:2]
    imp = jnp.sum(p, axis=3).reshape(b, t, NSA_KV_HEADS, nsb, L_SEL // L_CMP).sum(-1)
    cur = q_pos // L_SEL
    j = jnp.arange(nsb, dtype=jnp.int32)[None, :]
    forced = (j == 0) | (j == cur[:, None]) | (j == cur[:, None] - 1)
    visible = j <= cur[:, None]
    imp = jnp.where(forced[:, None, :], jnp.inf, imp)
    imp = jnp.where(visible[:, None, :], imp, -jnp.inf)
    vals, idx = lax.top_k(imp, min(N_SEL, nsb))
    return idx, vals > -jnp.inf


def sel_branch(q, q_pos, idx, valid, ks, vs, rel_table):
    b, t = q.shape[:2]
    kk = idx.shape[-1]
    ksg = ks.transpose(0, 3, 1, 2, 4)
    vsg = vs.transpose(0, 3, 1, 2, 4)
    bi = jnp.arange(b)[:, None, None, None]
    gi = jnp.arange(NSA_KV_HEADS)[None, None, :, None]
    kg = ksg[bi, gi, idx]
    vg = vsg[bi, gi, idx]
    s_pos = idx[..., None] * L_SEL + jnp.arange(L_SEL, dtype=jnp.int32)
    dist = q_pos[None, :, None, None, None] - s_pos
    mask = valid[..., None] & (dist >= 0)
    tb = rel_table.reshape(N_BUCKETS, NSA_KV_HEADS, NSA_QPG).transpose(1, 0, 2)
    bias = tb[gi[..., None], t5_bucket(dist)]
    logits = jnp.einsum('btgqd,btgksd->btgqks', q, kg) * SCALE + jnp.moveaxis(bias, -1, 3)
    logits = logits.reshape(b, t, NSA_KV_HEADS, NSA_QPG, kk * L_SEL)
    p = masked_softmax(logits, mask.reshape(b, t, NSA_KV_HEADS, 1, kk * L_SEL))
    return jnp.einsum('btgqn,btgnd->btgqd', p.astype(vg.dtype), vg.reshape(b, t, NSA_KV_HEADS, kk * L_SEL, HEAD_DIM))


def nsa_global_branches(q, q_pos, rows4, w_cmp, rel_table):
    rows4 = pad_len(rows4, L_SEL)
    b, lp = rows4.shape[:2]
    nsb = lp // L_SEL
    kc = compress(rows4[:, :, 0], w_cmp[0])
    vc = compress(rows4[:, :, 1], w_cmp[1])
    ks = rows4[:, :, 2].reshape(b, nsb, L_SEL, NSA_KV_HEADS, HEAD_DIM)
    vs = rows4[:, :, 3].reshape(b, nsb, L_SEL, NSA_KV_HEADS, HEAD_DIM)
    o_c, p = cmp_branch(q, q_pos, kc, vc, rel_table)
    idx, valid = select_blocks(p, q_pos, nsb)
    return o_c, sel_branch(q, q_pos, idx, valid, ks, vs, rel_table)


def win_branch_dense(q, q_pos, kw, vw, k_pos, rel_table):
    t, l = q.shape[1], kw.shape[1]
    dist = q_pos[:, None] - k_pos[None, :]
    mask = (dist >= 0) & (dist < WINDOW)
    bias = rel_table[t5_bucket(dist)].reshape(t, l, NSA_KV_HEADS, NSA_QPG).transpose(0, 2, 3, 1)
    logits = jnp.einsum('btgqd,blgd->btgql', q, kw) * SCALE + bias
    p = masked_softmax(logits, mask[:, None, None, :])
    return jnp.einsum('btgql,blgd->btgqd', p.astype(vw.dtype), vw)


def retention(q, k, v, s0, chunk):
    b, t = q.shape[:2]
    nc = t // chunk
    lg = jnp.log(1.0 - 2.0 ** (-5.0 - jnp.arange(RET_HEADS, dtype=jnp.float32)))
    i = jnp.arange(chunk, dtype=jnp.float32)
    diff = i[:, None] - i[None, :]
    dmat = jnp.where(diff >= 0, jnp.exp(jnp.maximum(diff, 0.0)[None] * lg[:, None, None]), 0.0)
    q_dec = jnp.exp((i + 1.0)[:, None] * lg[None, :])[None, :, :, None]
    k_dec = jnp.exp((chunk - 1.0 - i)[:, None] * lg[None, :])[None, :, :, None]
    s_dec = jnp.exp(chunk * lg)[None, :, None, None]

    def to_chunks(a):
        return jnp.moveaxis(a.astype(jnp.float32).reshape(b, nc, chunk, RET_HEADS, a.shape[-1]), 1, 0)

    def step(s, xs):
        qc, kc, vc = xs
        att = jnp.einsum('bihd,bjhd->bhij', qc, kc) * dmat
        o = jnp.einsum('bhij,bjhe->bihe', att, vc) + jnp.einsum('bihd,bhde->bihe', qc * q_dec, s)
        s = s * s_dec + jnp.einsum('bjhd,bjhe->bhde', kc * k_dec, vc)
        return s, o

    s, o = lax.scan(step, s0.astype(jnp.float32), (to_chunks(q), to_chunks(k), to_chunks(v)))
    return jnp.moveaxis(o, 0, 1).reshape(b, t, RET_HEADS, RET_DV), s


def _nsa_prompt_kernel(q_ref, gate_ref, kcT_ref, vc_ref, cb_ref, ksT_ref, vs_ref, kwT_ref, vw_ref, sb_ref, wb_ref,
                       o_ref, qa_ref, m_ref, l_ref, acc_ref, oc_ref):
    qi = pl.program_id(2)
    tq = q_ref.shape[3]
    ncb = kcT_ref.shape[3]
    ka = ksT_ref.shape[2]
    t0 = qi * tq
    rows = NSA_QPG * tq

    imp = jnp.zeros((tq, ncb), F32)
    for h in range(NSA_QPG):
        bias = cb_ref[0, h]
        lg = jnp.dot(q_ref[0, 0, h], kcT_ref[0, 0], preferred_element_type=F32) + bias
        mx = jnp.max(lg, axis=-1, keepdims=True)
        e = jnp.where(bias > 0.5 * CMP_MASK_NEG, jnp.exp(lg - mx), 0.0)
        p = e / jnp.maximum(jnp.sum(e, axis=-1, keepdims=True), 1e-30)
        imp = imp + p
        oc_ref[h] = jnp.dot(p.astype(BF16), vc_ref[0, 0], preferred_element_type=F32)

    lane = lax.broadcasted_iota(jnp.int32, (tq, ncb), 1)
    tpos = t0 + lax.broadcasted_iota(jnp.int32, (tq, ncb), 0)
    pair = imp + pltpu.roll(imp, ncb - 1, 1)
    blk = lane >> 1
    cur = tpos >> 6
    forced = (blk == 0) | (blk == cur) | (blk == cur - 1)
    cand = ((lane & 1) == 0) & (blk <= cur)
    score = jnp.where(cand, jnp.where(forced, jnp.inf, pair), NEG_INF)
    lane_f = lane.astype(F32)
    chosen = jnp.zeros((tq, ncb), F32)
    for _ in range(N_SEL):
        mx = jnp.max(score, axis=-1, keepdims=True)
        first = jnp.min(jnp.where(score == mx, lane_f, float(ncb)), axis=-1, keepdims=True)
        hit = (lane_f == first) & (mx > NEG_INF)
        chosen = jnp.where(hit, 1.0, chosen)
        score = jnp.where(hit, NEG_INF, score)
    blockmask = jnp.where(chosen > 0.0, 0.0, MASK_NEG).astype(BF16)

    for h in range(NSA_QPG):
        qa_ref[h * tq:(h + 1) * tq, 0:ncb] = blockmask
        qa_ref[h * tq:(h + 1) * tq, ncb:ncb + HEAD_DIM] = q_ref[0, 0, h]
        if ka > ncb + HEAD_DIM:
            qa_ref[h * tq:(h + 1) * tq, ncb + HEAD_DIM:ka] = jnp.zeros((tq, ka - ncb - HEAD_DIM), BF16)

    m_ref[...] = jnp.full((rows, 1), CMP_MASK_NEG, F32)
    l_ref[...] = jnp.zeros((rows, 1), F32)
    acc_ref[...] = jnp.zeros((rows, HEAD_DIM), F32)

    def softmax_step(s, v):
        m_old = m_ref[...]
        m_new = jnp.maximum(m_old, jnp.max(s, axis=-1, keepdims=True))
        p = jnp.exp(s - m_new)
        corr = jnp.exp(m_old - m_new)
        l_ref[...] = corr * l_ref[...] + jnp.sum(p, axis=-1, keepdims=True)
        acc_ref[...] = corr * acc_ref[...] + jnp.dot(p.astype(BF16), v, preferred_element_type=F32)
        m_ref[...] = m_new

    def far_body(c, carry):
        col = pl.multiple_of((c + 1) * tq, tq)
        s = jnp.dot(qa_ref[...], ksT_ref[0, 0, :, pl.ds(col, tq)], preferred_element_type=F32)
        softmax_step(s, vs_ref[0, 0, pl.ds(col, tq), :])
        return carry

    lax.fori_loop(0, jnp.maximum(qi - 1, 0), far_body, 0)

    col0 = pl.multiple_of(t0, tq)
    s = jnp.dot(qa_ref[...], ksT_ref[0, 0, :, pl.ds(col0, 2 * tq)], preferred_element_type=F32)
    jn = lax.broadcasted_iota(jnp.int32, (1, 2 * tq), 1)
    colmask = jnp.where(jn + (t0 - tq) >= 0, 0.0, MASK_NEG)
    s = jnp.concatenate([s[h * tq:(h + 1) * tq] + (sb_ref[0, h] + colmask) for h in range(NSA_QPG)], axis=0)
    softmax_step(s, vs_ref[0, 0, pl.ds(col0, 2 * tq), :])
    o_sel = acc_ref[...] / l_ref[...]

    nw = WINDOW + tq
    qw = qa_ref[:, ncb:ncb + HEAD_DIM]
    sw = jnp.dot(qw, kwT_ref[0, 0, :, pl.ds(col0, nw)], preferred_element_type=F32)
    jw = lax.broadcasted_iota(jnp.int32, (1, nw), 1)
    wmask = jnp.where(jw + (t0 - WINDOW) >= 0, 0.0, MASK_NEG)
    sw = jnp.concatenate([sw[h * tq:(h + 1) * tq] + (wb_ref[0, h] + wmask) for h in range(NSA_QPG)], axis=0)
    mw = jnp.max(sw, axis=-1, keepdims=True)
    pw = jnp.exp(sw - mw)
    lw = jnp.sum(pw, axis=-1, keepdims=True)
    o_win = jnp.dot(pw.astype(BF16), vw_ref[0, 0, pl.ds(col0, nw), :], preferred_element_type=F32) / lw

    g = jax.nn.sigmoid(gate_ref[0, 0])
    for h in range(NSA_QPG):
        o_h = (g[:, 3 * h:3 * h + 1] * oc_ref[h]
               + g[:, 3 * h + 1:3 * h + 2] * o_sel[h * tq:(h + 1) * tq]
               + g[:, 3 * h + 2:3 * h + 3] * o_win[h * tq:(h + 1) * tq])
        o_ref[0, :, h * HEAD_DIM:(h + 1) * HEAD_DIM] = o_h


def nsa_bias_tables(rel_bias, t, tq):
    ncb = t // L_CMP

    def heads_first(tab):
        return tab.transpose(2, 0, 1).reshape(NSA_KV_HEADS, NSA_QPG, tab.shape[0], tab.shape[1])

    pos = jnp.arange(t, dtype=jnp.int32)
    end = jnp.arange(ncb, dtype=jnp.int32) * L_CMP + (L_CMP - 1)
    dist = pos[:, None] - end[None, :]
    cb = jnp.where((dist >= 0)[..., None], rel_bias[t5_bucket(dist)], CMP_MASK_NEG)
    i = jnp.arange(tq, dtype=jnp.int32)
    dist = i[:, None] + tq - jnp.arange(2 * tq, dtype=jnp.int32)[None, :]
    sb = jnp.where((dist >= 0)[..., None], rel_bias[t5_bucket(dist)] - rel_bias[N_BUCKETS - 1], MASK_NEG)
    dist = i[:, None] + WINDOW - jnp.arange(WINDOW + tq, dtype=jnp.int32)[None, :]
    wb = jnp.where(((dist >= 0) & (dist < WINDOW))[..., None], rel_bias[t5_bucket(dist)], MASK_NEG)
    return heads_first(cb), heads_first(sb), heads_first(wb)


def nsa_prompt(q, kv, gates, kc, vc, tables):
    b, t = q.shape[:2]
    tq = NSA_TQ
    ncb = t // L_CMP
    ka = -(-(ncb + HEAD_DIM) // LANES) * LANES
    cb, sb, wb = tables
    q4 = (q * SCALE).astype(BF16).reshape(b, t, NSA_KV_HEADS, NSA_QPG, HEAD_DIM).transpose(0, 2, 3, 1, 4)
    g4 = gates.reshape(b, t, NSA_KV_HEADS, NSA_QPG * 3).transpose(0, 2, 1, 3)
    kcT = kc.astype(BF16).transpose(0, 2, 3, 1)
    vcg = vc.astype(BF16).transpose(0, 2, 1, 3)
    kvb = kv.astype(BF16)
    onehot = (2 * (jnp.arange(t, dtype=jnp.int32) // L_SEL)[None, :] == jnp.arange(ncb, dtype=jnp.int32)[:, None]).astype(BF16)
    ks_t = kvb[:, :, 2].transpose(0, 2, 3, 1)
    ksT = jnp.concatenate([jnp.broadcast_to(onehot, (b, NSA_KV_HEADS, ncb, t)), ks_t,
                           jnp.zeros((b, NSA_KV_HEADS, ka - ncb - HEAD_DIM, t), BF16)], axis=2)
    ksT = jnp.pad(ksT, ((0, 0), (0, 0), (0, 0), (tq, 0)))
    vs = jnp.pad(kvb[:, :, 3].transpose(0, 2, 1, 3), ((0, 0), (0, 0), (tq, 0), (0, 0)))
    kwT = jnp.pad(kvb[:, :, 4].transpose(0, 2, 3, 1), ((0, 0), (0, 0), (0, 0), (WINDOW, 0)))
    vw = jnp.pad(kvb[:, :, 5].transpose(0, 2, 1, 3), ((0, 0), (0, 0), (WINDOW, 0), (0, 0)))
    rows = NSA_QPG * tq

    def per_bg(shape):
        return pl.BlockSpec((1, 1) + shape, lambda bi, gi, qi: (bi, gi, 0, 0))

    return pl.pallas_call(
        _nsa_prompt_kernel,
        out_shape=jax.ShapeDtypeStruct((b, t, NSA_HEADS * HEAD_DIM), F32),
        grid=(b, NSA_KV_HEADS, t // tq),
        in_specs=[pl.BlockSpec((1, 1, NSA_QPG, tq, HEAD_DIM), lambda bi, gi, qi: (bi, gi, 0, qi, 0)),
                  pl.BlockSpec((1, 1, tq, NSA_QPG * 3), lambda bi, gi, qi: (bi, gi, qi, 0)),
                  per_bg((HEAD_DIM, ncb)), per_bg((ncb, HEAD_DIM)),
                  pl.BlockSpec((1, NSA_QPG, tq, ncb), lambda bi, gi, qi: (gi, 0, qi, 0)),
                  per_bg((ka, tq + t)), per_bg((tq + t, HEAD_DIM)),
                  per_bg((HEAD_DIM, WINDOW + t)), per_bg((WINDOW + t, HEAD_DIM)),
                  pl.BlockSpec((1, NSA_QPG, tq, 2 * tq), lambda bi, gi, qi: (gi, 0, 0, 0)),
                  pl.BlockSpec((1, NSA_QPG, tq, WINDOW + tq), lambda bi, gi, qi: (gi, 0, 0, 0))],
        out_specs=pl.BlockSpec((1, tq, NSA_QPG * HEAD_DIM), lambda bi, gi, qi: (bi, qi, gi)),
        scratch_shapes=[pltpu.VMEM((rows, ka), BF16),
                        pltpu.VMEM((rows, 1), F32), pltpu.VMEM((rows, 1), F32),
                        pltpu.VMEM((rows, HEAD_DIM), F32),
                        pltpu.VMEM((NSA_QPG, tq, HEAD_DIM), F32)],
        compiler_params=pltpu.CompilerParams(dimension_semantics=("arbitrary", "arbitrary", "arbitrary"),
                                             vmem_limit_bytes=VMEM_LIMIT),
        name="nsa_prompt",
    )(q4, g4, kcT, vcg, cb, ksT, vs, kwT, vw, sb, wb)


_PEER_CANDS = [(a, b) for a in range(PEER_TOPK) for b in range(PEER_TOPK) if (a + 1) * (b + 1) <= PEER_TOPK]
_PEER_NCAND = len(_PEER_CANDS)
_PEER_NCAND_PAD = -(-_PEER_NCAND // 8) * 8
_PEER_GROUP_START = [min(c for c, (a, _) in enumerate(_PEER_CANDS) if a == aa) for aa in range(PEER_TOPK)]
_PEER_GROUP_LEN = [sum(1 for (a, _) in _PEER_CANDS if a == aa) for aa in range(PEER_TOPK)]


def _peer_route_kernel(xT_ref, wqT_ref, keys_ref, r2_ref, beta_ref, alpha_ref, lam_ref,
                       qT_ref, s_ref, rk_ref, vals_ref, cand_ref, sel_ref):
    tl = xT_ref.shape[1]
    qT_ref[...] = jnp.dot(wqT_ref[...], xT_ref[...], preferred_element_type=F32).astype(BF16)
    iota_k = lax.broadcasted_iota(jnp.int32, (N_KEYS, tl), 0).astype(F32)

    def head_body(h, carry):
        for p in range(2):
            row0 = pl.multiple_of(h * PEER_DK + p * (PEER_DK // 2), PEER_DK // 2)
            qs = qT_ref[pl.ds(row0, PEER_DK // 2), :]
            s = jnp.dot(keys_ref[2 * h + p], qs, preferred_element_type=F32)
            s_ref[p] = s
            cur = s
            rk = jnp.full((N_KEYS, tl), float(PEER_TOPK), F32)
            for a in range(PEER_TOPK):
                m = jnp.max(cur, axis=0, keepdims=True)
                idx = jnp.min(jnp.where(cur == m, iota_k, float(N_KEYS)), axis=0, keepdims=True)
                hit = iota_k == idx
                rk = jnp.where(hit, float(a), rk)
                cur = jnp.where(hit, NEG_INF, cur)
                vals_ref[p, a:a + 1, :] = m
            rk_ref[p] = rk
        for c, (a, b) in enumerate(_PEER_CANDS):
            cand_ref[c:c + 1, :] = vals_ref[0, a:a + 1, :] + vals_ref[1, b:b + 1, :]
        if _PEER_NCAND_PAD > _PEER_NCAND:
            cand_ref[_PEER_NCAND:_PEER_NCAND_PAD, :] = jnp.full((_PEER_NCAND_PAD - _PEER_NCAND, tl), NEG_INF, F32)
        ngrp = _PEER_NCAND_PAD // 8
        iota8 = lax.broadcasted_iota(jnp.int32, (8, tl), 0)
        ranks = [jnp.zeros((8, tl), F32) for _ in range(ngrp)]
        for cp in range(_PEER_NCAND):
            rowb = cand_ref[cp:cp + 1, :]
            for k in range(ngrp):
                blk = cand_ref[8 * k:8 * k + 8, :]
                if 8 * k > cp:
                    inc = jnp.where(rowb >= blk, 1.0, 0.0)
                elif 8 * k + 7 < cp:
                    inc = jnp.where(rowb > blk, 1.0, 0.0)
                else:
                    inc = jnp.where(iota8 + 8 * k > cp, jnp.where(rowb >= blk, 1.0, 0.0), jnp.where(rowb > blk, 1.0, 0.0))
                ranks[k] = ranks[k] + inc
        top = cand_ref[0:1, :]
        z = jnp.zeros((1, tl), F32)
        for k in range(ngrp):
            blk = cand_ref[8 * k:8 * k + 8, :]
            selk = ranks[k] < float(PEER_TOPK)
            sel_ref[8 * k:8 * k + 8, :] = jnp.where(selk, 1.0, 0.0)
            z = z + jnp.sum(jnp.where(selk, jnp.exp(blk - top), 0.0), axis=0, keepdims=True)
        rk1 = rk_ref[0]
        lam = jnp.full((N_KEYS, tl), -1.0, F32)
        for a in range(PEER_TOPK):
            g0, gl = _PEER_GROUP_START[a], _PEER_GROUP_LEN[a]
            la = jnp.sum(sel_ref[g0:g0 + gl, :], axis=0, keepdims=True) - 1.0
            lam = jnp.where(rk1 == float(a), la, lam)
        alpha = jnp.where(rk1 < float(PEER_TOPK), jnp.exp(s_ref[0] - vals_ref[0, 0:1, :]), 0.0)
        rk2 = rk_ref[1]
        beta = jnp.where(rk2 < float(PEER_TOPK), jnp.exp(s_ref[1] - vals_ref[1, 0:1, :]), 0.0) / z
        r2_ref[h] = rk2.astype(BF16)
        beta_ref[h] = beta.astype(BF16)
        alpha_ref[h] = alpha
        lam_ref[h] = lam
        return carry

    lax.fori_loop(0, PEER_HEADS, head_body, 0)


def peer_route(xT, wqT, keys2):
    d, n = xT.shape
    tl = LANES
    assert n % tl == 0
    nq = PEER_HEADS * PEER_DK
    out_bf = jax.ShapeDtypeStruct((PEER_HEADS, N_KEYS, n), BF16)
    out_f = jax.ShapeDtypeStruct((PEER_HEADS, N_KEYS, n), F32)
    tab_spec = pl.BlockSpec((PEER_HEADS, N_KEYS, tl), lambda i: (0, 0, i))
    return pl.pallas_call(
        _peer_route_kernel,
        out_shape=(out_bf, out_bf, out_f, out_f),
        grid=(n // tl,),
        in_specs=[pl.BlockSpec((d, tl), lambda i: (0, i)),
                  pl.BlockSpec((nq, d), lambda i: (0, 0)),
                  pl.BlockSpec((2 * PEER_HEADS, N_KEYS, PEER_DK // 2), lambda i: (0, 0, 0))],
        out_specs=(tab_spec, tab_spec, tab_spec, tab_spec),
        scratch_shapes=[pltpu.VMEM((nq, tl), BF16),
                        pltpu.VMEM((2, N_KEYS, tl), F32),
                        pltpu.VMEM((2, N_KEYS, tl), F32),
                        pltpu.VMEM((2, PEER_TOPK, tl), F32),
                        pltpu.VMEM((_PEER_NCAND_PAD, tl), F32),
                        pltpu.VMEM((_PEER_NCAND_PAD, tl), F32)],
        compiler_params=pltpu.CompilerParams(dimension_semantics=("arbitrary",), vmem_limit_bytes=VMEM_LIMIT),
        name="peer_route",
    )(xT, wqT, keys2)


def _gelu_tanh(x):
    return 0.5 * x * (1.0 + jnp.tanh(math.sqrt(2.0 / math.pi) * (x + 0.044715 * (x * x * x))))


def _peer_dense_kernel(xT_ref, u_ref, vT_ref, r2_ref, beta_ref, alpha_ref, lam_ref, yT_ref, a_ref, h_ref):
    j = pl.program_id(1)
    te = u_ref.shape[0]
    tl = xT_ref.shape[1]

    @pl.when(j == 0)
    def _():
        yT_ref[...] = jnp.zeros_like(yT_ref)

    a_ref[...] = jnp.dot(u_ref[...], xT_ref[...], preferred_element_type=F32)
    for r in range(te // N_KEYS):
        g = jnp.zeros((N_KEYS, tl), BF16)
        for h in range(PEER_HEADS):
            lam = lam_ref[h, r:r + 1, :].astype(BF16)
            alp = alpha_ref[h, r:r + 1, :].astype(BF16)
            g = g + jnp.where(r2_ref[h] <= lam, beta_ref[h], jnp.zeros((), BF16)) * alp
        act = _gelu_tanh(a_ref[r * N_KEYS:(r + 1) * N_KEYS, :]).astype(BF16)
        h_ref[r * N_KEYS:(r + 1) * N_KEYS, :] = act * g
    yT_ref[...] += jnp.dot(vT_ref[...], h_ref[...], preferred_element_type=F32)


def peer_dense(xT, u_bf, vT_bf, r2, beta, alpha, lam, tl, te):
    d, n = xT.shape
    e = u_bf.shape[0]
    assert n % tl == 0 and e % te == 0 and te % (8 * N_KEYS) == 0
    tab_spec = pl.BlockSpec((PEER_HEADS, N_KEYS, tl), lambda i, j: (0, 0, i))
    row_spec = pl.BlockSpec((PEER_HEADS, te // N_KEYS, tl), lambda i, j: (0, j, i))
    return pl.pallas_call(
        _peer_dense_kernel,
        out_shape=jax.ShapeDtypeStruct((d, n), F32),
        grid=(n // tl, e // te),
        in_specs=[pl.BlockSpec((d, tl), lambda i, j: (0, i)),
                  pl.BlockSpec((te, d), lambda i, j: (j, 0)),
                  pl.BlockSpec((d, te), lambda i, j: (0, j)),
                  tab_spec, tab_spec, row_spec, row_spec],
        out_specs=pl.BlockSpec((d, tl), lambda i, j: (0, i)),
        scratch_shapes=[pltpu.VMEM((te, tl), F32), pltpu.VMEM((te, tl), BF16)],
        compiler_params=pltpu.CompilerParams(dimension_semantics=("arbitrary", "arbitrary"), vmem_limit_bytes=VMEM_LIMIT),
        name="peer_dense",
    )(xT, u_bf, vT_bf, r2, beta, alpha, lam)


def peer_ffn(h, peer_w):
    wqT, keys2, u_bf, vT_bf = peer_w
    b, t, d = h.shape
    n = b * t
    tl = PEER_TL if n % PEER_TL == 0 else LANES
    npad = -(-n // tl) * tl
    xT = jnp.pad(h.reshape(n, d).astype(BF16).T, ((0, 0), (0, npad - n)))
    r2, beta, alpha, lam = peer_route(xT, wqT, keys2)
    yT = peer_dense(xT, u_bf, vT_bf, r2, beta, alpha, lam, tl, PEER_TE)
    return yT.T[:n].reshape(b, t, d)


def token_mixers(parts, pos, rel_table, w_cmp, conv_w, conv_b, ret_g, past, nsa_tables):
    nq, nkv, ngate, rq, rk, rv, rg, cb, cc, ch = parts
    b, t = nq.shape[:2]
    q = nq.reshape(b, t, NSA_KV_HEADS, NSA_QPG, HEAD_DIM)
    kv = nkv.reshape(b, t, 6, NSA_KV_HEADS, HEAD_DIM)
    if past is None:
        kc = compress(kv[:, :, 0], w_cmp[0])
        vc = compress(kv[:, :, 1], w_cmp[1])
        o_nsa = nsa_prompt(nq, kv, ngate, kc, vc, nsa_tables)
        win_rows = kv[:, t - min(WINDOW, t):, 4:]
        s0 = jnp.zeros((b, RET_HEADS, RET_DK, RET_DV), jnp.float32)
        zbuf = jnp.zeros((b, CONV_W - 1, CONV_CH), ch.dtype)
        chunk = RET_CHUNK
    else:
        past4, win_buf, s0, zbuf, past_len = past
        rows4 = jnp.concatenate([past4.astype(kv.dtype), kv[:, :, :4]], axis=1)
        wrows = jnp.concatenate([win_buf.astype(kv.dtype), kv[:, :, 4:]], axis=1)
        k_pos = past_len - win_buf.shape[1] + jnp.arange(wrows.shape[1], dtype=jnp.int32)
        o_win = win_branch_dense(q, pos, wrows[:, :, 0], wrows[:, :, 1], k_pos, rel_table)
        win_rows = wrows[:, wrows.shape[1] - min(WINDOW, wrows.shape[1]):]
        chunk = t
        o_cmp, o_sel = nsa_global_branches(q, pos, rows4, w_cmp, rel_table)
        g = jax.nn.sigmoid(ngate.astype(jnp.float32)).reshape(b, t, NSA_KV_HEADS, NSA_QPG, 3).astype(o_cmp.dtype)
        o_nsa = (g[..., 0:1] * o_cmp + g[..., 1:2] * o_sel + g[..., 2:3] * o_win).reshape(b, t, NSA_WIDTH)
    rqh = rotary(rq.reshape(b, t, RET_HEADS, RET_DK), pos)
    rkh = rotary(rk.reshape(b, t, RET_HEADS, RET_DK), pos) * (RET_DK ** -0.5)
    rvh = rv.reshape(b, t, RET_HEADS, RET_DV)
    o_r, s_new = retention(rqh, rkh, rvh, s0, chunk)
    mu = jnp.mean(o_r, -1, keepdims=True)
    var = jnp.mean(jnp.square(o_r - mu), -1, keepdims=True)
    on = (o_r - mu) * lax.rsqrt(var + LN_EPS) * ret_g.reshape(RET_HEADS, RET_DV)
    o_ret = (on.reshape(b, t, RET_WIDTH) * jax.nn.silu(rg.astype(jnp.float32))).astype(nq.dtype)
    z = cc * ch
    zp = jnp.concatenate([zbuf.astype(z.dtype), z], axis=1)
    y = conv_b + sum(zp[:, j:j + t] * conv_w[j] for j in range(CONV_W))
    o_conv = (cb * y).astype(nq.dtype)
    mix = jnp.concatenate([o_nsa, o_ret, o_conv], axis=-1)
    return mix, (kv[:, :, :4], win_rows, s_new, zp[:, t:])


def kernel(x_prompt, x_sample, cache_nsa_kv, cache_win_kv, state_ret, state_conv, page_table, c_prompt, c_sample, rel_bias, w_ada, b_ada, w_in, w_cmp, conv_w, conv_b, ret_norm_g, w_out, ln1_g, ln1_b, ln2_g, ln2_b, peer_wq, peer_keys, peer_u, peer_v):
    alpha = (2.0 * DEPTH) ** 0.25
    n_pages = page_table.shape[1]
    past_len = n_pages * PAGE_SIZE
    sp = x_prompt.shape[1]
    bs, ts = x_sample.shape[:2]
    pos_p = jnp.arange(sp, dtype=jnp.int32)
    pos_s = past_len + jnp.arange(ts, dtype=jnp.int32)
    nsa_tables = nsa_bias_tables(rel_bias, sp, NSA_TQ)

    def run_layer(x, c, l, pos, past, peer_w):
        m = (jax.nn.silu(c) @ w_ada[l] + b_ada[l]).reshape(c.shape[0], 6, 1, D_MODEL)
        h = x * (1.0 + m[:, 1]) + m[:, 0]
        parts = split_proj(h @ w_in[l])
        mix, st = token_mixers(parts, pos, rel_bias, w_cmp[l], conv_w[l], conv_b[l], ret_norm_g[l], past, nsa_tables)
        x = layer_norm(alpha * x + (1.0 + m[:, 2]) * (mix @ w_out[l]), ln1_g[l], ln1_b[l])
        h = x * (1.0 + m[:, 4]) + m[:, 3]
        y = peer_ffn(h, peer_w)
        x = layer_norm(alpha * x + (1.0 + m[:, 5]) * y, ln2_g[l], ln2_b[l])
        return x, st

    xp, xs = x_prompt, x_sample
    nkv_p, nkv_s, win_p, win_s, ret_p, ret_s, conv_p, conv_s = [], [], [], [], [], [], [], []
    for l in range(DEPTH):
        peer_w = (peer_wq[l].T.astype(BF16),
                  peer_keys[l].reshape(2 * PEER_HEADS, N_KEYS, PEER_DK // 2).astype(BF16),
                  peer_u[l].astype(BF16),
                  peer_v[l].T.astype(BF16))
        xp, st_p = run_layer(xp, c_prompt, l, pos_p, None, peer_w)
        pages = cache_nsa_kv[l].reshape(cache_nsa_kv.shape[1], -1)
        past4 = jnp.take(pages, page_table.reshape(-1), axis=0).reshape(bs, past_len, 4, NSA_KV_HEADS, HEAD_DIM)
        xs, st_s = run_layer(xs, c_sample, l, pos_s, (past4, cache_win_kv[l], state_ret[l], state_conv[l], past_len), peer_w)
        nkv_p.append(st_p[0])
        win_p.append(st_p[1])
        ret_p.append(st_p[2])
        conv_p.append(st_p[3])
        nkv_s.append(st_s[0])
        win_s.append(st_s[1])
        ret_s.append(st_s[2])
        conv_s.append(st_s[3])
    return (xp, xs, jnp.stack(nkv_p), jnp.stack(nkv_s), jnp.stack(win_p), jnp.stack(win_s), jnp.stack(ret_p), jnp.stack(ret_s), jnp.stack(conv_p), jnp.stack(conv_s))
```

```python
import math
import functools
import jax
import jax.numpy as jnp
from jax import lax
import numpy as np
from jax.experimental import pallas as pl
from jax.experimental.pallas import tpu as pltpu

D_MODEL = 1024
BATCH = 4
SEQ = 4096
DEPTH = 2
DEC_BATCH = 32
DEC_SEQ = 1
PAST_LEN = 8192
PAGE_SIZE = 128

HEAD_DIM = 64
NSA_WIDTH = D_MODEL // 2
NSA_HEADS = NSA_WIDTH // HEAD_DIM
NSA_KV_HEADS = 2
NSA_QPG = NSA_HEADS // NSA_KV_HEADS
KV_WIDTH = NSA_KV_HEADS * HEAD_DIM
SCALE = HEAD_DIM ** -0.5
L_CMP = 32
L_SEL = 64
N_SEL = 16
WINDOW = 512
Q_BLOCK = 128
SEL_Q_BLOCK = 64
N_BUCKETS = 32
REL_MAX_DIST = 128
RET_WIDTH = D_MODEL // 4
RET_DK = 64
RET_DV = 64
RET_HEADS = RET_WIDTH // RET_DV
RET_CHUNK = 128
ROPE_BASE = 10000.0
CONV_CH = D_MODEL // 4
CONV_W = 3
MIX_WIDTH = NSA_WIDTH + RET_WIDTH + CONV_CH
PEER_HEADS = 8
PEER_DK = 256
N_KEYS = 128
N_EXPERTS = N_KEYS * N_KEYS
PEER_TOPK = 16
PEER_CHUNK = 256
LN_EPS = 1e-5
SPLIT_SIZES = (NSA_WIDTH, 6 * KV_WIDTH, 3 * NSA_HEADS, RET_HEADS * RET_DK, RET_HEADS * RET_DK, RET_WIDTH, RET_WIDTH, CONV_CH, CONV_CH, CONV_CH)
N_IN = sum(SPLIT_SIZES)

F32 = jnp.float32
BF16 = jnp.bfloat16
NEG_INF = float('-inf')
MASK_NEG = -1e9
CMP_MASK_NEG = -1e30
NSA_TQ = 256
NSA_RB = 128
ROW_WIDTH = 4 * KV_WIDTH
HALF_WIDTH = 2 * KV_WIDTH
PEER_TL = 512
PEER_TE = 1024
LANES = 128
VMEM_LIMIT = 56 * 1024 * 1024


def _ln_kernel(x_ref, g_ref, b_ref, o_ref):
    x = x_ref[...]
    mu = jnp.mean(x, -1, keepdims=True)
    xc = x - mu
    var = jnp.mean(xc * xc, -1, keepdims=True)
    o_ref[...] = xc * lax.rsqrt(var + LN_EPS) * g_ref[...] + b_ref[...]


def layer_norm(x, g, b):
    shp = x.shape
    x2 = x.reshape(-1, shp[-1])
    n = x2.shape[0]
    tm = min(n, 512)
    out = pl.pallas_call(
        _ln_kernel,
        out_shape=jax.ShapeDtypeStruct(x2.shape, jnp.float32),
        grid=(n // tm,),
        in_specs=[pl.BlockSpec((tm, shp[-1]), lambda i: (i, 0)),
                  pl.BlockSpec((1, shp[-1]), lambda i: (0, 0)),
                  pl.BlockSpec((1, shp[-1]), lambda i: (0, 0))],
        out_specs=pl.BlockSpec((tm, shp[-1]), lambda i: (i, 0)),
        name="layer_norm",
    )(x2, g.reshape(1, -1), b.reshape(1, -1))
    return out.reshape(shp)


def t5_bucket(dist):
    n = jnp.maximum(dist, 0)
    exact = N_BUCKETS // 2
    nf = jnp.maximum(n, exact).astype(jnp.float32)
    big = exact + (jnp.log(nf / exact) / math.log(REL_MAX_DIST / exact) * (N_BUCKETS - exact)).astype(jnp.int32)
    return jnp.where(n < exact, n, jnp.minimum(big, N_BUCKETS - 1))


def rotary(x, pos):
    half = x.shape[-1] // 2
    inv = ROPE_BASE ** (-jnp.arange(half, dtype=jnp.float32) / half)
    ang = pos.astype(jnp.float32)[:, None] * inv[None, :]
    cos = jnp.cos(ang)[None, :, None, :]
    sin = jnp.sin(ang)[None, :, None, :]
    x1, x2 = x[..., :half], x[..., half:]
    return jnp.concatenate([x1 * cos - x2 * sin, x1 * sin + x2 * cos], axis=-1)


def split_proj(p):
    outs, start = [], 0
    for size in SPLIT_SIZES:
        outs.append(p[..., start:start + size])
        start += size
    return outs


def compress(rows, w):
    b, lp, g, d = rows.shape
    blk = rows.reshape(b, lp // L_CMP, L_CMP, g, d).transpose(0, 1, 3, 2, 4).reshape(b, lp // L_CMP, g, L_CMP * d)
    return blk @ w


def retention(q, k, v, s0, chunk):
    b, t = q.shape[:2]
    nc = t // chunk
    lg = jnp.log(1.0 - 2.0 ** (-5.0 - jnp.arange(RET_HEADS, dtype=jnp.float32)))
    i = jnp.arange(chunk, dtype=jnp.float32)
    diff = i[:, None] - i[None, :]
    dmat = jnp.where(diff >= 0, jnp.exp(jnp.maximum(diff, 0.0)[None] * lg[:, None, None]), 0.0)
    q_dec = jnp.exp((i + 1.0)[:, None] * lg[None, :])[None, :, :, None]
    k_dec = jnp.exp((chunk - 1.0 - i)[:, None] * lg[None, :])[None, :, :, None]
    s_dec = jnp.exp(chunk * lg)[None, :, None, None]

    def to_chunks(a):
        return jnp.moveaxis(a.astype(jnp.float32).reshape(b, nc, chunk, RET_HEADS, a.shape[-1]), 1, 0)

    def step(s, xs):
        qc, kc, vc = xs
        att = jnp.einsum('bihd,bjhd->bhij', qc, kc) * dmat
        o = jnp.einsum('bhij,bjhe->bihe', att, vc) + jnp.einsum('bihd,bhde->bihe', qc * q_dec, s)
        s = s * s_dec + jnp.einsum('bjhd,bjhe->bhde', kc * k_dec, vc)
        return s, o

    s, o = lax.scan(step, s0.astype(jnp.float32), (to_chunks(q), to_chunks(k), to_chunks(v)))
    return jnp.moveaxis(o, 0, 1).reshape(b, t, RET_HEADS, RET_DV), s


def _nsa_prompt_kernel(q_ref, gate_ref, kcT_ref, vc_ref, cb_ref, ksT_ref, vs_ref, kwT_ref, vw_ref, sb_ref, wb_ref,
                       o_ref, qa_ref, m_ref, acc_ref, oc_ref, osel_ref, owin_ref):
    qi = pl.program_id(2)
    tq = q_ref.shape[3]
    ncb = kcT_ref.shape[3]
    ka = ksT_ref.shape[2]
    t0 = qi * tq
    rows = NSA_QPG * tq

    imp = jnp.zeros((tq, ncb), F32)
    for h in range(NSA_QPG):
        bias = cb_ref[0, h]
        lg = jnp.dot(q_ref[0, 0, h], kcT_ref[0, 0], preferred_element_type=F32) + bias
        mx = jnp.max(lg, axis=-1, keepdims=True)
        e = jnp.where(bias > 0.5 * CMP_MASK_NEG, jnp.exp(lg - mx), 0.0)
        p = e / jnp.maximum(jnp.sum(e, axis=-1, keepdims=True), 1e-30)
        imp = imp + p
        oc_ref[h] = jnp.dot(p.astype(BF16), vc_ref[0, 0], preferred_element_type=F32)

    lane = lax.broadcasted_iota(jnp.int32, (tq, ncb), 1)
    tpos = t0 + lax.broadcasted_iota(jnp.int32, (tq, ncb), 0)
    pair = imp + pltpu.roll(imp, ncb - 1, 1)
    blk = lane >> 1
    cur = tpos >> 6
    forced = (blk == 0) | (blk == cur) | (blk == cur - 1)
    cand = ((lane & 1) == 0) & (blk <= cur)
    score = jnp.where(cand, jnp.where(forced, jnp.inf, pair), NEG_INF)
    lane_f = lane.astype(F32)
    chosen = jnp.zeros((tq, ncb), F32)
    for _ in range(N_SEL):
        mx = jnp.max(score, axis=-1, keepdims=True)
        first = jnp.min(jnp.where(score == mx, lane_f, float(ncb)), axis=-1, keepdims=True)
        hit = (lane_f == first) & (mx > NEG_INF)
        chosen = jnp.where(hit, 1.0, chosen)
        score = jnp.where(hit, NEG_INF, score)
    blockmask = jnp.where(chosen > 0.0, 0.0, MASK_NEG).astype(BF16)

    for h in range(NSA_QPG):
        qa_ref[h * tq:(h + 1) * tq, 0:ncb] = blockmask
        qa_ref[h * tq:(h + 1) * tq, ncb:ncb + HEAD_DIM] = q_ref[0, 0, h]
        if ka > ncb + HEAD_DIM:
            qa_ref[h * tq:(h + 1) * tq, ncb + HEAD_DIM:ka] = jnp.zeros((tq, ka - ncb - HEAD_DIM), BF16)

    nrb = rows // NSA_RB
    per_head = tq // NSA_RB

    def reset():
        m_ref[...] = jnp.full((rows, LANES), CMP_MASK_NEG, F32)
        acc_ref[...] = jnp.zeros((rows, LANES), F32)

    def attn_step(rb, qrb, k_t, v, bias):
        r0 = rb * NSA_RB
        s = jnp.dot(qrb, k_t, preferred_element_type=F32)
        if bias is not None:
            s = s + bias
        parts = [s[:, i * LANES:(i + 1) * LANES] for i in range(s.shape[1] // LANES)]
        red = parts[0]
        for part in parts[1:]:
            red = jnp.maximum(red, part)
        m_old = m_ref[r0:r0 + NSA_RB, :]
        m_new = jnp.maximum(m_old, jnp.max(red, axis=-1, keepdims=True))
        p = jnp.concatenate([jnp.exp(part - m_new) for part in parts], axis=1).astype(BF16)
        acc_ref[r0:r0 + NSA_RB, :] = (jnp.exp(m_old - m_new) * acc_ref[r0:r0 + NSA_RB, :]
                                      + jnp.dot(p, v, preferred_element_type=F32))
        m_ref[r0:r0 + NSA_RB, :] = m_new

    def finish(dst_ref):
        acc = acc_ref[...]
        dst_ref[...] = acc[:, 0:HEAD_DIM] / acc[:, HEAD_DIM:HEAD_DIM + 1]

    reset()

    def far_body(c, carry):
        col = pl.multiple_of((c + 1) * tq, tq)
        k_t = ksT_ref[0, 0, :, pl.ds(col, tq)]
        v = vs_ref[0, 0, pl.ds(col, tq), :]
        for rb in range(nrb):
            attn_step(rb, qa_ref[rb * NSA_RB:(rb + 1) * NSA_RB, :], k_t, v, None)
        return carry

    lax.fori_loop(0, jnp.maximum(qi - 1, 0), far_body, 0)

    col0 = pl.multiple_of(t0, tq)
    for c in range(2):
        colc = pl.multiple_of(col0 + c * tq, tq)
        k_t = ksT_ref[0, 0, :, pl.ds(colc, tq)]
        v = vs_ref[0, 0, pl.ds(colc, tq), :]
        jn = lax.broadcasted_iota(jnp.int32, (1, tq), 1) + c * tq
        colmask = jnp.where(jn + (t0 - tq) >= 0, 0.0, MASK_NEG)
        for rb in range(nrb):
            h, part = rb // per_head, rb % per_head
            bias = sb_ref[0, h, part * NSA_RB:(part + 1) * NSA_RB, c * tq:(c + 1) * tq] + colmask
            attn_step(rb, qa_ref[rb * NSA_RB:(rb + 1) * NSA_RB, :], k_t, v, bias)
    finish(osel_ref)

    reset()
    for c in range((WINDOW + tq) // tq):
        colc = pl.multiple_of(col0 + c * tq, tq)
        k_t = kwT_ref[0, 0, :, pl.ds(colc, tq)]
        v = vw_ref[0, 0, pl.ds(colc, tq), :]
        jw = lax.broadcasted_iota(jnp.int32, (1, tq), 1) + c * tq
        wmask = jnp.where(jw + (t0 - WINDOW) >= 0, 0.0, MASK_NEG)
        for rb in range(nrb):
            h, part = rb // per_head, rb % per_head
            bias = wb_ref[0, h, part * NSA_RB:(part + 1) * NSA_RB, c * tq:(c + 1) * tq] + wmask
            attn_step(rb, qa_ref[rb * NSA_RB:(rb + 1) * NSA_RB, ncb:ncb + HEAD_DIM], k_t, v, bias)
    finish(owin_ref)

    g = jax.nn.sigmoid(gate_ref[0, 0])
    for h in range(NSA_QPG):
        o_h = (g[:, 3 * h:3 * h + 1] * oc_ref[h]
               + g[:, 3 * h + 1:3 * h + 2] * osel_ref[h * tq:(h + 1) * tq, :]
               + g[:, 3 * h + 2:3 * h + 3] * owin_ref[h * tq:(h + 1) * tq, :])
        o_ref[0, :, h * HEAD_DIM:(h + 1) * HEAD_DIM] = o_h


def nsa_bias_tables(rel_bias, t, tq):
    ncb = t // L_CMP

    def heads_first(tab):
        return tab.transpose(2, 0, 1).reshape(NSA_KV_HEADS, NSA_QPG, tab.shape[0], tab.shape[1])

    pos = jnp.arange(t, dtype=jnp.int32)
    end = jnp.arange(ncb, dtype=jnp.int32) * L_CMP + (L_CMP - 1)
    dist = pos[:, None] - end[None, :]
    cb = jnp.where((dist >= 0)[..., None], rel_bias[t5_bucket(dist)], CMP_MASK_NEG)
    i = jnp.arange(tq, dtype=jnp.int32)
    dist = i[:, None] + tq - jnp.arange(2 * tq, dtype=jnp.int32)[None, :]
    sb = jnp.where((dist >= 0)[..., None], rel_bias[t5_bucket(dist)] - rel_bias[N_BUCKETS - 1], MASK_NEG)
    dist = i[:, None] + WINDOW - jnp.arange(WINDOW + tq, dtype=jnp.int32)[None, :]
    wb = jnp.where(((dist >= 0) & (dist < WINDOW))[..., None], rel_bias[t5_bucket(dist)], MASK_NEG)
    return heads_first(cb), heads_first(sb), heads_first(wb)


def nsa_prompt(q, kv, gates, kc, vc, tables):
    b, t = q.shape[:2]
    tq = NSA_TQ
    ncb = t // L_CMP
    ka = -(-(ncb + HEAD_DIM) // LANES) * LANES
    cb, sb, wb = tables
    q4 = (q * SCALE).astype(BF16).reshape(b, t, NSA_KV_HEADS, NSA_QPG, HEAD_DIM).transpose(0, 2, 3, 1, 4)
    g4 = gates.reshape(b, t, NSA_KV_HEADS, NSA_QPG * 3).transpose(0, 2, 1, 3)
    kcT = kc.astype(BF16).transpose(0, 2, 3, 1)
    vcg = vc.astype(BF16).transpose(0, 2, 1, 3)
    kvb = kv.astype(BF16)
    onehot = (2 * (jnp.arange(t, dtype=jnp.int32) // L_SEL)[None, :] == jnp.arange(ncb, dtype=jnp.int32)[:, None]).astype(BF16)
    ks_t = kvb[:, :, 2].transpose(0, 2, 3, 1)
    ksT = jnp.concatenate([jnp.broadcast_to(onehot, (b, NSA_KV_HEADS, ncb, t)), ks_t,
                           jnp.zeros((b, NSA_KV_HEADS, ka - ncb - HEAD_DIM, t), BF16)], axis=2)
    ksT = jnp.pad(ksT, ((0, 0), (0, 0), (0, 0), (tq, 0)))
    def with_ones(v):
        one = jnp.ones(v.shape[:-1] + (1,), BF16)
        return jnp.concatenate([v, one, jnp.zeros(v.shape[:-1] + (LANES - HEAD_DIM - 1,), BF16)], axis=-1)

    vs = jnp.pad(with_ones(kvb[:, :, 3].transpose(0, 2, 1, 3)), ((0, 0), (0, 0), (tq, 0), (0, 0)))
    kwT = jnp.pad(kvb[:, :, 4].transpose(0, 2, 3, 1), ((0, 0), (0, 0), (0, 0), (WINDOW, 0)))
    vw = jnp.pad(with_ones(kvb[:, :, 5].transpose(0, 2, 1, 3)), ((0, 0), (0, 0), (WINDOW, 0), (0, 0)))
    rows = NSA_QPG * tq

    def per_bg(shape):
        return pl.BlockSpec((1, 1) + shape, lambda bi, gi, qi: (bi, gi, 0, 0))

    return pl.pallas_call(
        _nsa_prompt_kernel,
        out_shape=jax.ShapeDtypeStruct((b, t, NSA_HEADS * HEAD_DIM), F32),
        grid=(b, NSA_KV_HEADS, t // tq),
        in_specs=[pl.BlockSpec((1, 1, NSA_QPG, tq, HEAD_DIM), lambda bi, gi, qi: (bi, gi, 0, qi, 0)),
                  pl.BlockSpec((1, 1, tq, NSA_QPG * 3), lambda bi, gi, qi: (bi, gi, qi, 0)),
                  per_bg((HEAD_DIM, ncb)), per_bg((ncb, HEAD_DIM)),
                  pl.BlockSpec((1, NSA_QPG, tq, ncb), lambda bi, gi, qi: (gi, 0, qi, 0)),
                  per_bg((ka, tq + t)), per_bg((tq + t, LANES)),
                  per_bg((HEAD_DIM, WINDOW + t)), per_bg((WINDOW + t, LANES)),
                  pl.BlockSpec((1, NSA_QPG, tq, 2 * tq), lambda bi, gi, qi: (gi, 0, 0, 0)),
                  pl.BlockSpec((1, NSA_QPG, tq, WINDOW + tq), lambda bi, gi, qi: (gi, 0, 0, 0))],
        out_specs=pl.BlockSpec((1, tq, NSA_QPG * HEAD_DIM), lambda bi, gi, qi: (bi, qi, gi)),
        scratch_shapes=[pltpu.VMEM((rows, ka), BF16),
                        pltpu.VMEM((rows, LANES), F32), pltpu.VMEM((rows, LANES), F32),
                        pltpu.VMEM((NSA_QPG, tq, HEAD_DIM), F32),
                        pltpu.VMEM((rows, HEAD_DIM), F32), pltpu.VMEM((rows, HEAD_DIM), F32)],
        compiler_params=pltpu.CompilerParams(dimension_semantics=("arbitrary", "arbitrary", "arbitrary"),
                                             vmem_limit_bytes=VMEM_LIMIT),
        name="nsa_prompt",
    )(q4, g4, kcT, vcg, cb, ksT, vs, kwT, vw, sb, wb)


def _page_copy(cache_ref, buf, sem, layer, page, slot, p, col0, part):
    return pltpu.make_async_copy(cache_ref.at[layer, page, :, pl.ds(col0 + part * KV_WIDTH, KV_WIDTH)],
                                 buf.at[slot, part, pl.ds(p * PAGE_SIZE, PAGE_SIZE), :], sem.at[slot])


def _gather_pages(pt_ref, cache_ref, buf, sem, layer, col0):
    s = pl.program_id(0)
    n_pages = pt_ref.shape[1]
    slot = lax.rem(s, 2)

    def start(seq, sl):
        for p in range(n_pages):
            for part in range(2):
                _page_copy(cache_ref, buf, sem, layer, pt_ref[seq, p], sl, p, col0, part).start()

    @pl.when(s == 0)
    def _():
        start(0, 0)

    @pl.when(s + 1 < pl.num_programs(0))
    def _():
        start(s + 1, 1 - slot)

    for p in range(n_pages):
        for part in range(2):
            _page_copy(cache_ref, buf, sem, layer, 0, slot, p, col0, part).wait()
    return slot


def _decode_compress_kernel(pt_ref, cache_ref, w_ref, out_ref, buf, sem, *, layer):
    slot = _gather_pages(pt_ref, cache_ref, buf, sem, layer, 0)
    ncb = out_ref.shape[1]
    for part in range(2):
        acc = jnp.zeros((ncb, KV_WIDTH), F32)
        for r in range(L_CMP):
            x = buf[slot, part, pl.ds(r, ncb, stride=L_CMP), :]
            acc = acc + jnp.dot(x.astype(BF16), w_ref[r, part], preferred_element_type=F32)
        out_ref[0, :, part * KV_WIDTH:(part + 1) * KV_WIDTH] = acc


def decode_compress(page_table, cache, w_bd, layer):
    bs, n_pages = page_table.shape
    past = n_pages * PAGE_SIZE
    ncb = past // L_CMP
    return pl.pallas_call(
        functools.partial(_decode_compress_kernel, layer=layer),
        out_shape=jax.ShapeDtypeStruct((bs, ncb, HALF_WIDTH), F32),
        grid_spec=pltpu.PrefetchScalarGridSpec(
            num_scalar_prefetch=1,
            grid=(bs,),
            in_specs=[pl.BlockSpec(memory_space=pl.ANY),
                      pl.BlockSpec((L_CMP, 2, KV_WIDTH, KV_WIDTH), lambda s, pt: (0, 0, 0, 0))],
            out_specs=pl.BlockSpec((1, ncb, HALF_WIDTH), lambda s, pt: (s, 0, 0)),
            scratch_shapes=[pltpu.VMEM((2, 2, past, KV_WIDTH), F32), pltpu.SemaphoreType.DMA((2,))]),
        compiler_params=pltpu.CompilerParams(dimension_semantics=("arbitrary",), vmem_limit_bytes=VMEM_LIMIT),
        name="decode_compress",
    )(page_table, cache, w_bd)


def _dot_nt(a, b):
    return lax.dot_general(a, b, (((1,), (1,)), ((), ())), preferred_element_type=F32)


def _decode_attend_kernel(pt_ref, cache_ref, q_ref, gate_ref, kcvc_ref, new_ref, win_ref, cb_ref, sb_ref, wb_ref,
                          rel0_ref, onehot_ref, hmask_ref, o_ref, buf, sem, *, layer):
    slot = _gather_pages(pt_ref, cache_ref, buf, sem, layer, HALF_WIDTH)
    q = q_ref[0]
    qf = q.astype(F32)
    ncb = kcvc_ref.shape[1]
    rel0 = rel0_ref[:, 0:1]

    def bf_round(x):
        return x.astype(BF16).astype(F32)

    kc = kcvc_ref[0, :, 0:KV_WIDTH].astype(BF16)
    vc = kcvc_ref[0, :, KV_WIDTH:HALF_WIDTH].astype(BF16)
    lg = _dot_nt(q, kc) + cb_ref[...]
    e = jnp.exp(lg - jnp.max(lg, axis=-1, keepdims=True))
    p = e / jnp.sum(e, axis=-1, keepdims=True)
    o_cmp = jnp.dot(p.astype(BF16), vc, preferred_element_type=F32)

    row = lax.broadcasted_iota(jnp.int32, (NSA_HEADS, ncb), 0)
    lane = lax.broadcasted_iota(jnp.int32, (NSA_HEADS, ncb), 1)
    pg0 = jnp.sum(p[0:NSA_QPG], axis=0, keepdims=True)
    pg1 = jnp.sum(p[NSA_QPG:NSA_HEADS], axis=0, keepdims=True)
    imp = jnp.where(row < NSA_QPG, pg0, pg1)
    pair = imp + pltpu.roll(imp, ncb - 1, 1)
    blk = lane >> 1
    forced = (blk == 0) | (blk == ncb // 2 - 1)
    score = jnp.where((lane & 1) == 0, jnp.where(forced, jnp.inf, pair), NEG_INF)
    lane_f = lane.astype(F32)
    chosen = jnp.zeros((NSA_HEADS, ncb), F32)
    for _ in range(N_SEL - 1):
        mx = jnp.max(score, axis=-1, keepdims=True)
        first = jnp.min(jnp.where(score == mx, lane_f, float(ncb)), axis=-1, keepdims=True)
        hit = lane_f == first
        chosen = jnp.where(hit, 1.0, chosen)
        score = jnp.where(hit, NEG_INF, score)
    blockmask = jnp.where(chosen > 0.0, 0.0, MASK_NEG).astype(BF16)

    ks = buf[slot, 0].astype(BF16)
    vs = buf[slot, 1].astype(BF16)
    s = _dot_nt(q, ks) + jnp.dot(blockmask, onehot_ref[...], preferred_element_type=F32) + sb_ref[...]
    s_new = jnp.sum(qf * bf_round(new_ref[0, 0:1, :]), axis=-1, keepdims=True) + rel0
    m = jnp.maximum(jnp.max(s, axis=-1, keepdims=True), s_new)
    e = jnp.exp(s - m)
    e_new = jnp.exp(s_new - m)
    den = jnp.sum(e, axis=-1, keepdims=True) + e_new
    o_sel = (jnp.dot(e.astype(BF16), vs, preferred_element_type=F32) + e_new * bf_round(new_ref[0, 1:2, :])) / den

    wk = win_ref[0, 0, :, 0:KV_WIDTH].astype(BF16)
    wv = win_ref[0, 0, :, KV_WIDTH:HALF_WIDTH].astype(BF16)
    sw = _dot_nt(q, wk) + wb_ref[...]
    sw_new = jnp.sum(qf * bf_round(new_ref[0, 2:3, :]), axis=-1, keepdims=True) + rel0
    mw = jnp.maximum(jnp.max(sw, axis=-1, keepdims=True), sw_new)
    ew = jnp.exp(sw - mw)
    ew_new = jnp.exp(sw_new - mw)
    denw = jnp.sum(ew, axis=-1, keepdims=True) + ew_new
    o_win = (jnp.dot(ew.astype(BF16), wv, preferred_element_type=F32) + ew_new * bf_round(new_ref[0, 3:4, :])) / denw

    g = jax.nn.sigmoid(gate_ref[0])
    o_ref[0] = (g[:, 0:1] * o_cmp + g[:, 1:2] * o_sel + g[:, 2:3] * o_win) * hmask_ref[...]


def decode_tables(rel_bias, past):
    ncb = past // L_CMP
    end = jnp.arange(ncb, dtype=jnp.int32) * L_CMP + (L_CMP - 1)
    cb = rel_bias[t5_bucket(past - end)].T
    sb = rel_bias[t5_bucket(past - jnp.arange(past, dtype=jnp.int32))].T
    dist = WINDOW - jnp.arange(WINDOW, dtype=jnp.int32)
    wb = jnp.where((dist < WINDOW)[None, :], rel_bias[t5_bucket(dist)].T, MASK_NEG)
    rel0 = jnp.broadcast_to(rel_bias[0][:, None], (NSA_HEADS, LANES))
    onehot = (2 * (jnp.arange(past, dtype=jnp.int32) // L_SEL)[None, :] == jnp.arange(ncb, dtype=jnp.int32)[:, None]).astype(BF16)
    hmask = (jnp.arange(KV_WIDTH, dtype=jnp.int32)[None, :] // HEAD_DIM == jnp.arange(NSA_HEADS, dtype=jnp.int32)[:, None] // NSA_QPG).astype(F32)
    return cb, sb, wb, rel0, onehot, hmask


def compress_block_weights(w_cmp):
    w = w_cmp.reshape(2, L_CMP, HEAD_DIM, HEAD_DIM).transpose(1, 0, 2, 3)
    z = jnp.zeros_like(w)
    return jnp.concatenate([jnp.concatenate([w, z], axis=3), jnp.concatenate([z, w], axis=3)], axis=2).astype(BF16)


def nsa_decode(nq, nkv, ngate, page_table, cache_nsa, cache_win, w_bd, tables, layer):
    bs = nq.shape[0]
    n_pages = page_table.shape[1]
    past = n_pages * PAGE_SIZE
    ncb = past // L_CMP
    cb, sb, wb, rel0, onehot, hmask = tables
    kcvc = decode_compress(page_table, cache_nsa, w_bd, layer)
    qh = nq.reshape(bs, NSA_HEADS, 1, HEAD_DIM) * SCALE
    own_group = jnp.arange(NSA_KV_HEADS)[None, None, :, None] == (jnp.arange(NSA_HEADS) // NSA_QPG)[None, :, None, None]
    qblk = (qh * own_group).reshape(bs, NSA_HEADS, KV_WIDTH).astype(BF16)
    gpad = jnp.pad(ngate.reshape(bs, NSA_HEADS, 3), ((0, 0), (0, 0), (0, LANES - 3)))
    newr = jnp.pad(nkv.reshape(bs, 6, KV_WIDTH)[:, 2:6], ((0, 0), (0, 4), (0, 0)))

    def const2(shape):
        return pl.BlockSpec(shape, lambda s, pt: (0, 0))

    out = pl.pallas_call(
        functools.partial(_decode_attend_kernel, layer=layer),
        out_shape=jax.ShapeDtypeStruct((bs, NSA_HEADS, KV_WIDTH), F32),
        grid_spec=pltpu.PrefetchScalarGridSpec(
            num_scalar_prefetch=1,
            grid=(bs,),
            in_specs=[pl.BlockSpec(memory_space=pl.ANY),
                      pl.BlockSpec((1, NSA_HEADS, KV_WIDTH), lambda s, pt: (s, 0, 0)),
                      pl.BlockSpec((1, NSA_HEADS, LANES), lambda s, pt: (s, 0, 0)),
                      pl.BlockSpec((1, ncb, HALF_WIDTH), lambda s, pt: (s, 0, 0)),
                      pl.BlockSpec((1, 8, KV_WIDTH), lambda s, pt: (s, 0, 0)),
                      pl.BlockSpec((1, 1, WINDOW, HALF_WIDTH), lambda s, pt: (layer, s, 0, 0)),
                      const2((NSA_HEADS, ncb)), const2((NSA_HEADS, past)), const2((NSA_HEADS, WINDOW)),
                      const2((NSA_HEADS, LANES)), const2((ncb, past)), const2((NSA_HEADS, KV_WIDTH))],
            out_specs=pl.BlockSpec((1, NSA_HEADS, KV_WIDTH), lambda s, pt: (s, 0, 0)),
            scratch_shapes=[pltpu.VMEM((2, 2, past, KV_WIDTH), F32), pltpu.SemaphoreType.DMA((2,))]),
        compiler_params=pltpu.CompilerParams(dimension_semantics=("arbitrary",), vmem_limit_bytes=VMEM_LIMIT),
        name="decode_attend",
    )(page_table, cache_nsa, qblk, gpad, kcvc, newr, cache_win, cb, sb, wb, rel0, onehot, hmask)
    return (out[..., :HEAD_DIM] + out[..., HEAD_DIM:]).reshape(bs, 1, NSA_HEADS * HEAD_DIM)


_PEER_CANDS = [(a, b) for a in range(PEER_TOPK) for b in range(PEER_TOPK) if (a + 1) * (b + 1) <= PEER_TOPK]
_PEER_NCAND = len(_PEER_CANDS)
_PEER_NCAND_PAD = -(-_PEER_NCAND // 8) * 8
_PEER_GROUP_START = [min(c for c, (a, _) in enumerate(_PEER_CANDS) if a == aa) for aa in range(PEER_TOPK)]
_PEER_GROUP_LEN = [sum(1 for (a, _) in _PEER_CANDS if a == aa) for aa in range(PEER_TOPK)]


def _peer_route_kernel(xT_ref, wqT_ref, keys_ref, r2_ref, beta_ref, alpha_ref, lam_ref,
                       qT_ref, s_ref, rk_ref, vals_ref, cand_ref, sel_ref):
    tl = xT_ref.shape[1]
    qT_ref[...] = jnp.dot(wqT_ref[...], xT_ref[...], preferred_element_type=F32).astype(BF16)
    iota_k = lax.broadcasted_iota(jnp.int32, (N_KEYS, tl), 0).astype(F32)

    def head_body(h, carry):
        for p in range(2):
            row0 = pl.multiple_of(h * PEER_DK + p * (PEER_DK // 2), PEER_DK // 2)
            qs = qT_ref[pl.ds(row0, PEER_DK // 2), :]
            s = jnp.dot(keys_ref[2 * h + p], qs, preferred_element_type=F32)
            s_ref[p] = s
            cur = s
            rk = jnp.full((N_KEYS, tl), float(PEER_TOPK), F32)
            for a in range(PEER_TOPK):
                m = jnp.max(cur, axis=0, keepdims=True)
                idx = jnp.min(jnp.where(cur == m, iota_k, float(N_KEYS)), axis=0, keepdims=True)
                hit = iota_k == idx
                rk = jnp.where(hit, float(a), rk)
                cur = jnp.where(hit, NEG_INF, cur)
                vals_ref[p, a:a + 1, :] = m
            rk_ref[p] = rk
        for c, (a, b) in enumerate(_PEER_CANDS):
            cand_ref[c:c + 1, :] = vals_ref[0, a:a + 1, :] + vals_ref[1, b:b + 1, :]
        if _PEER_NCAND_PAD > _PEER_NCAND:
            cand_ref[_PEER_NCAND:_PEER_NCAND_PAD, :] = jnp.full((_PEER_NCAND_PAD - _PEER_NCAND, tl), NEG_INF, F32)
        ngrp = _PEER_NCAND_PAD // 8
        iota8 = lax.broadcasted_iota(jnp.int32, (8, tl), 0)
        ranks = [jnp.zeros((8, tl), F32) for _ in range(ngrp)]
        for cp in range(_PEER_NCAND):
            rowb = cand_ref[cp:cp + 1, :]
            for k in range(ngrp):
                blk = cand_ref[8 * k:8 * k + 8, :]
                if 8 * k > cp:
                    inc = jnp.where(rowb >= blk, 1.0, 0.0)
                elif 8 * k + 7 < cp:
                    inc = jnp.where(rowb > blk, 1.0, 0.0)
                else:
                    inc = jnp.where(iota8 + 8 * k > cp, jnp.where(rowb >= blk, 1.0, 0.0), jnp.where(rowb > blk, 1.0, 0.0))
                ranks[k] = ranks[k] + inc
        top = cand_ref[0:1, :]
        z = jnp.zeros((1, tl), F32)
        for k in range(ngrp):
            blk = cand_ref[8 * k:8 * k + 8, :]
            selk = ranks[k] < float(PEER_TOPK)
            sel_ref[8 * k:8 * k + 8, :] = jnp.where(selk, 1.0, 0.0)
            z = z + jnp.sum(jnp.where(selk, jnp.exp(blk - top), 0.0), axis=0, keepdims=True)
        rk1 = rk_ref[0]
        lam = jnp.full((N_KEYS, tl), -1.0, F32)
        for a in range(PEER_TOPK):
            g0, gl = _PEER_GROUP_START[a], _PEER_GROUP_LEN[a]
            la = jnp.sum(sel_ref[g0:g0 + gl, :], axis=0, keepdims=True) - 1.0
            lam = jnp.where(rk1 == float(a), la, lam)
        alpha = jnp.where(rk1 < float(PEER_TOPK), jnp.exp(s_ref[0] - vals_ref[0, 0:1, :]), 0.0)
        rk2 = rk_ref[1]
        beta = jnp.where(rk2 < float(PEER_TOPK), jnp.exp(s_ref[1] - vals_ref[1, 0:1, :]), 0.0) / z
        r2_ref[h] = rk2.astype(BF16)
        beta_ref[h] = beta.astype(BF16)
        alpha_ref[h] = alpha
        lam_ref[h] = lam
        return carry

    lax.fori_loop(0, PEER_HEADS, head_body, 0)


def peer_route(xT, wqT, keys2):
    d, n = xT.shape
    tl = LANES
    assert n % tl == 0
    nq = PEER_HEADS * PEER_DK
    out_bf = jax.ShapeDtypeStruct((PEER_HEADS, N_KEYS, n), BF16)
    out_f = jax.ShapeDtypeStruct((PEER_HEADS, N_KEYS, n), F32)
    tab_spec = pl.BlockSpec((PEER_HEADS, N_KEYS, tl), lambda i: (0, 0, i))
    return pl.pallas_call(
        _peer_route_kernel,
        out_shape=(out_bf, out_bf, out_f, out_f),
        grid=(n // tl,),
        in_specs=[pl.BlockSpec((d, tl), lambda i: (0, i)),
                  pl.BlockSpec((nq, d), lambda i: (0, 0)),
                  pl.BlockSpec((2 * PEER_HEADS, N_KEYS, PEER_DK // 2), lambda i: (0, 0, 0))],
        out_specs=(tab_spec, tab_spec, tab_spec, tab_spec),
        scratch_shapes=[pltpu.VMEM((nq, tl), BF16),
                        pltpu.VMEM((2, N_KEYS, tl), F32),
                        pltpu.VMEM((2, N_KEYS, tl), F32),
                        pltpu.VMEM((2, PEER_TOPK, tl), F32),
                        pltpu.VMEM((_PEER_NCAND_PAD, tl), F32),
                        pltpu.VMEM((_PEER_NCAND_PAD, tl), F32)],
        compiler_params=pltpu.CompilerParams(dimension_semantics=("arbitrary",), vmem_limit_bytes=VMEM_LIMIT),
        name="peer_route",
    )(xT, wqT, keys2)


def _gelu_tanh(x):
    return 0.5 * x * (1.0 + jnp.tanh(math.sqrt(2.0 / math.pi) * (x + 0.044715 * (x * x * x))))


def _peer_dense_kernel(xT_ref, u_ref, vT_ref, r2_ref, beta_ref, alpha_ref, lam_ref, yT_ref, a_ref, h_ref):
    j = pl.program_id(1)
    te = u_ref.shape[0]
    tl = xT_ref.shape[1]

    @pl.when(j == 0)
    def _():
        yT_ref[...] = jnp.zeros_like(yT_ref)

    a_ref[...] = jnp.dot(u_ref[...], xT_ref[...], preferred_element_type=F32)
    for r in range(te // N_KEYS):
        g = jnp.zeros((N_KEYS, tl), BF16)
        for h in range(PEER_HEADS):
            lam = lam_ref[h, r:r + 1, :].astype(BF16)
            alp = alpha_ref[h, r:r + 1, :].astype(BF16)
            g = g + jnp.where(r2_ref[h] <= lam, beta_ref[h], jnp.zeros((), BF16)) * alp
        act = _gelu_tanh(a_ref[r * N_KEYS:(r + 1) * N_KEYS, :]).astype(BF16)
        h_ref[r * N_KEYS:(r + 1) * N_KEYS, :] = act * g
    yT_ref[...] += jnp.dot(vT_ref[...], h_ref[...], preferred_element_type=F32)


def peer_dense(xT, u_bf, vT_bf, r2, beta, alpha, lam, tl, te):
    d, n = xT.shape
    e = u_bf.shape[0]
    assert n % tl == 0 and e % te == 0 and te % (8 * N_KEYS) == 0
    tab_spec = pl.BlockSpec((PEER_HEADS, N_KEYS, tl), lambda i, j: (0, 0, i))
    row_spec = pl.BlockSpec((PEER_HEADS, te // N_KEYS, tl), lambda i, j: (0, j, i))
    return pl.pallas_call(
        _peer_dense_kernel,
        out_shape=jax.ShapeDtypeStruct((d, n), F32),
        grid=(n // tl, e // te),
        in_specs=[pl.BlockSpec((d, tl), lambda i, j: (0, i)),
                  pl.BlockSpec((te, d), lambda i, j: (j, 0)),
                  pl.BlockSpec((d, te), lambda i, j: (0, j)),
                  tab_spec, tab_spec, row_spec, row_spec],
        out_specs=pl.BlockSpec((d, tl), lambda i, j: (0, i)),
        scratch_shapes=[pltpu.VMEM((te, tl), F32), pltpu.VMEM((te, tl), BF16)],
        compiler_params=pltpu.CompilerParams(dimension_semantics=("arbitrary", "arbitrary"), vmem_limit_bytes=VMEM_LIMIT),
        name="peer_dense",
    )(xT, u_bf, vT_bf, r2, beta, alpha, lam)


def peer_ffn(h, peer_w):
    wqT, keys2, u_bf, vT_bf = peer_w
    b, t, d = h.shape
    n = b * t
    tl = PEER_TL if n % PEER_TL == 0 else LANES
    npad = -(-n // tl) * tl
    xT = jnp.pad(h.reshape(n, d).astype(BF16).T, ((0, 0), (0, npad - n)))
    r2, beta, alpha, lam = peer_route(xT, wqT, keys2)
    yT = peer_dense(xT, u_bf, vT_bf, r2, beta, alpha, lam, tl, PEER_TE)
    return yT.T[:n].reshape(b, t, d)


def token_mixers(parts, pos, rel_table, w_cmp, conv_w, conv_b, ret_g, past, nsa_tables):
    nq, nkv, ngate, rq, rk, rv, rg, cb, cc, ch = parts
    b, t = nq.shape[:2]
    q = nq.reshape(b, t, NSA_KV_HEADS, NSA_QPG, HEAD_DIM)
    kv = nkv.reshape(b, t, 6, NSA_KV_HEADS, HEAD_DIM)
    if past is None:
        kc = compress(kv[:, :, 0], w_cmp[0])
        vc = compress(kv[:, :, 1], w_cmp[1])
        o_nsa = nsa_prompt(nq, kv, ngate, kc, vc, nsa_tables)
        win_rows = kv[:, t - min(WINDOW, t):, 4:]
        s0 = jnp.zeros((b, RET_HEADS, RET_DK, RET_DV), jnp.float32)
        zbuf = jnp.zeros((b, CONV_W - 1, CONV_CH), ch.dtype)
        chunk = RET_CHUNK
    else:
        decode_nsa, win_buf, s0, zbuf = past
        o_nsa = decode_nsa(nq, nkv, ngate)
        wrows = jnp.concatenate([win_buf.astype(kv.dtype), kv[:, :, 4:]], axis=1)
        win_rows = wrows[:, wrows.shape[1] - min(WINDOW, wrows.shape[1]):]
        chunk = t
    rqh = rotary(rq.reshape(b, t, RET_HEADS, RET_DK), pos)
    rkh = rotary(rk.reshape(b, t, RET_HEADS, RET_DK), pos) * (RET_DK ** -0.5)
    rvh = rv.reshape(b, t, RET_HEADS, RET_DV)
    o_r, s_new = retention(rqh, rkh, rvh, s0, chunk)
    mu = jnp.mean(o_r, -1, keepdims=True)
    var = jnp.mean(jnp.square(o_r - mu), -1, keepdims=True)
    on = (o_r - mu) * lax.rsqrt(var + LN_EPS) * ret_g.reshape(RET_HEADS, RET_DV)
    o_ret = (on.reshape(b, t, RET_WIDTH) * jax.nn.silu(rg.astype(jnp.float32))).astype(nq.dtype)
    z = cc * ch
    zp = jnp.concatenate([zbuf.astype(z.dtype), z], axis=1)
    y = conv_b + sum(zp[:, j:j + t] * conv_w[j] for j in range(CONV_W))
    o_conv = (cb * y).astype(nq.dtype)
    mix = jnp.concatenate([o_nsa, o_ret, o_conv], axis=-1)
    return mix, (kv[:, :, :4], win_rows, s_new, zp[:, t:])


def kernel(x_prompt, x_sample, cache_nsa_kv, cache_win_kv, state_ret, state_conv, page_table, c_prompt, c_sample, rel_bias, w_ada, b_ada, w_in, w_cmp, conv_w, conv_b, ret_norm_g, w_out, ln1_g, ln1_b, ln2_g, ln2_b, peer_wq, peer_keys, peer_u, peer_v):
    alpha = (2.0 * DEPTH) ** 0.25
    n_pages = page_table.shape[1]
    past_len = n_pages * PAGE_SIZE
    sp = x_prompt.shape[1]
    bs, ts = x_sample.shape[:2]
    pos_p = jnp.arange(sp, dtype=jnp.int32)
    pos_s = past_len + jnp.arange(ts, dtype=jnp.int32)
    nsa_tables = nsa_bias_tables(rel_bias, sp, NSA_TQ)
    dec_tables = decode_tables(rel_bias, past_len)
    cache_nsa = cache_nsa_kv.reshape(cache_nsa_kv.shape[0], cache_nsa_kv.shape[1], PAGE_SIZE, ROW_WIDTH)
    cache_win = cache_win_kv.reshape(cache_win_kv.shape[0], bs, cache_win_kv.shape[2], HALF_WIDTH)

    def run_layer(x, c, l, pos, past, peer_w):
        m = (jax.nn.silu(c) @ w_ada[l] + b_ada[l]).reshape(c.shape[0], 6, 1, D_MODEL)
        h = x * (1.0 + m[:, 1]) + m[:, 0]
        parts = split_proj(h @ w_in[l])
        mix, st = token_mixers(parts, pos, rel_bias, w_cmp[l], conv_w[l], conv_b[l], ret_norm_g[l], past, nsa_tables)
        x = layer_norm(alpha * x + (1.0 + m[:, 2]) * (mix @ w_out[l]), ln1_g[l], ln1_b[l])
        h = x * (1.0 + m[:, 4]) + m[:, 3]
        y = peer_ffn(h, peer_w)
        x = layer_norm(alpha * x + (1.0 + m[:, 5]) * y, ln2_g[l], ln2_b[l])
        return x, st

    xp, xs = x_prompt, x_sample
    nkv_p, nkv_s, win_p, win_s, ret_p, ret_s, conv_p, conv_s = [], [], [], [], [], [], [], []
    for l in range(DEPTH):
        peer_w = (peer_wq[l].T.astype(BF16),
                  peer_keys[l].reshape(2 * PEER_HEADS, N_KEYS, PEER_DK // 2).astype(BF16),
                  peer_u[l].astype(BF16),
                  peer_v[l].T.astype(BF16))
        xp, st_p = run_layer(xp, c_prompt, l, pos_p, None, peer_w)
        decode_nsa = functools.partial(nsa_decode, page_table=page_table, cache_nsa=cache_nsa, cache_win=cache_win,
                                       w_bd=compress_block_weights(w_cmp[l]), tables=dec_tables, layer=l)
        xs, st_s = run_layer(xs, c_sample, l, pos_s, (decode_nsa, cache_win_kv[l], state_ret[l], state_conv[l]), peer_w)
        nkv_p.append(st_p[0])
        win_p.append(st_p[1])
        ret_p.append(st_p[2])
        conv_p.append(st_p[3])
        nkv_s.append(st_s[0])
        win_s.append(st_s[1])
        ret_s.append(st_s[2])
        conv_s.append(st_s[3])
    return (xp, xs, jnp.stack(nkv_p), jnp.stack(nkv_s), jnp.stack(win_p), jnp.stack(win_s), jnp.stack(ret_p), jnp.stack(ret_s), jnp.stack(conv_p), jnp.stack(conv_s))
```

```python
import math
import functools
import jax
import jax.numpy as jnp
from jax import lax
import numpy as np
from jax.experimental import pallas as pl
from jax.experimental.pallas import tpu as pltpu

D_MODEL = 1024
BATCH = 4
SEQ = 4096
DEPTH = 2
DEC_BATCH = 32
DEC_SEQ = 1
PAST_LEN = 8192
PAGE_SIZE = 128

HEAD_DIM = 64
NSA_WIDTH = D_MODEL // 2
NSA_HEADS = NSA_WIDTH // HEAD_DIM
NSA_KV_HEADS = 2
NSA_QPG = NSA_HEADS // NSA_KV_HEADS
KV_WIDTH = NSA_KV_HEADS * HEAD_DIM
SCALE = HEAD_DIM ** -0.5
L_CMP = 32
L_SEL = 64
N_SEL = 16
WINDOW = 512
Q_BLOCK = 128
SEL_Q_BLOCK = 64
N_BUCKETS = 32
REL_MAX_DIST = 128
RET_WIDTH = D_MODEL // 4
RET_DK = 64
RET_DV = 64
RET_HEADS = RET_WIDTH // RET_DV
RET_CHUNK = 128
ROPE_BASE = 10000.0
CONV_CH = D_MODEL // 4
CONV_W = 3
MIX_WIDTH = NSA_WIDTH + RET_WIDTH + CONV_CH
PEER_HEADS = 8
PEER_DK = 256
N_KEYS = 128
N_EXPERTS = N_KEYS * N_KEYS
PEER_TOPK = 16
PEER_CHUNK = 256
LN_EPS = 1e-5
SPLIT_SIZES = (NSA_WIDTH, 6 * KV_WIDTH, 3 * NSA_HEADS, RET_HEADS * RET_DK, RET_HEADS * RET_DK, RET_WIDTH, RET_WIDTH, CONV_CH, CONV_CH, CONV_CH)
N_IN = sum(SPLIT_SIZES)

F32 = jnp.float32
BF16 = jnp.bfloat16
NEG_INF = float('-inf')
MASK_NEG = -1e9
CMP_MASK_NEG = -1e30
NSA_TQ = 256
NSA_RB = 128
ROW_WIDTH = 4 * KV_WIDTH
HALF_WIDTH = 2 * KV_WIDTH
PEER_TL = 512
PEER_TE = 1024
PEER_SUB_PER_DOT = 2
LANES = 128
VMEM_LIMIT = 56 * 1024 * 1024


def _ln_kernel(x_ref, g_ref, b_ref, o_ref):
    x = x_ref[...]
    mu = jnp.mean(x, -1, keepdims=True)
    xc = x - mu
    var = jnp.mean(xc * xc, -1, keepdims=True)
    o_ref[...] = xc * lax.rsqrt(var + LN_EPS) * g_ref[...] + b_ref[...]


def layer_norm(x, g, b):
    shp = x.shape
    x2 = x.reshape(-1, shp[-1])
    n = x2.shape[0]
    tm = min(n, 512)
    out = pl.pallas_call(
        _ln_kernel,
        out_shape=jax.ShapeDtypeStruct(x2.shape, jnp.float32),
        grid=(n // tm,),
        in_specs=[pl.BlockSpec((tm, shp[-1]), lambda i: (i, 0)),
                  pl.BlockSpec((1, shp[-1]), lambda i: (0, 0)),
                  pl.BlockSpec((1, shp[-1]), lambda i: (0, 0))],
        out_specs=pl.BlockSpec((tm, shp[-1]), lambda i: (i, 0)),
        name="layer_norm",
    )(x2, g.reshape(1, -1), b.reshape(1, -1))
    return out.reshape(shp)


def t5_bucket(dist):
    n = jnp.maximum(dist, 0)
    exact = N_BUCKETS // 2
    nf = jnp.maximum(n, exact).astype(jnp.float32)
    big = exact + (jnp.log(nf / exact) / math.log(REL_MAX_DIST / exact) * (N_BUCKETS - exact)).astype(jnp.int32)
    return jnp.where(n < exact, n, jnp.minimum(big, N_BUCKETS - 1))


def rel_bias_lookup(rel_bias, dist):
    onehot = jax.nn.one_hot(t5_bucket(dist), N_BUCKETS, dtype=jnp.float32)
    return jnp.einsum('...k,kh->...h', onehot, rel_bias, precision=lax.Precision.HIGHEST)


def rotary(x, pos):
    half = x.shape[-1] // 2
    inv = ROPE_BASE ** (-jnp.arange(half, dtype=jnp.float32) / half)
    ang = pos.astype(jnp.float32)[:, None] * inv[None, :]
    cos = jnp.cos(ang)[None, :, None, :]
    sin = jnp.sin(ang)[None, :, None, :]
    x1, x2 = x[..., :half], x[..., half:]
    return jnp.concatenate([x1 * cos - x2 * sin, x1 * sin + x2 * cos], axis=-1)


def split_proj(p):
    outs, start = [], 0
    for size in SPLIT_SIZES:
        outs.append(p[..., start:start + size])
        start += size
    return outs


def compress(rows, w):
    b, lp, g, d = rows.shape
    blk = rows.reshape(b, lp // L_CMP, L_CMP, g, d).transpose(0, 1, 3, 2, 4).reshape(b, lp // L_CMP, g, L_CMP * d)
    return blk @ w


def retention(q, k, v, s0, chunk):
    b, t = q.shape[:2]
    nc = t // chunk
    lg = jnp.log(1.0 - 2.0 ** (-5.0 - jnp.arange(RET_HEADS, dtype=jnp.float32)))
    i = jnp.arange(chunk, dtype=jnp.float32)
    diff = i[:, None] - i[None, :]
    dmat = jnp.where(diff >= 0, jnp.exp(jnp.maximum(diff, 0.0)[None] * lg[:, None, None]), 0.0)
    q_dec = jnp.exp((i + 1.0)[:, None] * lg[None, :])[None, :, :, None]
    k_dec = jnp.exp((chunk - 1.0 - i)[:, None] * lg[None, :])[None, :, :, None]
    s_dec = jnp.exp(chunk * lg)[None, :, None, None]

    def to_chunks(a):
        return jnp.moveaxis(a.astype(jnp.float32).reshape(b, nc, chunk, RET_HEADS, a.shape[-1]), 1, 0)

    def step(s, xs):
        qc, kc, vc = xs
        att = jnp.einsum('bihd,bjhd->bhij', qc, kc) * dmat
        o = jnp.einsum('bhij,bjhe->bihe', att, vc) + jnp.einsum('bihd,bhde->bihe', qc * q_dec, s)
        s = s * s_dec + jnp.einsum('bjhd,bjhe->bhde', kc * k_dec, vc)
        return s, o

    s, o = lax.scan(step, s0.astype(jnp.float32), (to_chunks(q), to_chunks(k), to_chunks(v)))
    return jnp.moveaxis(o, 0, 1).reshape(b, t, RET_HEADS, RET_DV), s


def _nsa_prompt_kernel(q_ref, gate_ref, kcT_ref, vc_ref, cb_ref, ksT_ref, vs_ref, kwT_ref, vw_ref, sb_ref, wb_ref,
                       o_ref, qa_ref, m_ref, acc_ref, oc_ref, osel_ref, owin_ref):
    qi = pl.program_id(2)
    tq = q_ref.shape[3]
    ncb = kcT_ref.shape[3]
    ka = ksT_ref.shape[2]
    t0 = qi * tq
    rows = NSA_QPG * tq

    imp = jnp.zeros((tq, ncb), F32)
    for h in range(NSA_QPG):
        bias = cb_ref[0, h]
        lg = jnp.dot(q_ref[0, 0, h], kcT_ref[0, 0], preferred_element_type=F32) + bias
        mx = jnp.max(lg, axis=-1, keepdims=True)
        e = jnp.where(bias > 0.5 * CMP_MASK_NEG, jnp.exp(lg - mx), 0.0)
        p = e / jnp.maximum(jnp.sum(e, axis=-1, keepdims=True), 1e-30)
        imp = imp + p
        oc_ref[h] = jnp.dot(p.astype(BF16), vc_ref[0, 0], preferred_element_type=F32)

    lane = lax.broadcasted_iota(jnp.int32, (tq, ncb), 1)
    tpos = t0 + lax.broadcasted_iota(jnp.int32, (tq, ncb), 0)
    pair = imp + pltpu.roll(imp, ncb - 1, 1)
    blk = lane >> 1
    cur = tpos >> 6
    forced = (blk == 0) | (blk == cur) | (blk == cur - 1)
    cand = ((lane & 1) == 0) & (blk <= cur)
    score = jnp.where(cand, jnp.where(forced, jnp.inf, pair), NEG_INF)
    lane_f = lane.astype(F32)
    chosen = jnp.zeros((tq, ncb), F32)
    for _ in range(N_SEL):
        mx = jnp.max(score, axis=-1, keepdims=True)
        first = jnp.min(jnp.where(score == mx, lane_f, float(ncb)), axis=-1, keepdims=True)
        hit = (lane_f == first) & (mx > NEG_INF)
        chosen = jnp.where(hit, 1.0, chosen)
        score = jnp.where(hit, NEG_INF, score)
    blockmask = jnp.where(chosen > 0.0, 0.0, MASK_NEG).astype(BF16)

    for h in range(NSA_QPG):
        qa_ref[h * tq:(h + 1) * tq, 0:ncb] = blockmask
        qa_ref[h * tq:(h + 1) * tq, ncb:ncb + HEAD_DIM] = q_ref[0, 0, h]
        if ka > ncb + HEAD_DIM:
            qa_ref[h * tq:(h + 1) * tq, ncb + HEAD_DIM:ka] = jnp.zeros((tq, ka - ncb - HEAD_DIM), BF16)

    nrb = rows // NSA_RB
    per_head = tq // NSA_RB

    def reset():
        m_ref[...] = jnp.full((rows, LANES), CMP_MASK_NEG, F32)
        acc_ref[...] = jnp.zeros((rows, LANES), F32)

    def attn_step(rb, qrb, k_t, v, bias):
        r0 = rb * NSA_RB
        s = jnp.dot(qrb, k_t, preferred_element_type=F32)
        if bias is not None:
            s = s + bias
        parts = [s[:, i * LANES:(i + 1) * LANES] for i in range(s.shape[1] // LANES)]
        red = parts[0]
        for part in parts[1:]:
            red = jnp.maximum(red, part)
        m_old = m_ref[r0:r0 + NSA_RB, :]
        m_new = jnp.maximum(m_old, jnp.max(red, axis=-1, keepdims=True))
        p = jnp.concatenate([jnp.exp(part - m_new) for part in parts], axis=1).astype(BF16)
        acc_ref[r0:r0 + NSA_RB, :] = (jnp.exp(m_old - m_new) * acc_ref[r0:r0 + NSA_RB, :]
                                      + jnp.dot(p, v, preferred_element_type=F32))
        m_ref[r0:r0 + NSA_RB, :] = m_new

    def finish(dst_ref):
        acc = acc_ref[...]
        dst_ref[...] = acc[:, 0:HEAD_DIM] / acc[:, HEAD_DIM:HEAD_DIM + 1]

    reset()

    def far_body(c, carry):
        col = pl.multiple_of((c + 1) * tq, tq)
        k_t = ksT_ref[0, 0, :, pl.ds(col, tq)]
        v = vs_ref[0, 0, pl.ds(col, tq), :]
        for rb in range(nrb):
            attn_step(rb, qa_ref[rb * NSA_RB:(rb + 1) * NSA_RB, :], k_t, v, None)
        return carry

    lax.fori_loop(0, jnp.maximum(qi - 1, 0), far_body, 0)

    col0 = pl.multiple_of(t0, tq)
    for c in range(2):
        colc = pl.multiple_of(col0 + c * tq, tq)
        k_t = ksT_ref[0, 0, :, pl.ds(colc, tq)]
        v = vs_ref[0, 0, pl.ds(colc, tq), :]
        jn = lax.broadcasted_iota(jnp.int32, (1, tq), 1) + c * tq
        colmask = jnp.where(jn + (t0 - tq) >= 0, 0.0, MASK_NEG)
        for rb in range(nrb):
            h, part = rb // per_head, rb % per_head
            bias = sb_ref[0, h, part * NSA_RB:(part + 1) * NSA_RB, c * tq:(c + 1) * tq] + colmask
            attn_step(rb, qa_ref[rb * NSA_RB:(rb + 1) * NSA_RB, :], k_t, v, bias)
    finish(osel_ref)

    reset()
    for c in range((WINDOW + tq) // tq):
        colc = pl.multiple_of(col0 + c * tq, tq)
        k_t = kwT_ref[0, 0, :, pl.ds(colc, tq)]
        v = vw_ref[0, 0, pl.ds(colc, tq), :]
        jw = lax.broadcasted_iota(jnp.int32, (1, tq), 1) + c * tq
        wmask = jnp.where(jw + (t0 - WINDOW) >= 0, 0.0, MASK_NEG)
        for rb in range(nrb):
            h, part = rb // per_head, rb % per_head
            bias = wb_ref[0, h, part * NSA_RB:(part + 1) * NSA_RB, c * tq:(c + 1) * tq] + wmask
            attn_step(rb, qa_ref[rb * NSA_RB:(rb + 1) * NSA_RB, ncb:ncb + HEAD_DIM], k_t, v, bias)
    finish(owin_ref)

    g = jax.nn.sigmoid(gate_ref[0, 0])
    for h in range(NSA_QPG):
        o_h = (g[:, 3 * h:3 * h + 1] * oc_ref[h]
               + g[:, 3 * h + 1:3 * h + 2] * osel_ref[h * tq:(h + 1) * tq, :]
               + g[:, 3 * h + 2:3 * h + 3] * owin_ref[h * tq:(h + 1) * tq, :])
        o_ref[0, :, h * HEAD_DIM:(h + 1) * HEAD_DIM] = o_h


def nsa_bias_tables(rel_bias, t, tq):
    ncb = t // L_CMP

    def heads_first(tab):
        return tab.transpose(2, 0, 1).reshape(NSA_KV_HEADS, NSA_QPG, tab.shape[0], tab.shape[1])

    pos = jnp.arange(t, dtype=jnp.int32)
    end = jnp.arange(ncb, dtype=jnp.int32) * L_CMP + (L_CMP - 1)
    dist = pos[:, None] - end[None, :]
    cb = jnp.where((dist >= 0)[..., None], rel_bias_lookup(rel_bias, dist), CMP_MASK_NEG)
    i = jnp.arange(tq, dtype=jnp.int32)
    dist = i[:, None] + tq - jnp.arange(2 * tq, dtype=jnp.int32)[None, :]
    sb = jnp.where((dist >= 0)[..., None], rel_bias_lookup(rel_bias, dist) - rel_bias[N_BUCKETS - 1], MASK_NEG)
    dist = i[:, None] + WINDOW - jnp.arange(WINDOW + tq, dtype=jnp.int32)[None, :]
    wb = jnp.where(((dist >= 0) & (dist < WINDOW))[..., None], rel_bias_lookup(rel_bias, dist), MASK_NEG)
    return heads_first(cb), heads_first(sb), heads_first(wb)


def nsa_prompt(q, kv, gates, kc, vc, tables):
    b, t = q.shape[:2]
    tq = NSA_TQ
    ncb = t // L_CMP
    ka = -(-(ncb + HEAD_DIM) // LANES) * LANES
    cb, sb, wb = tables
    q4 = (q * SCALE).astype(BF16).reshape(b, t, NSA_KV_HEADS, NSA_QPG, HEAD_DIM).transpose(0, 2, 3, 1, 4)
    g4 = gates.reshape(b, t, NSA_KV_HEADS, NSA_QPG * 3).transpose(0, 2, 1, 3)
    kcT = kc.astype(BF16).transpose(0, 2, 3, 1)
    vcg = vc.astype(BF16).transpose(0, 2, 1, 3)
    kvb = kv.astype(BF16)
    onehot = (2 * (jnp.arange(t, dtype=jnp.int32) // L_SEL)[None, :] == jnp.arange(ncb, dtype=jnp.int32)[:, None]).astype(BF16)
    ks_t = kvb[:, :, 2].transpose(0, 2, 3, 1)
    ksT = jnp.concatenate([jnp.broadcast_to(onehot, (b, NSA_KV_HEADS, ncb, t)), ks_t,
                           jnp.zeros((b, NSA_KV_HEADS, ka - ncb - HEAD_DIM, t), BF16)], axis=2)
    ksT = jnp.pad(ksT, ((0, 0), (0, 0), (0, 0), (tq, 0)))
    def with_ones(v):
        one = jnp.ones(v.shape[:-1] + (1,), BF16)
        return jnp.concatenate([v, one, jnp.zeros(v.shape[:-1] + (LANES - HEAD_DIM - 1,), BF16)], axis=-1)

    vs = jnp.pad(with_ones(kvb[:, :, 3].transpose(0, 2, 1, 3)), ((0, 0), (0, 0), (tq, 0), (0, 0)))
    kwT = jnp.pad(kvb[:, :, 4].transpose(0, 2, 3, 1), ((0, 0), (0, 0), (0, 0), (WINDOW, 0)))
    vw = jnp.pad(with_ones(kvb[:, :, 5].transpose(0, 2, 1, 3)), ((0, 0), (0, 0), (WINDOW, 0), (0, 0)))
    rows = NSA_QPG * tq

    def per_bg(shape):
        return pl.BlockSpec((1, 1) + shape, lambda bi, gi, qi: (bi, gi, 0, 0))

    return pl.pallas_call(
        _nsa_prompt_kernel,
        out_shape=jax.ShapeDtypeStruct((b, t, NSA_HEADS * HEAD_DIM), F32),
        grid=(b, NSA_KV_HEADS, t // tq),
        in_specs=[pl.BlockSpec((1, 1, NSA_QPG, tq, HEAD_DIM), lambda bi, gi, qi: (bi, gi, 0, qi, 0)),
                  pl.BlockSpec((1, 1, tq, NSA_QPG * 3), lambda bi, gi, qi: (bi, gi, qi, 0)),
                  per_bg((HEAD_DIM, ncb)), per_bg((ncb, HEAD_DIM)),
                  pl.BlockSpec((1, NSA_QPG, tq, ncb), lambda bi, gi, qi: (gi, 0, qi, 0)),
                  per_bg((ka, tq + t)), per_bg((tq + t, LANES)),
                  per_bg((HEAD_DIM, WINDOW + t)), per_bg((WINDOW + t, LANES)),
                  pl.BlockSpec((1, NSA_QPG, tq, 2 * tq), lambda bi, gi, qi: (gi, 0, 0, 0)),
                  pl.BlockSpec((1, NSA_QPG, tq, WINDOW + tq), lambda bi, gi, qi: (gi, 0, 0, 0))],
        out_specs=pl.BlockSpec((1, tq, NSA_QPG * HEAD_DIM), lambda bi, gi, qi: (bi, qi, gi)),
        scratch_shapes=[pltpu.VMEM((rows, ka), BF16),
                        pltpu.VMEM((rows, LANES), F32), pltpu.VMEM((rows, LANES), F32),
                        pltpu.VMEM((NSA_QPG, tq, HEAD_DIM), F32),
                        pltpu.VMEM((rows, HEAD_DIM), F32), pltpu.VMEM((rows, HEAD_DIM), F32)],
        compiler_params=pltpu.CompilerParams(dimension_semantics=("arbitrary", "arbitrary", "arbitrary"),
                                             vmem_limit_bytes=VMEM_LIMIT),
        name="nsa_prompt",
    )(q4, g4, kcT, vcg, cb, ksT, vs, kwT, vw, sb, wb)


def _gather_pages(pt_ref, cache_ref, sem, layer, feat0, dst_of):
    s = pl.program_id(0)
    n_pages = pt_ref.shape[1]
    slot = lax.rem(s, 2)

    def copy(page, sl, p):
        return pltpu.make_async_copy(cache_ref.at[layer, page, pl.ds(feat0, HALF_WIDTH), :], dst_of(sl, p), sem.at[sl])

    def start(seq, sl):
        for p in range(n_pages):
            copy(pt_ref[seq, p], sl, p).start()

    @pl.when(s == 0)
    def _():
        start(0, 0)

    @pl.when(s + 1 < pl.num_programs(0))
    def _():
        start(s + 1, 1 - slot)

    for p in range(n_pages):
        copy(0, slot, p).wait()
    return slot


def _decode_compress_kernel(pt_ref, cache_ref, w_ref, out_ref, buf, rows_ref, sem, *, layer):
    slot = _gather_pages(pt_ref, cache_ref, sem, layer, 0, lambda sl, p: buf.at[sl, p])
    ncb = out_ref.shape[1]
    n_pages = pt_ref.shape[1]
    eye = (lax.broadcasted_iota(jnp.int32, (PAGE_SIZE, PAGE_SIZE), 0)
           == lax.broadcasted_iota(jnp.int32, (PAGE_SIZE, PAGE_SIZE), 1)).astype(F32).astype(BF16)
    for p in range(n_pages):
        x = _dot_nt(eye, buf[slot, p].astype(BF16))
        for part in range(2):
            rows_ref[part, p * PAGE_SIZE:(p + 1) * PAGE_SIZE, :] = x[:, part * KV_WIDTH:(part + 1) * KV_WIDTH]
    for part in range(2):
        acc = jnp.zeros((ncb, KV_WIDTH), F32)
        for r in range(L_CMP):
            x = rows_ref[part, pl.ds(r, ncb, stride=L_CMP), :]
            acc = acc + jnp.dot(x.astype(BF16), w_ref[r, part], preferred_element_type=F32)
        out_ref[0, :, part * KV_WIDTH:(part + 1) * KV_WIDTH] = acc


def decode_compress(page_table, cache, w_bd, layer):
    bs, n_pages = page_table.shape
    past = n_pages * PAGE_SIZE
    ncb = past // L_CMP
    return pl.pallas_call(
        functools.partial(_decode_compress_kernel, layer=layer),
        out_shape=jax.ShapeDtypeStruct((bs, ncb, HALF_WIDTH), F32),
        grid_spec=pltpu.PrefetchScalarGridSpec(
            num_scalar_prefetch=1,
            grid=(bs,),
            in_specs=[pl.BlockSpec(memory_space=pl.ANY),
                      pl.BlockSpec((L_CMP, 2, KV_WIDTH, KV_WIDTH), lambda s, pt: (0, 0, 0, 0))],
            out_specs=pl.BlockSpec((1, ncb, HALF_WIDTH), lambda s, pt: (s, 0, 0)),
            scratch_shapes=[pltpu.VMEM((2, n_pages, HALF_WIDTH, PAGE_SIZE), F32),
                            pltpu.VMEM((2, past, KV_WIDTH), F32), pltpu.SemaphoreType.DMA((2,))]),
        compiler_params=pltpu.CompilerParams(dimension_semantics=("arbitrary",), vmem_limit_bytes=VMEM_LIMIT),
        name="decode_compress",
    )(page_table, cache, w_bd)


def _dot_nt(a, b):
    return lax.dot_general(a, b, (((1,), (1,)), ((), ())), preferred_element_type=F32)


def _decode_attend_kernel(pt_ref, cache_ref, q_ref, gate_ref, kcvc_ref, new_ref, win_ref, cb_ref, sb_ref, wb_ref,
                          rel0_ref, onehot_ref, hmask_ref, o_ref, buf, sem, *, layer):
    slot = _gather_pages(pt_ref, cache_ref, sem, layer, HALF_WIDTH,
                         lambda sl, p: buf.at[sl, :, pl.ds(p * PAGE_SIZE, PAGE_SIZE)])
    q = q_ref[0]
    qf = q.astype(F32)
    ncb = kcvc_ref.shape[1]
    rel0 = rel0_ref[:, 0:1]

    def bf_round(x):
        return x.astype(BF16).astype(F32)

    kc = kcvc_ref[0, :, 0:KV_WIDTH].astype(BF16)
    vc = kcvc_ref[0, :, KV_WIDTH:HALF_WIDTH].astype(BF16)
    lg = _dot_nt(q, kc) + cb_ref[...]
    e = jnp.exp(lg - jnp.max(lg, axis=-1, keepdims=True))
    p = e / jnp.sum(e, axis=-1, keepdims=True)
    o_cmp = jnp.dot(p.astype(BF16), vc, preferred_element_type=F32)

    row = lax.broadcasted_iota(jnp.int32, (NSA_HEADS, ncb), 0)
    lane = lax.broadcasted_iota(jnp.int32, (NSA_HEADS, ncb), 1)
    pg0 = jnp.sum(p[0:NSA_QPG], axis=0, keepdims=True)
    pg1 = jnp.sum(p[NSA_QPG:NSA_HEADS], axis=0, keepdims=True)
    imp = jnp.where(row < NSA_QPG, pg0, pg1)
    pair = imp + pltpu.roll(imp, ncb - 1, 1)
    blk = lane >> 1
    forced = (blk == 0) | (blk == ncb // 2 - 1)
    score = jnp.where((lane & 1) == 0, jnp.where(forced, jnp.inf, pair), NEG_INF)
    lane_f = lane.astype(F32)
    chosen = jnp.zeros((NSA_HEADS, ncb), F32)
    for _ in range(N_SEL - 1):
        mx = jnp.max(score, axis=-1, keepdims=True)
        first = jnp.min(jnp.where(score == mx, lane_f, float(ncb)), axis=-1, keepdims=True)
        hit = lane_f == first
        chosen = jnp.where(hit, 1.0, chosen)
        score = jnp.where(hit, NEG_INF, score)
    blockmask = jnp.where(chosen > 0.0, 0.0, MASK_NEG).astype(BF16)

    ks_t = buf[slot, 0:KV_WIDTH, :].astype(BF16)
    vs_t = buf[slot, KV_WIDTH:HALF_WIDTH, :].astype(BF16)
    s = (jnp.dot(q, ks_t, preferred_element_type=F32)
         + jnp.dot(blockmask, onehot_ref[...], preferred_element_type=F32) + sb_ref[...])
    s_new = jnp.sum(qf * bf_round(new_ref[0, 0:1, :]), axis=-1, keepdims=True) + rel0
    m = jnp.maximum(jnp.max(s, axis=-1, keepdims=True), s_new)
    e = jnp.exp(s - m)
    e_new = jnp.exp(s_new - m)
    den = jnp.sum(e, axis=-1, keepdims=True) + e_new
    o_sel = (_dot_nt(e.astype(BF16), vs_t) + e_new * bf_round(new_ref[0, 1:2, :])) / den

    wk_t = win_ref[0, 0, 0:KV_WIDTH, :].astype(BF16)
    wv_t = win_ref[0, 0, KV_WIDTH:HALF_WIDTH, :].astype(BF16)
    sw = jnp.dot(q, wk_t, preferred_element_type=F32) + wb_ref[...]
    sw_new = jnp.sum(qf * bf_round(new_ref[0, 2:3, :]), axis=-1, keepdims=True) + rel0
    mw = jnp.maximum(jnp.max(sw, axis=-1, keepdims=True), sw_new)
    ew = jnp.exp(sw - mw)
    ew_new = jnp.exp(sw_new - mw)
    denw = jnp.sum(ew, axis=-1, keepdims=True) + ew_new
    o_win = (_dot_nt(ew.astype(BF16), wv_t) + ew_new * bf_round(new_ref[0, 3:4, :])) / denw

    g = jax.nn.sigmoid(gate_ref[0])
    o_ref[0] = (g[:, 0:1] * o_cmp + g[:, 1:2] * o_sel + g[:, 2:3] * o_win) * hmask_ref[...]


def decode_tables(rel_bias, past):
    ncb = past // L_CMP
    end = jnp.arange(ncb, dtype=jnp.int32) * L_CMP + (L_CMP - 1)
    cb = rel_bias_lookup(rel_bias, past - end).T
    sb = rel_bias_lookup(rel_bias, past - jnp.arange(past, dtype=jnp.int32)).T
    dist = WINDOW - jnp.arange(WINDOW, dtype=jnp.int32)
    wb = jnp.where((dist < WINDOW)[None, :], rel_bias_lookup(rel_bias, dist).T, MASK_NEG)
    rel0 = jnp.broadcast_to(rel_bias[0][:, None], (NSA_HEADS, LANES))
    onehot = (2 * (jnp.arange(past, dtype=jnp.int32) // L_SEL)[None, :] == jnp.arange(ncb, dtype=jnp.int32)[:, None]).astype(BF16)
    hmask = (jnp.arange(KV_WIDTH, dtype=jnp.int32)[None, :] // HEAD_DIM == jnp.arange(NSA_HEADS, dtype=jnp.int32)[:, None] // NSA_QPG).astype(F32)
    return cb, sb, wb, rel0, onehot, hmask


def compress_block_weights(w_cmp):
    w = w_cmp.reshape(2, L_CMP, HEAD_DIM, HEAD_DIM).transpose(1, 0, 2, 3)
    z = jnp.zeros_like(w)
    return jnp.concatenate([jnp.concatenate([w, z], axis=3), jnp.concatenate([z, w], axis=3)], axis=2).astype(BF16)


def nsa_decode(nq, nkv, ngate, page_table, cache_nsa, cache_win, w_bd, tables, layer):
    bs = nq.shape[0]
    n_pages = page_table.shape[1]
    past = n_pages * PAGE_SIZE
    ncb = past // L_CMP
    cb, sb, wb, rel0, onehot, hmask = tables
    kcvc = decode_compress(page_table, cache_nsa, w_bd, layer)
    qh = nq.reshape(bs, NSA_HEADS, 1, HEAD_DIM) * SCALE
    own_group = jnp.arange(NSA_KV_HEADS)[None, None, :, None] == (jnp.arange(NSA_HEADS) // NSA_QPG)[None, :, None, None]
    qblk = (qh * own_group).reshape(bs, NSA_HEADS, KV_WIDTH).astype(BF16)
    gpad = jnp.pad(ngate.reshape(bs, NSA_HEADS, 3), ((0, 0), (0, 0), (0, LANES - 3)))
    newr = jnp.pad(nkv.reshape(bs, 6, KV_WIDTH)[:, 2:6], ((0, 0), (0, 4), (0, 0)))

    def const2(shape):
        return pl.BlockSpec(shape, lambda s, pt: (0, 0))

    out = pl.pallas_call(
        functools.partial(_decode_attend_kernel, layer=layer),
        out_shape=jax.ShapeDtypeStruct((bs, NSA_HEADS, KV_WIDTH), F32),
        grid_spec=pltpu.PrefetchScalarGridSpec(
            num_scalar_prefetch=1,
            grid=(bs,),
            in_specs=[pl.BlockSpec(memory_space=pl.ANY),
                      pl.BlockSpec((1, NSA_HEADS, KV_WIDTH), lambda s, pt: (s, 0, 0)),
                      pl.BlockSpec((1, NSA_HEADS, LANES), lambda s, pt: (s, 0, 0)),
                      pl.BlockSpec((1, ncb, HALF_WIDTH), lambda s, pt: (s, 0, 0)),
                      pl.BlockSpec((1, 8, KV_WIDTH), lambda s, pt: (s, 0, 0)),
                      pl.BlockSpec((1, 1, HALF_WIDTH, WINDOW), lambda s, pt: (layer, s, 0, 0)),
                      const2((NSA_HEADS, ncb)), const2((NSA_HEADS, past)), const2((NSA_HEADS, WINDOW)),
                      const2((NSA_HEADS, LANES)), const2((ncb, past)), const2((NSA_HEADS, KV_WIDTH))],
            out_specs=pl.BlockSpec((1, NSA_HEADS, KV_WIDTH), lambda s, pt: (s, 0, 0)),
            scratch_shapes=[pltpu.VMEM((2, HALF_WIDTH, past), F32), pltpu.SemaphoreType.DMA((2,))]),
        compiler_params=pltpu.CompilerParams(dimension_semantics=("arbitrary",), vmem_limit_bytes=VMEM_LIMIT),
        name="decode_attend",
    )(page_table, cache_nsa, qblk, gpad, kcvc, newr, cache_win, cb, sb, wb, rel0, onehot, hmask)
    return (out[..., :HEAD_DIM] + out[..., HEAD_DIM:]).reshape(bs, 1, NSA_HEADS * HEAD_DIM)


_PEER_CANDS = [(a, b) for a in range(PEER_TOPK) for b in range(PEER_TOPK) if (a + 1) * (b + 1) <= PEER_TOPK]
_PEER_NCAND = len(_PEER_CANDS)
_PEER_NCAND_PAD = -(-_PEER_NCAND // 8) * 8
_PEER_GROUP_START = [min(c for c, (a, _) in enumerate(_PEER_CANDS) if a == aa) for aa in range(PEER_TOPK)]
_PEER_GROUP_LEN = [sum(1 for (a, _) in _PEER_CANDS if a == aa) for aa in range(PEER_TOPK)]


def _peer_route_kernel(xT_ref, wqT_ref, keys_ref, r2_ref, beta_ref, alpha_ref, lam_ref,
                       qT_ref, s_ref, rk_ref, vals_ref, cand_ref, sel_ref):
    tl = xT_ref.shape[1]
    qT_ref[...] = jnp.dot(wqT_ref[...], xT_ref[...], preferred_element_type=F32).astype(BF16)
    iota_k = lax.broadcasted_iota(jnp.int32, (N_KEYS, tl), 0).astype(F32)

    def head_body(h, carry):
        for p in range(2):
            row0 = pl.multiple_of(h * PEER_DK + p * (PEER_DK // 2), PEER_DK // 2)
            qs = qT_ref[pl.ds(row0, PEER_DK // 2), :]
            s = jnp.dot(keys_ref[2 * h + p], qs, preferred_element_type=F32)
            s_ref[p] = s
            cur = s
            rk = jnp.full((N_KEYS, tl), float(PEER_TOPK), F32)
            for a in range(PEER_TOPK):
                m = jnp.max(cur, axis=0, keepdims=True)
                idx = jnp.min(jnp.where(cur == m, iota_k, float(N_KEYS)), axis=0, keepdims=True)
                hit = iota_k == idx
                rk = jnp.where(hit, float(a), rk)
                cur = jnp.where(hit, NEG_INF, cur)
                vals_ref[p, a:a + 1, :] = m
            rk_ref[p] = rk
        for c, (a, b) in enumerate(_PEER_CANDS):
            cand_ref[c:c + 1, :] = vals_ref[0, a:a + 1, :] + vals_ref[1, b:b + 1, :]
        if _PEER_NCAND_PAD > _PEER_NCAND:
            cand_ref[_PEER_NCAND:_PEER_NCAND_PAD, :] = jnp.full((_PEER_NCAND_PAD - _PEER_NCAND, tl), NEG_INF, F32)
        ngrp = _PEER_NCAND_PAD // 8
        iota8 = lax.broadcasted_iota(jnp.int32, (8, tl), 0)
        ranks = [jnp.zeros((8, tl), F32) for _ in range(ngrp)]
        for cp in range(_PEER_NCAND):
            rowb = cand_ref[cp:cp + 1, :]
            for k in range(ngrp):
                blk = cand_ref[8 * k:8 * k + 8, :]
                if 8 * k > cp:
                    inc = jnp.where(rowb >= blk, 1.0, 0.0)
                elif 8 * k + 7 < cp:
                    inc = jnp.where(rowb > blk, 1.0, 0.0)
                else:
                    inc = jnp.where(iota8 + 8 * k > cp, jnp.where(rowb >= blk, 1.0, 0.0), jnp.where(rowb > blk, 1.0, 0.0))
                ranks[k] = ranks[k] + inc
        top = cand_ref[0:1, :]
        z = jnp.zeros((1, tl), F32)
        for k in range(ngrp):
            blk = cand_ref[8 * k:8 * k + 8, :]
            selk = ranks[k] < float(PEER_TOPK)
            sel_ref[8 * k:8 * k + 8, :] = jnp.where(selk, 1.0, 0.0)
            z = z + jnp.sum(jnp.where(selk, jnp.exp(blk - top), 0.0), axis=0, keepdims=True)
        rk1 = rk_ref[0]
        lam = jnp.full((N_KEYS, tl), -1.0, F32)
        for a in range(PEER_TOPK):
            g0, gl = _PEER_GROUP_START[a], _PEER_GROUP_LEN[a]
            la = jnp.sum(sel_ref[g0:g0 + gl, :], axis=0, keepdims=True) - 1.0
            lam = jnp.where(rk1 == float(a), la, lam)
        alpha = jnp.where(rk1 < float(PEER_TOPK), jnp.exp(s_ref[0] - vals_ref[0, 0:1, :]), 0.0)
        rk2 = rk_ref[1]
        beta = jnp.where(rk2 < float(PEER_TOPK), jnp.exp(s_ref[1] - vals_ref[1, 0:1, :]), 0.0) / z
        r2_ref[h] = rk2.astype(BF16)
        beta_ref[h] = beta.astype(BF16)
        alpha_ref[h] = alpha
        lam_ref[h] = lam
        return carry

    lax.fori_loop(0, PEER_HEADS, head_body, 0)


def peer_route(xT, wqT, keys2):
    d, n = xT.shape
    tl = LANES
    assert n % tl == 0
    nq = PEER_HEADS * PEER_DK
    out_bf = jax.ShapeDtypeStruct((PEER_HEADS, N_KEYS, n), BF16)
    out_f = jax.ShapeDtypeStruct((PEER_HEADS, N_KEYS, n), F32)
    tab_spec = pl.BlockSpec((PEER_HEADS, N_KEYS, tl), lambda i: (0, 0, i))
    return pl.pallas_call(
        _peer_route_kernel,
        out_shape=(out_bf, out_bf, out_f, out_f),
        grid=(n // tl,),
        in_specs=[pl.BlockSpec((d, tl), lambda i: (0, i)),
                  pl.BlockSpec((nq, d), lambda i: (0, 0)),
                  pl.BlockSpec((2 * PEER_HEADS, N_KEYS, PEER_DK // 2), lambda i: (0, 0, 0))],
        out_specs=(tab_spec, tab_spec, tab_spec, tab_spec),
        scratch_shapes=[pltpu.VMEM((nq, tl), BF16),
                        pltpu.VMEM((2, N_KEYS, tl), F32),
                        pltpu.VMEM((2, N_KEYS, tl), F32),
                        pltpu.VMEM((2, PEER_TOPK, tl), F32),
                        pltpu.VMEM((_PEER_NCAND_PAD, tl), F32),
                        pltpu.VMEM((_PEER_NCAND_PAD, tl), F32)],
        compiler_params=pltpu.CompilerParams(dimension_semantics=("arbitrary",), vmem_limit_bytes=VMEM_LIMIT),
        name="peer_route",
    )(xT, wqT, keys2)


def _gelu_tanh(x):
    return 0.5 * x * (1.0 + jnp.tanh(math.sqrt(2.0 / math.pi) * (x + 0.044715 * (x * x * x))))


def _peer_dense_kernel(xT_ref, u_ref, vT_ref, r2_ref, beta_ref, alpha_ref, lam_ref, yT_ref, a_ref, h_ref):
    j = pl.program_id(1)
    te = u_ref.shape[0]
    tl = xT_ref.shape[1]

    @pl.when(j == 0)
    def _():
        yT_ref[...] = jnp.zeros_like(yT_ref)

    n_sub = te // N_KEYS

    def pre_activation(r):
        rows = slice(r * N_KEYS, (r + 1) * N_KEYS)
        a_ref[rows, :] = jnp.dot(u_ref[rows, :], xT_ref[...], preferred_element_type=F32)

    pre_activation(0)
    for r in range(n_sub):
        rows = slice(r * N_KEYS, (r + 1) * N_KEYS)
        if r + 1 < n_sub:
            pre_activation(r + 1)
        g = jnp.zeros((N_KEYS, tl), BF16)
        for h in range(PEER_HEADS):
            lam = lam_ref[h, r:r + 1, :].astype(BF16)
            alp = alpha_ref[h, r:r + 1, :].astype(BF16)
            g = g + jnp.where(r2_ref[h] <= lam, beta_ref[h], jnp.zeros((), BF16)) * alp
        h_ref[rows, :] = _gelu_tanh(a_ref[rows, :]).astype(BF16) * g
        if r % PEER_SUB_PER_DOT == PEER_SUB_PER_DOT - 1:
            cols = slice((r + 1 - PEER_SUB_PER_DOT) * N_KEYS, (r + 1) * N_KEYS)
            yT_ref[...] += jnp.dot(vT_ref[:, cols], h_ref[cols, :], preferred_element_type=F32)


def peer_dense(xT, u_bf, vT_bf, r2, beta, alpha, lam, tl, te):
    d, n = xT.shape
    e = u_bf.shape[0]
    assert n % tl == 0 and e % te == 0 and te % (8 * N_KEYS) == 0
    tab_spec = pl.BlockSpec((PEER_HEADS, N_KEYS, tl), lambda i, j: (0, 0, i))
    row_spec = pl.BlockSpec((PEER_HEADS, te // N_KEYS, tl), lambda i, j: (0, j, i))
    return pl.pallas_call(
        _peer_dense_kernel,
        out_shape=jax.ShapeDtypeStruct((d, n), F32),
        grid=(n // tl, e // te),
        in_specs=[pl.BlockSpec((d, tl), lambda i, j: (0, i)),
                  pl.BlockSpec((te, d), lambda i, j: (j, 0)),
                  pl.BlockSpec((d, te), lambda i, j: (0, j)),
                  tab_spec, tab_spec, row_spec, row_spec],
        out_specs=pl.BlockSpec((d, tl), lambda i, j: (0, i)),
        scratch_shapes=[pltpu.VMEM((te, tl), F32), pltpu.VMEM((te, tl), BF16)],
        compiler_params=pltpu.CompilerParams(dimension_semantics=("arbitrary", "arbitrary"), vmem_limit_bytes=VMEM_LIMIT),
        name="peer_dense",
    )(xT, u_bf, vT_bf, r2, beta, alpha, lam)


def peer_ffn(h, peer_w):
    wqT, keys2, u_bf, vT_bf = peer_w
    b, t, d = h.shape
    n = b * t
    tl = PEER_TL if n % PEER_TL == 0 else LANES
    npad = -(-n // tl) * tl
    xT = jnp.pad(h.reshape(n, d).astype(BF16).T, ((0, 0), (0, npad - n)))
    r2, beta, alpha, lam = peer_route(xT, wqT, keys2)
    yT = peer_dense(xT, u_bf, vT_bf, r2, beta, alpha, lam, tl, PEER_TE)
    return yT.T[:n].reshape(b, t, d)


def token_mixers(parts, pos, rel_table, w_cmp, conv_w, conv_b, ret_g, past, nsa_tables):
    nq, nkv, ngate, rq, rk, rv, rg, cb, cc, ch = parts
    b, t = nq.shape[:2]
    q = nq.reshape(b, t, NSA_KV_HEADS, NSA_QPG, HEAD_DIM)
    kv = nkv.reshape(b, t, 6, NSA_KV_HEADS, HEAD_DIM)
    if past is None:
        kc = compress(kv[:, :, 0], w_cmp[0])
        vc = compress(kv[:, :, 1], w_cmp[1])
        o_nsa = nsa_prompt(nq, kv, ngate, kc, vc, nsa_tables)
        win_rows = kv[:, t - min(WINDOW, t):, 4:]
        s0 = jnp.zeros((b, RET_HEADS, RET_DK, RET_DV), jnp.float32)
        zbuf = jnp.zeros((b, CONV_W - 1, CONV_CH), ch.dtype)
        chunk = RET_CHUNK
    else:
        decode_nsa, win_buf, s0, zbuf = past
        o_nsa = decode_nsa(nq, nkv, ngate)
        wrows = jnp.concatenate([win_buf.astype(kv.dtype), kv[:, :, 4:]], axis=1)
        win_rows = wrows[:, wrows.shape[1] - min(WINDOW, wrows.shape[1]):]
        chunk = t
    rqh = rotary(rq.reshape(b, t, RET_HEADS, RET_DK), pos)
    rkh = rotary(rk.reshape(b, t, RET_HEADS, RET_DK), pos) * (RET_DK ** -0.5)
    rvh = rv.reshape(b, t, RET_HEADS, RET_DV)
    o_r, s_new = retention(rqh, rkh, rvh, s0, chunk)
    mu = jnp.mean(o_r, -1, keepdims=True)
    var = jnp.mean(jnp.square(o_r - mu), -1, keepdims=True)
    on = (o_r - mu) * lax.rsqrt(var + LN_EPS) * ret_g.reshape(RET_HEADS, RET_DV)
    o_ret = (on.reshape(b, t, RET_WIDTH) * jax.nn.silu(rg.astype(jnp.float32))).astype(nq.dtype)
    z = cc * ch
    zp = jnp.concatenate([zbuf.astype(z.dtype), z], axis=1)
    y = conv_b + sum(zp[:, j:j + t] * conv_w[j] for j in range(CONV_W))
    o_conv = (cb * y).astype(nq.dtype)
    mix = jnp.concatenate([o_nsa, o_ret, o_conv], axis=-1)
    return mix, (kv[:, :, :4], win_rows, s_new, zp[:, t:])


def kernel(x_prompt, x_sample, cache_nsa_kv, cache_win_kv, state_ret, state_conv, page_table, c_prompt, c_sample, rel_bias, w_ada, b_ada, w_in, w_cmp, conv_w, conv_b, ret_norm_g, w_out, ln1_g, ln1_b, ln2_g, ln2_b, peer_wq, peer_keys, peer_u, peer_v):
    alpha = (2.0 * DEPTH) ** 0.25
    n_pages = page_table.shape[1]
    past_len = n_pages * PAGE_SIZE
    sp = x_prompt.shape[1]
    bs, ts = x_sample.shape[:2]
    pos_p = jnp.arange(sp, dtype=jnp.int32)
    pos_s = past_len + jnp.arange(ts, dtype=jnp.int32)
    nsa_tables = nsa_bias_tables(rel_bias, sp, NSA_TQ)
    dec_tables = decode_tables(rel_bias, past_len)
    cache_nsa = cache_nsa_kv.transpose(0, 1, 3, 4, 5, 2).reshape(cache_nsa_kv.shape[0], cache_nsa_kv.shape[1], ROW_WIDTH, PAGE_SIZE)
    cache_win = cache_win_kv.transpose(0, 1, 3, 4, 5, 2).reshape(cache_win_kv.shape[0], bs, HALF_WIDTH, cache_win_kv.shape[2])

    def run_layer(x, c, l, pos, past, peer_w):
        m = (jax.nn.silu(c) @ w_ada[l] + b_ada[l]).reshape(c.shape[0], 6, 1, D_MODEL)
        h = x * (1.0 + m[:, 1]) + m[:, 0]
        parts = split_proj(h @ w_in[l])
        mix, st = token_mixers(parts, pos, rel_bias, w_cmp[l], conv_w[l], conv_b[l], ret_norm_g[l], past, nsa_tables)
        x = layer_norm(alpha * x + (1.0 + m[:, 2]) * (mix @ w_out[l]), ln1_g[l], ln1_b[l])
        h = x * (1.0 + m[:, 4]) + m[:, 3]
        y = peer_ffn(h, peer_w)
        x = layer_norm(alpha * x + (1.0 + m[:, 5]) * y, ln2_g[l], ln2_b[l])
        return x, st

    xp, xs = x_prompt, x_sample
    nkv_p, nkv_s, win_p, win_s, ret_p, ret_s, conv_p, conv_s = [], [], [], [], [], [], [], []
    for l in range(DEPTH):
        peer_w = (peer_wq[l].T.astype(BF16),
                  peer_keys[l].reshape(2 * PEER_HEADS, N_KEYS, PEER_DK // 2).astype(BF16),
                  peer_u[l].astype(BF16),
                  peer_v[l].T.astype(BF16))
        xp, st_p = run_layer(xp, c_prompt, l, pos_p, None, peer_w)
        decode_nsa = functools.partial(nsa_decode, page_table=page_table, cache_nsa=cache_nsa, cache_win=cache_win,
                                       w_bd=compress_block_weights(w_cmp[l]), tables=dec_tables, layer=l)
        xs, st_s = run_layer(xs, c_sample, l, pos_s, (decode_nsa, cache_win_kv[l], state_ret[l], state_conv[l]), peer_w)
        nkv_p.append(st_p[0])
        win_p.append(st_p[1])
        ret_p.append(st_p[2])
        conv_p.append(st_p[3])
        nkv_s.append(st_s[0])
        win_s.append(st_s[1])
        ret_s.append(st_s[2])
        conv_s.append(st_s[3])
    return (xp, xs, jnp.stack(nkv_p), jnp.stack(nkv_s), jnp.stack(win_p), jnp.stack(win_s), jnp.stack(ret_p), jnp.stack(ret_s), jnp.stack(conv_p), jnp.stack(conv_s))
```

```python
import math
import functools
import jax
import jax.numpy as jnp
from jax import lax
import numpy as np
from jax.experimental import pallas as pl
from jax.experimental.pallas import tpu as pltpu

D_MODEL = 1024
BATCH = 4
SEQ = 4096
DEPTH = 2
DEC_BATCH = 32
DEC_SEQ = 1
PAST_LEN = 8192
PAGE_SIZE = 128

HEAD_DIM = 64
NSA_WIDTH = D_MODEL // 2
NSA_HEADS = NSA_WIDTH // HEAD_DIM
NSA_KV_HEADS = 2
NSA_QPG = NSA_HEADS // NSA_KV_HEADS
KV_WIDTH = NSA_KV_HEADS * HEAD_DIM
SCALE = HEAD_DIM ** -0.5
L_CMP = 32
L_SEL = 64
N_SEL = 16
WINDOW = 512
Q_BLOCK = 128
SEL_Q_BLOCK = 64
N_BUCKETS = 32
REL_MAX_DIST = 128
RET_WIDTH = D_MODEL // 4
RET_DK = 64
RET_DV = 64
RET_HEADS = RET_WIDTH // RET_DV
RET_CHUNK = 128
ROPE_BASE = 10000.0
CONV_CH = D_MODEL // 4
CONV_W = 3
MIX_WIDTH = NSA_WIDTH + RET_WIDTH + CONV_CH
PEER_HEADS = 8
PEER_DK = 256
N_KEYS = 128
N_EXPERTS = N_KEYS * N_KEYS
PEER_TOPK = 16
PEER_CHUNK = 256
LN_EPS = 1e-5
SPLIT_SIZES = (NSA_WIDTH, 6 * KV_WIDTH, 3 * NSA_HEADS, RET_HEADS * RET_DK, RET_HEADS * RET_DK, RET_WIDTH, RET_WIDTH, CONV_CH, CONV_CH, CONV_CH)
N_IN = sum(SPLIT_SIZES)

F32 = jnp.float32
BF16 = jnp.bfloat16
NEG_INF = float('-inf')
MASK_NEG = -1e9
CMP_MASK_NEG = -1e30
NSA_TQ = 256
NSA_RB = 128
ROW_WIDTH = 4 * KV_WIDTH
HALF_WIDTH = 2 * KV_WIDTH
PEER_TL = 512
PEER_TE = 1024
PEER_SUB_PER_DOT = 2
LANES = 128
VMEM_LIMIT = 56 * 1024 * 1024


def _ln_kernel(x_ref, g_ref, b_ref, o_ref):
    x = x_ref[...]
    mu = jnp.mean(x, -1, keepdims=True)
    xc = x - mu
    var = jnp.mean(xc * xc, -1, keepdims=True)
    o_ref[...] = xc * lax.rsqrt(var + LN_EPS) * g_ref[...] + b_ref[...]


def layer_norm(x, g, b):
    shp = x.shape
    x2 = x.reshape(-1, shp[-1])
    n = x2.shape[0]
    tm = min(n, 512)
    out = pl.pallas_call(
        _ln_kernel,
        out_shape=jax.ShapeDtypeStruct(x2.shape, jnp.float32),
        grid=(n // tm,),
        in_specs=[pl.BlockSpec((tm, shp[-1]), lambda i: (i, 0)),
                  pl.BlockSpec((1, shp[-1]), lambda i: (0, 0)),
                  pl.BlockSpec((1, shp[-1]), lambda i: (0, 0))],
        out_specs=pl.BlockSpec((tm, shp[-1]), lambda i: (i, 0)),
        name="layer_norm",
    )(x2, g.reshape(1, -1), b.reshape(1, -1))
    return out.reshape(shp)


def t5_bucket(dist):
    n = jnp.maximum(dist, 0)
    exact = N_BUCKETS // 2
    nf = jnp.maximum(n, exact).astype(jnp.float32)
    big = exact + (jnp.log(nf / exact) / math.log(REL_MAX_DIST / exact) * (N_BUCKETS - exact)).astype(jnp.int32)
    return jnp.where(n < exact, n, jnp.minimum(big, N_BUCKETS - 1))


def rel_bias_lookup(rel_bias, dist):
    onehot = jax.nn.one_hot(t5_bucket(dist), N_BUCKETS, dtype=jnp.float32)
    return jnp.einsum('...k,kh->...h', onehot, rel_bias, precision=lax.Precision.HIGHEST)


def rotary(x, pos):
    half = x.shape[-1] // 2
    inv = ROPE_BASE ** (-jnp.arange(half, dtype=jnp.float32) / half)
    ang = pos.astype(jnp.float32)[:, None] * inv[None, :]
    cos = jnp.cos(ang)[None, :, None, :]
    sin = jnp.sin(ang)[None, :, None, :]
    x1, x2 = x[..., :half], x[..., half:]
    return jnp.concatenate([x1 * cos - x2 * sin, x1 * sin + x2 * cos], axis=-1)


def split_proj(p):
    outs, start = [], 0
    for size in SPLIT_SIZES:
        outs.append(p[..., start:start + size])
        start += size
    return outs


def compress(rows, w):
    b, lp, g, d = rows.shape
    blk = rows.reshape(b, lp // L_CMP, L_CMP, g, d).transpose(0, 1, 3, 2, 4).reshape(b, lp // L_CMP, g, L_CMP * d)
    return blk @ w


def retention(q, k, v, s0, chunk):
    b, t = q.shape[:2]
    nc = t // chunk
    lg = jnp.log(1.0 - 2.0 ** (-5.0 - jnp.arange(RET_HEADS, dtype=jnp.float32)))
    i = jnp.arange(chunk, dtype=jnp.float32)
    diff = i[:, None] - i[None, :]
    dmat = jnp.where(diff >= 0, jnp.exp(jnp.maximum(diff, 0.0)[None] * lg[:, None, None]), 0.0)
    q_dec = jnp.exp((i + 1.0)[:, None] * lg[None, :])[None, :, :, None]
    k_dec = jnp.exp((chunk - 1.0 - i)[:, None] * lg[None, :])[None, :, :, None]
    s_dec = jnp.exp(chunk * lg)[None, :, None, None]

    def to_chunks(a):
        return jnp.moveaxis(a.astype(jnp.float32).reshape(b, nc, chunk, RET_HEADS, a.shape[-1]), 1, 0)

    def step(s, xs):
        qc, kc, vc = xs
        att = jnp.einsum('bihd,bjhd->bhij', qc, kc) * dmat
        o = jnp.einsum('bhij,bjhe->bihe', att, vc) + jnp.einsum('bihd,bhde->bihe', qc * q_dec, s)
        s = s * s_dec + jnp.einsum('bjhd,bjhe->bhde', kc * k_dec, vc)
        return s, o

    s, o = lax.scan(step, s0.astype(jnp.float32), (to_chunks(q), to_chunks(k), to_chunks(v)))
    return jnp.moveaxis(o, 0, 1).reshape(b, t, RET_HEADS, RET_DV), s


def _nsa_prompt_kernel(q_ref, gate_ref, kcT_ref, vc_ref, cb_ref, ksT_ref, vs_ref, kwT_ref, vw_ref, sb_ref, wb_ref,
                       o_ref, qa_ref, m_ref, acc_ref, oc_ref, osel_ref, owin_ref):
    qi = pl.program_id(2)
    tq = q_ref.shape[3]
    ncb = kcT_ref.shape[3]
    ka = ksT_ref.shape[2]
    t0 = qi * tq
    rows = NSA_QPG * tq

    imp = jnp.zeros((tq, ncb), F32)
    for h in range(NSA_QPG):
        bias = cb_ref[0, h]
        lg = jnp.dot(q_ref[0, 0, h], kcT_ref[0, 0], preferred_element_type=F32) + bias
        mx = jnp.max(lg, axis=-1, keepdims=True)
        e = jnp.where(bias > 0.5 * CMP_MASK_NEG, jnp.exp(lg - mx), 0.0)
        p = e / jnp.maximum(jnp.sum(e, axis=-1, keepdims=True), 1e-30)
        imp = imp + p
        oc_ref[h] = jnp.dot(p.astype(BF16), vc_ref[0, 0], preferred_element_type=F32)

    lane = lax.broadcasted_iota(jnp.int32, (tq, ncb), 1)
    tpos = t0 + lax.broadcasted_iota(jnp.int32, (tq, ncb), 0)
    pair = imp + pltpu.roll(imp, ncb - 1, 1)
    blk = lane >> 1
    cur = tpos >> 6
    forced = (blk == 0) | (blk == cur) | (blk == cur - 1)
    cand = ((lane & 1) == 0) & (blk <= cur)
    score = jnp.where(cand, jnp.where(forced, jnp.inf, pair), NEG_INF)
    lane_f = lane.astype(F32)
    chosen = jnp.zeros((tq, ncb), F32)
    for _ in range(N_SEL):
        mx = jnp.max(score, axis=-1, keepdims=True)
        first = jnp.min(jnp.where(score == mx, lane_f, float(ncb)), axis=-1, keepdims=True)
        hit = (lane_f == first) & (mx > NEG_INF)
        chosen = jnp.where(hit, 1.0, chosen)
        score = jnp.where(hit, NEG_INF, score)
    blockmask = jnp.where(chosen > 0.0, 0.0, MASK_NEG).astype(BF16)

    for h in range(NSA_QPG):
        qa_ref[h * tq:(h + 1) * tq, 0:ncb] = blockmask
        qa_ref[h * tq:(h + 1) * tq, ncb:ncb + HEAD_DIM] = q_ref[0, 0, h]
        if ka > ncb + HEAD_DIM:
            qa_ref[h * tq:(h + 1) * tq, ncb + HEAD_DIM:ka] = jnp.zeros((tq, ka - ncb - HEAD_DIM), BF16)

    nrb = rows // NSA_RB
    per_head = tq // NSA_RB

    def reset():
        m_ref[...] = jnp.full((rows, LANES), CMP_MASK_NEG, F32)
        acc_ref[...] = jnp.zeros((rows, LANES), F32)

    def attn_step(rb, qrb, k_t, v, bias):
        r0 = rb * NSA_RB
        s = jnp.dot(qrb, k_t, preferred_element_type=F32)
        if bias is not None:
            s = s + bias
        parts = [s[:, i * LANES:(i + 1) * LANES] for i in range(s.shape[1] // LANES)]
        red = parts[0]
        for part in parts[1:]:
            red = jnp.maximum(red, part)
        m_old = m_ref[r0:r0 + NSA_RB, :]
        m_new = jnp.maximum(m_old, jnp.max(red, axis=-1, keepdims=True))
        p = jnp.concatenate([jnp.exp(part - m_new) for part in parts], axis=1).astype(BF16)
        acc_ref[r0:r0 + NSA_RB, :] = (jnp.exp(m_old - m_new) * acc_ref[r0:r0 + NSA_RB, :]
                                      + jnp.dot(p, v, preferred_element_type=F32))
        m_ref[r0:r0 + NSA_RB, :] = m_new

    def finish(dst_ref):
        acc = acc_ref[...]
        dst_ref[...] = acc[:, 0:HEAD_DIM] / acc[:, HEAD_DIM:HEAD_DIM + 1]

    reset()

    def far_body(c, carry):
        col = pl.multiple_of((c + 1) * tq, tq)
        k_t = ksT_ref[0, 0, :, pl.ds(col, tq)]
        v = vs_ref[0, 0, pl.ds(col, tq), :]
        for rb in range(nrb):
            attn_step(rb, qa_ref[rb * NSA_RB:(rb + 1) * NSA_RB, :], k_t, v, None)
        return carry

    lax.fori_loop(0, jnp.maximum(qi - 1, 0), far_body, 0)

    col0 = pl.multiple_of(t0, tq)
    for c in range(2):
        colc = pl.multiple_of(col0 + c * tq, tq)
        k_t = ksT_ref[0, 0, :, pl.ds(colc, tq)]
        v = vs_ref[0, 0, pl.ds(colc, tq), :]
        jn = lax.broadcasted_iota(jnp.int32, (1, tq), 1) + c * tq
        colmask = jnp.where(jn + (t0 - tq) >= 0, 0.0, MASK_NEG)
        for rb in range(nrb):
            h, part = rb // per_head, rb % per_head
            bias = sb_ref[0, h, part * NSA_RB:(part + 1) * NSA_RB, c * tq:(c + 1) * tq] + colmask
            attn_step(rb, qa_ref[rb * NSA_RB:(rb + 1) * NSA_RB, :], k_t, v, bias)
    finish(osel_ref)

    reset()
    for c in range((WINDOW + tq) // tq):
        colc = pl.multiple_of(col0 + c * tq, tq)
        k_t = kwT_ref[0, 0, :, pl.ds(colc, tq)]
        v = vw_ref[0, 0, pl.ds(colc, tq), :]
        jw = lax.broadcasted_iota(jnp.int32, (1, tq), 1) + c * tq
        wmask = jnp.where(jw + (t0 - WINDOW) >= 0, 0.0, MASK_NEG)
        for rb in range(nrb):
            h, part = rb // per_head, rb % per_head
            bias = wb_ref[0, h, part * NSA_RB:(part + 1) * NSA_RB, c * tq:(c + 1) * tq] + wmask
            attn_step(rb, qa_ref[rb * NSA_RB:(rb + 1) * NSA_RB, ncb:ncb + HEAD_DIM], k_t, v, bias)
    finish(owin_ref)

    g = jax.nn.sigmoid(gate_ref[0, 0])
    for h in range(NSA_QPG):
        o_h = (g[:, 3 * h:3 * h + 1] * oc_ref[h]
               + g[:, 3 * h + 1:3 * h + 2] * osel_ref[h * tq:(h + 1) * tq, :]
               + g[:, 3 * h + 2:3 * h + 3] * owin_ref[h * tq:(h + 1) * tq, :])
        o_ref[0, :, h * HEAD_DIM:(h + 1) * HEAD_DIM] = o_h


def nsa_bias_tables(rel_bias, t, tq):
    ncb = t // L_CMP

    def heads_first(tab):
        return tab.transpose(2, 0, 1).reshape(NSA_KV_HEADS, NSA_QPG, tab.shape[0], tab.shape[1])

    pos = jnp.arange(t, dtype=jnp.int32)
    end = jnp.arange(ncb, dtype=jnp.int32) * L_CMP + (L_CMP - 1)
    dist = pos[:, None] - end[None, :]
    cb = jnp.where((dist >= 0)[..., None], rel_bias_lookup(rel_bias, dist), CMP_MASK_NEG)
    i = jnp.arange(tq, dtype=jnp.int32)
    dist = i[:, None] + tq - jnp.arange(2 * tq, dtype=jnp.int32)[None, :]
    sb = jnp.where((dist >= 0)[..., None], rel_bias_lookup(rel_bias, dist) - rel_bias[N_BUCKETS - 1], MASK_NEG)
    dist = i[:, None] + WINDOW - jnp.arange(WINDOW + tq, dtype=jnp.int32)[None, :]
    wb = jnp.where(((dist >= 0) & (dist < WINDOW))[..., None], rel_bias_lookup(rel_bias, dist), MASK_NEG)
    return heads_first(cb), heads_first(sb), heads_first(wb)


def nsa_prompt(q, kv, gates, kc, vc, tables):
    b, t = q.shape[:2]
    tq = NSA_TQ
    ncb = t // L_CMP
    ka = -(-(ncb + HEAD_DIM) // LANES) * LANES
    cb, sb, wb = tables
    q4 = (q * SCALE).astype(BF16).reshape(b, t, NSA_KV_HEADS, NSA_QPG, HEAD_DIM).transpose(0, 2, 3, 1, 4)
    g4 = gates.reshape(b, t, NSA_KV_HEADS, NSA_QPG * 3).transpose(0, 2, 1, 3)
    kcT = kc.astype(BF16).transpose(0, 2, 3, 1)
    vcg = vc.astype(BF16).transpose(0, 2, 1, 3)
    kvb = kv.astype(BF16)
    onehot = (2 * (jnp.arange(t, dtype=jnp.int32) // L_SEL)[None, :] == jnp.arange(ncb, dtype=jnp.int32)[:, None]).astype(BF16)
    ks_t = kvb[:, :, 2].transpose(0, 2, 3, 1)
    ksT = jnp.concatenate([jnp.broadcast_to(onehot, (b, NSA_KV_HEADS, ncb, t)), ks_t,
                           jnp.zeros((b, NSA_KV_HEADS, ka - ncb - HEAD_DIM, t), BF16)], axis=2)
    ksT = jnp.pad(ksT, ((0, 0), (0, 0), (0, 0), (tq, 0)))
    def with_ones(v):
        one = jnp.ones(v.shape[:-1] + (1,), BF16)
        return jnp.concatenate([v, one, jnp.zeros(v.shape[:-1] + (LANES - HEAD_DIM - 1,), BF16)], axis=-1)

    vs = jnp.pad(with_ones(kvb[:, :, 3].transpose(0, 2, 1, 3)), ((0, 0), (0, 0), (tq, 0), (0, 0)))
    kwT = jnp.pad(kvb[:, :, 4].transpose(0, 2, 3, 1), ((0, 0), (0, 0), (0, 0), (WINDOW, 0)))
    vw = jnp.pad(with_ones(kvb[:, :, 5].transpose(0, 2, 1, 3)), ((0, 0), (0, 0), (WINDOW, 0), (0, 0)))
    rows = NSA_QPG * tq

    def per_bg(shape):
        return pl.BlockSpec((1, 1) + shape, lambda bi, gi, qi: (bi, gi, 0, 0))

    return pl.pallas_call(
        _nsa_prompt_kernel,
        out_shape=jax.ShapeDtypeStruct((b, t, NSA_HEADS * HEAD_DIM), F32),
        grid=(b, NSA_KV_HEADS, t // tq),
        in_specs=[pl.BlockSpec((1, 1, NSA_QPG, tq, HEAD_DIM), lambda bi, gi, qi: (bi, gi, 0, qi, 0)),
                  pl.BlockSpec((1, 1, tq, NSA_QPG * 3), lambda bi, gi, qi: (bi, gi, qi, 0)),
                  per_bg((HEAD_DIM, ncb)), per_bg((ncb, HEAD_DIM)),
                  pl.BlockSpec((1, NSA_QPG, tq, ncb), lambda bi, gi, qi: (gi, 0, qi, 0)),
                  per_bg((ka, tq + t)), per_bg((tq + t, LANES)),
                  per_bg((HEAD_DIM, WINDOW + t)), per_bg((WINDOW + t, LANES)),
                  pl.BlockSpec((1, NSA_QPG, tq, 2 * tq), lambda bi, gi, qi: (gi, 0, 0, 0)),
                  pl.BlockSpec((1, NSA_QPG, tq, WINDOW + tq), lambda bi, gi, qi: (gi, 0, 0, 0))],
        out_specs=pl.BlockSpec((1, tq, NSA_QPG * HEAD_DIM), lambda bi, gi, qi: (bi, qi, gi)),
        scratch_shapes=[pltpu.VMEM((rows, ka), BF16),
                        pltpu.VMEM((rows, LANES), F32), pltpu.VMEM((rows, LANES), F32),
                        pltpu.VMEM((NSA_QPG, tq, HEAD_DIM), F32),
                        pltpu.VMEM((rows, HEAD_DIM), F32), pltpu.VMEM((rows, HEAD_DIM), F32)],
        compiler_params=pltpu.CompilerParams(dimension_semantics=("arbitrary", "arbitrary", "arbitrary"),
                                             vmem_limit_bytes=VMEM_LIMIT),
        name="nsa_prompt",
    )(q4, g4, kcT, vcg, cb, ksT, vs, kwT, vw, sb, wb)


def _gather_pages(pt_ref, cache_ref, sem, layer, feat0, dst_of):
    s = pl.program_id(0)
    n_pages = pt_ref.shape[1]
    slot = lax.rem(s, 2)

    def copy(page, sl, p):
        return pltpu.make_async_copy(cache_ref.at[layer, page, pl.ds(feat0, HALF_WIDTH), :], dst_of(sl, p), sem.at[sl])

    def start(seq, sl):
        for p in range(n_pages):
            copy(pt_ref[seq, p], sl, p).start()

    @pl.when(s == 0)
    def _():
        start(0, 0)

    @pl.when(s + 1 < pl.num_programs(0))
    def _():
        start(s + 1, 1 - slot)

    for p in range(n_pages):
        copy(0, slot, p).wait()
    return slot


def _decode_compress_kernel(pt_ref, cache_ref, w_ref, out_ref, buf, rows_ref, sem, *, layer):
    slot = _gather_pages(pt_ref, cache_ref, sem, layer, 0, lambda sl, p: buf.at[sl, p])
    ncb = out_ref.shape[1]
    n_pages = pt_ref.shape[1]
    eye = (lax.broadcasted_iota(jnp.int32, (PAGE_SIZE, PAGE_SIZE), 0)
           == lax.broadcasted_iota(jnp.int32, (PAGE_SIZE, PAGE_SIZE), 1)).astype(F32).astype(BF16)
    for p in range(n_pages):
        x = _dot_nt(eye, buf[slot, p].astype(BF16))
        for part in range(2):
            rows_ref[part, p * PAGE_SIZE:(p + 1) * PAGE_SIZE, :] = x[:, part * KV_WIDTH:(part + 1) * KV_WIDTH]
    for part in range(2):
        acc = jnp.zeros((ncb, KV_WIDTH), F32)
        for r in range(L_CMP):
            x = rows_ref[part, pl.ds(r, ncb, stride=L_CMP), :]
            acc = acc + jnp.dot(x.astype(BF16), w_ref[r, part], preferred_element_type=F32)
        out_ref[0, :, part * KV_WIDTH:(part + 1) * KV_WIDTH] = acc


def decode_compress(page_table, cache, w_bd, layer):
    bs, n_pages = page_table.shape
    past = n_pages * PAGE_SIZE
    ncb = past // L_CMP
    return pl.pallas_call(
        functools.partial(_decode_compress_kernel, layer=layer),
        out_shape=jax.ShapeDtypeStruct((bs, ncb, HALF_WIDTH), F32),
        grid_spec=pltpu.PrefetchScalarGridSpec(
            num_scalar_prefetch=1,
            grid=(bs,),
            in_specs=[pl.BlockSpec(memory_space=pl.ANY),
                      pl.BlockSpec((L_CMP, 2, KV_WIDTH, KV_WIDTH), lambda s, pt: (0, 0, 0, 0))],
            out_specs=pl.BlockSpec((1, ncb, HALF_WIDTH), lambda s, pt: (s, 0, 0)),
            scratch_shapes=[pltpu.VMEM((2, n_pages, HALF_WIDTH, PAGE_SIZE), F32),
                            pltpu.VMEM((2, past, KV_WIDTH), F32), pltpu.SemaphoreType.DMA((2,))]),
        compiler_params=pltpu.CompilerParams(dimension_semantics=("arbitrary",), vmem_limit_bytes=VMEM_LIMIT),
        name="decode_compress",
    )(page_table, cache, w_bd)


def _dot_nt(a, b):
    return lax.dot_general(a, b, (((1,), (1,)), ((), ())), preferred_element_type=F32)


def _decode_attend_kernel(pt_ref, cache_ref, q_ref, gate_ref, kcvc_ref, new_ref, win_ref, cb_ref, sb_ref, wb_ref,
                          rel0_ref, onehot_ref, hmask_ref, o_ref, buf, sem, *, layer):
    slot = _gather_pages(pt_ref, cache_ref, sem, layer, HALF_WIDTH,
                         lambda sl, p: buf.at[sl, :, pl.ds(p * PAGE_SIZE, PAGE_SIZE)])
    q = q_ref[0]
    qf = q.astype(F32)
    ncb = kcvc_ref.shape[1]
    rel0 = rel0_ref[:, 0:1]

    def bf_round(x):
        return x.astype(BF16).astype(F32)

    kc = kcvc_ref[0, :, 0:KV_WIDTH].astype(BF16)
    vc = kcvc_ref[0, :, KV_WIDTH:HALF_WIDTH].astype(BF16)
    lg = _dot_nt(q, kc) + cb_ref[...]
    e = jnp.exp(lg - jnp.max(lg, axis=-1, keepdims=True))
    p = e / jnp.sum(e, axis=-1, keepdims=True)
    o_cmp = jnp.dot(p.astype(BF16), vc, preferred_element_type=F32)

    row = lax.broadcasted_iota(jnp.int32, (NSA_HEADS, ncb), 0)
    lane = lax.broadcasted_iota(jnp.int32, (NSA_HEADS, ncb), 1)
    pg0 = jnp.sum(p[0:NSA_QPG], axis=0, keepdims=True)
    pg1 = jnp.sum(p[NSA_QPG:NSA_HEADS], axis=0, keepdims=True)
    imp = jnp.where(row < NSA_QPG, pg0, pg1)
    pair = imp + pltpu.roll(imp, ncb - 1, 1)
    blk = lane >> 1
    forced = (blk == 0) | (blk == ncb // 2 - 1)
    score = jnp.where((lane & 1) == 0, jnp.where(forced, jnp.inf, pair), NEG_INF)
    lane_f = lane.astype(F32)
    chosen = jnp.zeros((NSA_HEADS, ncb), F32)
    for _ in range(N_SEL - 1):
        mx = jnp.max(score, axis=-1, keepdims=True)
        first = jnp.min(jnp.where(score == mx, lane_f, float(ncb)), axis=-1, keepdims=True)
        hit = lane_f == first
        chosen = jnp.where(hit, 1.0, chosen)
        score = jnp.where(hit, NEG_INF, score)
    blockmask = jnp.where(chosen > 0.0, 0.0, MASK_NEG).astype(BF16)

    ks_t = buf[slot, 0:KV_WIDTH, :].astype(BF16)
    vs_t = buf[slot, KV_WIDTH:HALF_WIDTH, :].astype(BF16)
    s = (jnp.dot(q, ks_t, preferred_element_type=F32)
         + jnp.dot(blockmask, onehot_ref[...], preferred_element_type=F32) + sb_ref[...])
    s_new = jnp.sum(qf * bf_round(new_ref[0, 0:1, :]), axis=-1, keepdims=True) + rel0
    m = jnp.maximum(jnp.max(s, axis=-1, keepdims=True), s_new)
    e = jnp.exp(s - m)
    e_new = jnp.exp(s_new - m)
    den = jnp.sum(e, axis=-1, keepdims=True) + e_new
    o_sel = (_dot_nt(e.astype(BF16), vs_t) + e_new * bf_round(new_ref[0, 1:2, :])) / den

    wk_t = win_ref[0, 0, 0:KV_WIDTH, :].astype(BF16)
    wv_t = win_ref[0, 0, KV_WIDTH:HALF_WIDTH, :].astype(BF16)
    sw = jnp.dot(q, wk_t, preferred_element_type=F32) + wb_ref[...]
    sw_new = jnp.sum(qf * bf_round(new_ref[0, 2:3, :]), axis=-1, keepdims=True) + rel0
    mw = jnp.maximum(jnp.max(sw, axis=-1, keepdims=True), sw_new)
    ew = jnp.exp(sw - mw)
    ew_new = jnp.exp(sw_new - mw)
    denw = jnp.sum(ew, axis=-1, keepdims=True) + ew_new
    o_win = (_dot_nt(ew.astype(BF16), wv_t) + ew_new * bf_round(new_ref[0, 3:4, :])) / denw

    g = jax.nn.sigmoid(gate_ref[0])
    o_ref[0] = (g[:, 0:1] * o_cmp + g[:, 1:2] * o_sel + g[:, 2:3] * o_win) * hmask_ref[...]


def decode_tables(rel_bias, past):
    ncb = past // L_CMP
    end = jnp.arange(ncb, dtype=jnp.int32) * L_CMP + (L_CMP - 1)
    cb = rel_bias_lookup(rel_bias, past - end).T
    sb = rel_bias_lookup(rel_bias, past - jnp.arange(past, dtype=jnp.int32)).T
    dist = WINDOW - jnp.arange(WINDOW, dtype=jnp.int32)
    wb = jnp.where((dist < WINDOW)[None, :], rel_bias_lookup(rel_bias, dist).T, MASK_NEG)
    rel0 = jnp.broadcast_to(rel_bias[0][:, None], (NSA_HEADS, LANES))
    onehot = (2 * (jnp.arange(past, dtype=jnp.int32) // L_SEL)[None, :] == jnp.arange(ncb, dtype=jnp.int32)[:, None]).astype(BF16)
    hmask = (jnp.arange(KV_WIDTH, dtype=jnp.int32)[None, :] // HEAD_DIM == jnp.arange(NSA_HEADS, dtype=jnp.int32)[:, None] // NSA_QPG).astype(F32)
    return cb, sb, wb, rel0, onehot, hmask


def compress_block_weights(w_cmp):
    w = w_cmp.reshape(2, L_CMP, HEAD_DIM, HEAD_DIM).transpose(1, 0, 2, 3)
    z = jnp.zeros_like(w)
    return jnp.concatenate([jnp.concatenate([w, z], axis=3), jnp.concatenate([z, w], axis=3)], axis=2).astype(BF16)


def nsa_decode(nq, nkv, ngate, page_table, cache_nsa, cache_win, w_bd, tables, layer):
    bs = nq.shape[0]
    n_pages = page_table.shape[1]
    past = n_pages * PAGE_SIZE
    ncb = past // L_CMP
    cb, sb, wb, rel0, onehot, hmask = tables
    kcvc = decode_compress(page_table, cache_nsa, w_bd, layer)
    qh = nq.reshape(bs, NSA_HEADS, 1, HEAD_DIM) * SCALE
    own_group = jnp.arange(NSA_KV_HEADS)[None, None, :, None] == (jnp.arange(NSA_HEADS) // NSA_QPG)[None, :, None, None]
    qblk = (qh * own_group).reshape(bs, NSA_HEADS, KV_WIDTH).astype(BF16)
    gpad = jnp.pad(ngate.reshape(bs, NSA_HEADS, 3), ((0, 0), (0, 0), (0, LANES - 3)))
    newr = jnp.pad(nkv.reshape(bs, 6, KV_WIDTH)[:, 2:6], ((0, 0), (0, 4), (0, 0)))

    def const2(shape):
        return pl.BlockSpec(shape, lambda s, pt: (0, 0))

    out = pl.pallas_call(
        functools.partial(_decode_attend_kernel, layer=layer),
        out_shape=jax.ShapeDtypeStruct((bs, NSA_HEADS, KV_WIDTH), F32),
        grid_spec=pltpu.PrefetchScalarGridSpec(
            num_scalar_prefetch=1,
            grid=(bs,),
            in_specs=[pl.BlockSpec(memory_space=pl.ANY),
                      pl.BlockSpec((1, NSA_HEADS, KV_WIDTH), lambda s, pt: (s, 0, 0)),
                      pl.BlockSpec((1, NSA_HEADS, LANES), lambda s, pt: (s, 0, 0)),
                      pl.BlockSpec((1, ncb, HALF_WIDTH), lambda s, pt: (s, 0, 0)),
                      pl.BlockSpec((1, 8, KV_WIDTH), lambda s, pt: (s, 0, 0)),
                      pl.BlockSpec((1, 1, HALF_WIDTH, WINDOW), lambda s, pt: (layer, s, 0, 0)),
                      const2((NSA_HEADS, ncb)), const2((NSA_HEADS, past)), const2((NSA_HEADS, WINDOW)),
                      const2((NSA_HEADS, LANES)), const2((ncb, past)), const2((NSA_HEADS, KV_WIDTH))],
            out_specs=pl.BlockSpec((1, NSA_HEADS, KV_WIDTH), lambda s, pt: (s, 0, 0)),
            scratch_shapes=[pltpu.VMEM((2, HALF_WIDTH, past), F32), pltpu.SemaphoreType.DMA((2,))]),
        compiler_params=pltpu.CompilerParams(dimension_semantics=("arbitrary",), vmem_limit_bytes=VMEM_LIMIT),
        name="decode_attend",
    )(page_table, cache_nsa, qblk, gpad, kcvc, newr, cache_win, cb, sb, wb, rel0, onehot, hmask)
    return (out[..., :HEAD_DIM] + out[..., HEAD_DIM:]).reshape(bs, 1, NSA_HEADS * HEAD_DIM)


_PEER_CANDS = [(a, b) for a in range(PEER_TOPK) for b in range(PEER_TOPK) if (a + 1) * (b + 1) <= PEER_TOPK]
_PEER_NCAND = len(_PEER_CANDS)
_PEER_NCAND_PAD = -(-_PEER_NCAND // 8) * 8
_PEER_GROUP_START = [min(c for c, (a, _) in enumerate(_PEER_CANDS) if a == aa) for aa in range(PEER_TOPK)]
_PEER_GROUP_LEN = [sum(1 for (a, _) in _PEER_CANDS if a == aa) for aa in range(PEER_TOPK)]


def _peer_route_kernel(xT_ref, wqT_ref, keys_ref, r2_ref, beta_ref, alpha_ref, lam_ref,
                       qT_ref, s_ref, rk_ref, vals_ref, cand_ref, sel_ref):
    tl = xT_ref.shape[1]
    qT_ref[...] = jnp.dot(wqT_ref[...], xT_ref[...], preferred_element_type=F32).astype(BF16)
    iota_k = lax.broadcasted_iota(jnp.int32, (N_KEYS, tl), 0).astype(F32)

    def rank_top16(p, stable):
        cur = s_ref[p]
        rk = jnp.full((N_KEYS, tl), float(PEER_TOPK), F32)
        for a in range(PEER_TOPK):
            m = jnp.max(cur, axis=0, keepdims=True)
            if stable:
                idx = jnp.min(jnp.where(cur == m, iota_k, float(N_KEYS)), axis=0, keepdims=True)
                hit = iota_k == idx
            else:
                hit = cur == m
            rk = jnp.where(hit, float(a), rk)
            cur = jnp.where(hit, NEG_INF, cur)
            vals_ref[p, a:a + 1, :] = m
        rk_ref[p] = rk
        return jnp.sum(jnp.where(rk < float(PEER_TOPK), 1.0, 0.0), axis=0, keepdims=True)

    def head_body(h, carry):
        taken = jnp.zeros((1, tl), F32)
        for p in range(2):
            row0 = pl.multiple_of(h * PEER_DK + p * (PEER_DK // 2), PEER_DK // 2)
            qs = qT_ref[pl.ds(row0, PEER_DK // 2), :]
            s_ref[p] = jnp.dot(keys_ref[2 * h + p], qs, preferred_element_type=F32)
            taken = jnp.maximum(taken, rank_top16(p, stable=False))

        @pl.when(jnp.max(taken) > float(PEER_TOPK))
        def _():
            for p in range(2):
                rank_top16(p, stable=True)

        for c, (a, b) in enumerate(_PEER_CANDS):
            cand_ref[c:c + 1, :] = vals_ref[0, a:a + 1, :] + vals_ref[1, b:b + 1, :]
        if _PEER_NCAND_PAD > _PEER_NCAND:
            cand_ref[_PEER_NCAND:_PEER_NCAND_PAD, :] = jnp.full((_PEER_NCAND_PAD - _PEER_NCAND, tl), NEG_INF, F32)
        ngrp = _PEER_NCAND_PAD // 8
        iota8 = lax.broadcasted_iota(jnp.int32, (8, tl), 0)
        ranks = [jnp.zeros((8, tl), F32) for _ in range(ngrp)]
        for cp in range(_PEER_NCAND):
            rowb = cand_ref[cp:cp + 1, :]
            for k in range(ngrp):
                blk = cand_ref[8 * k:8 * k + 8, :]
                if 8 * k > cp:
                    inc = jnp.where(rowb >= blk, 1.0, 0.0)
                elif 8 * k + 7 < cp:
                    inc = jnp.where(rowb > blk, 1.0, 0.0)
                else:
                    inc = jnp.where(iota8 + 8 * k > cp, jnp.where(rowb >= blk, 1.0, 0.0), jnp.where(rowb > blk, 1.0, 0.0))
                ranks[k] = ranks[k] + inc
        top = cand_ref[0:1, :]
        z = jnp.zeros((1, tl), F32)
        for k in range(ngrp):
            blk = cand_ref[8 * k:8 * k + 8, :]
            selk = ranks[k] < float(PEER_TOPK)
            sel_ref[8 * k:8 * k + 8, :] = jnp.where(selk, 1.0, 0.0)
            z = z + jnp.sum(jnp.where(selk, jnp.exp(blk - top), 0.0), axis=0, keepdims=True)
        rk1 = rk_ref[0]
        lam = jnp.full((N_KEYS, tl), -1.0, F32)
        for a in range(PEER_TOPK):
            g0, gl = _PEER_GROUP_START[a], _PEER_GROUP_LEN[a]
            la = jnp.sum(sel_ref[g0:g0 + gl, :], axis=0, keepdims=True) - 1.0
            lam = jnp.where(rk1 == float(a), la, lam)
        alpha = jnp.where(rk1 < float(PEER_TOPK), jnp.exp(s_ref[0] - vals_ref[0, 0:1, :]), 0.0)
        rk2 = rk_ref[1]
        beta = jnp.where(rk2 < float(PEER_TOPK), jnp.exp(s_ref[1] - vals_ref[1, 0:1, :]), 0.0) / z
        r2_ref[h] = rk2.astype(BF16)
        beta_ref[h] = beta.astype(BF16)
        alpha_ref[h] = alpha
        lam_ref[h] = lam
        return carry

    lax.fori_loop(0, PEER_HEADS, head_body, 0)


def peer_route(xT, wqT, keys2):
    d, n = xT.shape
    tl = LANES
    assert n % tl == 0
    nq = PEER_HEADS * PEER_DK
    out_bf = jax.ShapeDtypeStruct((PEER_HEADS, N_KEYS, n), BF16)
    out_f = jax.ShapeDtypeStruct((PEER_HEADS, N_KEYS, n), F32)
    tab_spec = pl.BlockSpec((PEER_HEADS, N_KEYS, tl), lambda i: (0, 0, i))
    return pl.pallas_call(
        _peer_route_kernel,
        out_shape=(out_bf, out_bf, out_f, out_f),
        grid=(n // tl,),
        in_specs=[pl.BlockSpec((d, tl), lambda i: (0, i)),
                  pl.BlockSpec((nq, d), lambda i: (0, 0)),
                  pl.BlockSpec((2 * PEER_HEADS, N_KEYS, PEER_DK // 2), lambda i: (0, 0, 0))],
        out_specs=(tab_spec, tab_spec, tab_spec, tab_spec),
        scratch_shapes=[pltpu.VMEM((nq, tl), BF16),
                        pltpu.VMEM((2, N_KEYS, tl), F32),
                        pltpu.VMEM((2, N_KEYS, tl), F32),
                        pltpu.VMEM((2, PEER_TOPK, tl), F32),
                        pltpu.VMEM((_PEER_NCAND_PAD, tl), F32),
                        pltpu.VMEM((_PEER_NCAND_PAD, tl), F32)],
        compiler_params=pltpu.CompilerParams(dimension_semantics=("arbitrary",), vmem_limit_bytes=VMEM_LIMIT),
        name="peer_route",
    )(xT, wqT, keys2)


def _gelu_tanh(x):
    return 0.5 * x * (1.0 + jnp.tanh(math.sqrt(2.0 / math.pi) * (x + 0.044715 * (x * x * x))))


def _peer_dense_kernel(xT_ref, u_ref, vT_ref, r2_ref, beta_ref, alpha_ref, lam_ref, yT_ref, a_ref, h_ref):
    j = pl.program_id(1)
    te = u_ref.shape[0]
    tl = xT_ref.shape[1]

    @pl.when(j == 0)
    def _():
        yT_ref[...] = jnp.zeros_like(yT_ref)

    n_sub = te // N_KEYS

    def pre_activation(r):
        rows = slice(r * N_KEYS, (r + 1) * N_KEYS)
        a_ref[rows, :] = jnp.dot(u_ref[rows, :], xT_ref[...], preferred_element_type=F32)

    pre_activation(0)
    for r in range(n_sub):
        rows = slice(r * N_KEYS, (r + 1) * N_KEYS)
        if r + 1 < n_sub:
            pre_activation(r + 1)
        g = jnp.zeros((N_KEYS, tl), BF16)
        for h in range(PEER_HEADS):
            lam = lam_ref[h, r:r + 1, :].astype(BF16)
            alp = alpha_ref[h, r:r + 1, :].astype(BF16)
            g = g + jnp.where(r2_ref[h] <= lam, beta_ref[h], jnp.zeros((), BF16)) * alp
        h_ref[rows, :] = _gelu_tanh(a_ref[rows, :].astype(BF16)) * g
        if r % PEER_SUB_PER_DOT == PEER_SUB_PER_DOT - 1:
            cols = slice((r + 1 - PEER_SUB_PER_DOT) * N_KEYS, (r + 1) * N_KEYS)
            yT_ref[...] += jnp.dot(vT_ref[:, cols], h_ref[cols, :], preferred_element_type=F32)


def peer_dense(xT, u_bf, vT_bf, r2, beta, alpha, lam, tl, te):
    d, n = xT.shape
    e = u_bf.shape[0]
    assert n % tl == 0 and e % te == 0 and te % (8 * N_KEYS) == 0
    tab_spec = pl.BlockSpec((PEER_HEADS, N_KEYS, tl), lambda i, j: (0, 0, i))
    row_spec = pl.BlockSpec((PEER_HEADS, te // N_KEYS, tl), lambda i, j: (0, j, i))
    return pl.pallas_call(
        _peer_dense_kernel,
        out_shape=jax.ShapeDtypeStruct((d, n), F32),
        grid=(n // tl, e // te),
        in_specs=[pl.BlockSpec((d, tl), lambda i, j: (0, i)),
                  pl.BlockSpec((te, d), lambda i, j: (j, 0)),
                  pl.BlockSpec((d, te), lambda i, j: (0, j)),
                  tab_spec, tab_spec, row_spec, row_spec],
        out_specs=pl.BlockSpec((d, tl), lambda i, j: (0, i)),
        scratch_shapes=[pltpu.VMEM((te, tl), F32), pltpu.VMEM((te, tl), BF16)],
        compiler_params=pltpu.CompilerParams(dimension_semantics=("arbitrary", "arbitrary"), vmem_limit_bytes=VMEM_LIMIT),
        name="peer_dense",
    )(xT, u_bf, vT_bf, r2, beta, alpha, lam)


def peer_ffn(h, peer_w):
    wqT, keys2, u_bf, vT_bf = peer_w
    b, t, d = h.shape
    n = b * t
    tl = PEER_TL if n % PEER_TL == 0 else LANES
    npad = -(-n // tl) * tl
    xT = jnp.pad(h.reshape(n, d).astype(BF16).T, ((0, 0), (0, npad - n)))
    r2, beta, alpha, lam = peer_route(xT, wqT, keys2)
    yT = peer_dense(xT, u_bf, vT_bf, r2, beta, alpha, lam, tl, PEER_TE)
    return yT.T[:n].reshape(b, t, d)


def token_mixers(parts, pos, rel_table, w_cmp, conv_w, conv_b, ret_g, past, nsa_tables):
    nq, nkv, ngate, rq, rk, rv, rg, cb, cc, ch = parts
    b, t = nq.shape[:2]
    q = nq.reshape(b, t, NSA_KV_HEADS, NSA_QPG, HEAD_DIM)
    kv = nkv.reshape(b, t, 6, NSA_KV_HEADS, HEAD_DIM)
    if past is None:
        kc = compress(kv[:, :, 0], w_cmp[0])
        vc = compress(kv[:, :, 1], w_cmp[1])
        o_nsa = nsa_prompt(nq, kv, ngate, kc, vc, nsa_tables)
        win_rows = kv[:, t - min(WINDOW, t):, 4:]
        s0 = jnp.zeros((b, RET_HEADS, RET_DK, RET_DV), jnp.float32)
        zbuf = jnp.zeros((b, CONV_W - 1, CONV_CH), ch.dtype)
        chunk = RET_CHUNK
    else:
        decode_nsa, win_buf, s0, zbuf = past
        o_nsa = decode_nsa(nq, nkv, ngate)
        wrows = jnp.concatenate([win_buf.astype(kv.dtype), kv[:, :, 4:]], axis=1)
        win_rows = wrows[:, wrows.shape[1] - min(WINDOW, wrows.shape[1]):]
        chunk = t
    rqh = rotary(rq.reshape(b, t, RET_HEADS, RET_DK), pos)
    rkh = rotary(rk.reshape(b, t, RET_HEADS, RET_DK), pos) * (RET_DK ** -0.5)
    rvh = rv.reshape(b, t, RET_HEADS, RET_DV)
    o_r, s_new = retention(rqh, rkh, rvh, s0, chunk)
    mu = jnp.mean(o_r, -1, keepdims=True)
    var = jnp.mean(jnp.square(o_r - mu), -1, keepdims=True)
    on = (o_r - mu) * lax.rsqrt(var + LN_EPS) * ret_g.reshape(RET_HEADS, RET_DV)
    o_ret = (on.reshape(b, t, RET_WIDTH) * jax.nn.silu(rg.astype(jnp.float32))).astype(nq.dtype)
    z = cc * ch
    zp = jnp.concatenate([zbuf.astype(z.dtype), z], axis=1)
    y = conv_b + sum(zp[:, j:j + t] * conv_w[j] for j in range(CONV_W))
    o_conv = (cb * y).astype(nq.dtype)
    mix = jnp.concatenate([o_nsa, o_ret, o_conv], axis=-1)
    return mix, (kv[:, :, :4], win_rows, s_new, zp[:, t:])


def kernel(x_prompt, x_sample, cache_nsa_kv, cache_win_kv, state_ret, state_conv, page_table, c_prompt, c_sample, rel_bias, w_ada, b_ada, w_in, w_cmp, conv_w, conv_b, ret_norm_g, w_out, ln1_g, ln1_b, ln2_g, ln2_b, peer_wq, peer_keys, peer_u, peer_v):
    alpha = (2.0 * DEPTH) ** 0.25
    n_pages = page_table.shape[1]
    past_len = n_pages * PAGE_SIZE
    sp = x_prompt.shape[1]
    bs, ts = x_sample.shape[:2]
    pos_p = jnp.arange(sp, dtype=jnp.int32)
    pos_s = past_len + jnp.arange(ts, dtype=jnp.int32)
    nsa_tables = nsa_bias_tables(rel_bias, sp, NSA_TQ)
    dec_tables = decode_tables(rel_bias, past_len)
    cache_nsa = cache_nsa_kv.transpose(0, 1, 3, 4, 5, 2).reshape(cache_nsa_kv.shape[0], cache_nsa_kv.shape[1], ROW_WIDTH, PAGE_SIZE)
    cache_win = cache_win_kv.transpose(0, 1, 3, 4, 5, 2).reshape(cache_win_kv.shape[0], bs, HALF_WIDTH, cache_win_kv.shape[2])

    def run_layer(x, c, l, pos, past, peer_w):
        m = (jax.nn.silu(c) @ w_ada[l] + b_ada[l]).reshape(c.shape[0], 6, 1, D_MODEL)
        h = x * (1.0 + m[:, 1]) + m[:, 0]
        parts = split_proj(h @ w_in[l])
        mix, st = token_mixers(parts, pos, rel_bias, w_cmp[l], conv_w[l], conv_b[l], ret_norm_g[l], past, nsa_tables)
        x = layer_norm(alpha * x + (1.0 + m[:, 2]) * (mix @ w_out[l]), ln1_g[l], ln1_b[l])
        h = x * (1.0 + m[:, 4]) + m[:, 3]
        y = peer_ffn(h, peer_w)
        x = layer_norm(alpha * x + (1.0 + m[:, 5]) * y, ln2_g[l], ln2_b[l])
        return x, st

    xp, xs = x_prompt, x_sample
    nkv_p, nkv_s, win_p, win_s, ret_p, ret_s, conv_p, conv_s = [], [], [], [], [], [], [], []
    for l in range(DEPTH):
        peer_w = (peer_wq[l].T.astype(BF16),
                  peer_keys[l].reshape(2 * PEER_HEADS, N_KEYS, PEER_DK // 2).astype(BF16),
                  peer_u[l].astype(BF16),
                  peer_v[l].T.astype(BF16))
        xp, st_p = run_layer(xp, c_prompt, l, pos_p, None, peer_w)
        decode_nsa = functools.partial(nsa_decode, page_table=page_table, cache_nsa=cache_nsa, cache_win=cache_win,
                                       w_bd=compress_block_weights(w_cmp[l]), tables=dec_tables, layer=l)
        xs, st_s = run_layer(xs, c_sample, l, pos_s, (decode_nsa, cache_win_kv[l], state_ret[l], state_conv[l]), peer_w)
        nkv_p.append(st_p[0])
        win_p.append(st_p[1])
        ret_p.append(st_p[2])
        conv_p.append(st_p[3])
        nkv_s.append(st_s[0])
        win_s.append(st_s[1])
        ret_s.append(st_s[2])
        conv_s.append(st_s[3])
    return (xp, xs, jnp.stack(nkv_p), jnp.stack(nkv_s), jnp.stack(win_p), jnp.stack(win_s), jnp.stack(ret_p), jnp.stack(ret_s), jnp.stack(conv_p), jnp.stack(conv_s))
```

```python
import math
import functools
import jax
import jax.numpy as jnp
from jax import lax
import numpy as np
from jax.experimental import pallas as pl
from jax.experimental.pallas import tpu as pltpu

D_MODEL = 1024
BATCH = 4
SEQ = 4096
DEPTH = 2
DEC_BATCH = 32
DEC_SEQ = 1
PAST_LEN = 8192
PAGE_SIZE = 128

HEAD_DIM = 64
NSA_WIDTH = D_MODEL // 2
NSA_HEADS = NSA_WIDTH // HEAD_DIM
NSA_KV_HEADS = 2
NSA_QPG = NSA_HEADS // NSA_KV_HEADS
KV_WIDTH = NSA_KV_HEADS * HEAD_DIM
SCALE = HEAD_DIM ** -0.5
L_CMP = 32
L_SEL = 64
N_SEL = 16
WINDOW = 512
Q_BLOCK = 128
SEL_Q_BLOCK = 64
N_BUCKETS = 32
REL_MAX_DIST = 128
RET_WIDTH = D_MODEL // 4
RET_DK = 64
RET_DV = 64
RET_HEADS = RET_WIDTH // RET_DV
RET_CHUNK = 128
ROPE_BASE = 10000.0
CONV_CH = D_MODEL // 4
CONV_W = 3
MIX_WIDTH = NSA_WIDTH + RET_WIDTH + CONV_CH
PEER_HEADS = 8
PEER_DK = 256
N_KEYS = 128
N_EXPERTS = N_KEYS * N_KEYS
PEER_TOPK = 16
PEER_CHUNK = 256
LN_EPS = 1e-5
SPLIT_SIZES = (NSA_WIDTH, 6 * KV_WIDTH, 3 * NSA_HEADS, RET_HEADS * RET_DK, RET_HEADS * RET_DK, RET_WIDTH, RET_WIDTH, CONV_CH, CONV_CH, CONV_CH)
N_IN = sum(SPLIT_SIZES)

F32 = jnp.float32
BF16 = jnp.bfloat16
NEG_INF = float('-inf')
MASK_NEG = -1e9
CMP_MASK_NEG = -1e30
NSA_TQ = 256
NSA_RB = 128
ROW_WIDTH = 4 * KV_WIDTH
HALF_WIDTH = 2 * KV_WIDTH
PEER_TL = 512
PEER_TE = 1024
PEER_SUB_PER_DOT = 2
LANES = 128
VMEM_LIMIT = 56 * 1024 * 1024


def _ln_kernel(x_ref, g_ref, b_ref, o_ref):
    x = x_ref[...]
    mu = jnp.mean(x, -1, keepdims=True)
    xc = x - mu
    var = jnp.mean(xc * xc, -1, keepdims=True)
    o_ref[...] = xc * lax.rsqrt(var + LN_EPS) * g_ref[...] + b_ref[...]


def layer_norm(x, g, b):
    shp = x.shape
    x2 = x.reshape(-1, shp[-1])
    n = x2.shape[0]
    tm = min(n, 512)
    out = pl.pallas_call(
        _ln_kernel,
        out_shape=jax.ShapeDtypeStruct(x2.shape, jnp.float32),
        grid=(n // tm,),
        in_specs=[pl.BlockSpec((tm, shp[-1]), lambda i: (i, 0)),
                  pl.BlockSpec((1, shp[-1]), lambda i: (0, 0)),
                  pl.BlockSpec((1, shp[-1]), lambda i: (0, 0))],
        out_specs=pl.BlockSpec((tm, shp[-1]), lambda i: (i, 0)),
        name="layer_norm",
    )(x2, g.reshape(1, -1), b.reshape(1, -1))
    return out.reshape(shp)


def t5_bucket(dist):
    n = jnp.maximum(dist, 0)
    exact = N_BUCKETS // 2
    nf = jnp.maximum(n, exact).astype(jnp.float32)
    big = exact + (jnp.log(nf / exact) / math.log(REL_MAX_DIST / exact) * (N_BUCKETS - exact)).astype(jnp.int32)
    return jnp.where(n < exact, n, jnp.minimum(big, N_BUCKETS - 1))


def rel_bias_lookup(rel_bias, dist):
    onehot = jax.nn.one_hot(t5_bucket(dist), N_BUCKETS, dtype=jnp.float32)
    return jnp.einsum('...k,kh->...h', onehot, rel_bias, precision=lax.Precision.HIGHEST)


def rotary(x, pos):
    half = x.shape[-1] // 2
    inv = ROPE_BASE ** (-jnp.arange(half, dtype=jnp.float32) / half)
    ang = pos.astype(jnp.float32)[:, None] * inv[None, :]
    cos = jnp.cos(ang)[None, :, None, :]
    sin = jnp.sin(ang)[None, :, None, :]
    x1, x2 = x[..., :half], x[..., half:]
    return jnp.concatenate([x1 * cos - x2 * sin, x1 * sin + x2 * cos], axis=-1)


def split_proj(p):
    outs, start = [], 0
    for size in SPLIT_SIZES:
        outs.append(p[..., start:start + size])
        start += size
    return outs


def compress(rows, w):
    b, lp, g, d = rows.shape
    blk = rows.reshape(b, lp // L_CMP, L_CMP, g, d).transpose(0, 1, 3, 2, 4).reshape(b, lp // L_CMP, g, L_CMP * d)
    return blk @ w


def retention(q, k, v, s0, chunk):
    b, t = q.shape[:2]
    nc = t // chunk
    lg = jnp.log(1.0 - 2.0 ** (-5.0 - jnp.arange(RET_HEADS, dtype=jnp.float32)))
    i = jnp.arange(chunk, dtype=jnp.float32)
    diff = i[:, None] - i[None, :]
    dmat = jnp.where(diff >= 0, jnp.exp(jnp.maximum(diff, 0.0)[None] * lg[:, None, None]), 0.0)
    q_dec = jnp.exp((i + 1.0)[:, None] * lg[None, :])[None, :, :, None]
    k_dec = jnp.exp((chunk - 1.0 - i)[:, None] * lg[None, :])[None, :, :, None]
    s_dec = jnp.exp(chunk * lg)[None, :, None, None]

    def to_chunks(a):
        return jnp.moveaxis(a.astype(jnp.float32).reshape(b, nc, chunk, RET_HEADS, a.shape[-1]), 1, 0)

    def step(s, xs):
        qc, kc, vc = xs
        att = jnp.einsum('bihd,bjhd->bhij', qc, kc) * dmat
        o = jnp.einsum('bhij,bjhe->bihe', att, vc) + jnp.einsum('bihd,bhde->bihe', qc * q_dec, s)
        s = s * s_dec + jnp.einsum('bjhd,bjhe->bhde', kc * k_dec, vc)
        return s, o

    s, o = lax.scan(step, s0.astype(jnp.float32), (to_chunks(q), to_chunks(k), to_chunks(v)))
    return jnp.moveaxis(o, 0, 1).reshape(b, t, RET_HEADS, RET_DV), s


def _nsa_prompt_kernel(q_ref, gate_ref, kcT_ref, vc_ref, cb_ref, ksT_ref, vs_ref, kwT_ref, vw_ref, sb_ref, wb_ref,
                       o_ref, qa_ref, m_ref, acc_ref, oc_ref, osel_ref, owin_ref):
    qi = pl.program_id(2)
    tq = q_ref.shape[3]
    ncb = kcT_ref.shape[3]
    ka = ksT_ref.shape[2]
    t0 = qi * tq
    rows = NSA_QPG * tq

    imp = jnp.zeros((tq, ncb), F32)
    for h in range(NSA_QPG):
        bias = cb_ref[0, h]
        lg = jnp.dot(q_ref[0, 0, h], kcT_ref[0, 0], preferred_element_type=F32) + bias
        mx = jnp.max(lg, axis=-1, keepdims=True)
        e = jnp.where(bias > 0.5 * CMP_MASK_NEG, jnp.exp(lg - mx), 0.0)
        p = e / jnp.maximum(jnp.sum(e, axis=-1, keepdims=True), 1e-30)
        imp = imp + p
        oc_ref[h] = jnp.dot(p.astype(BF16), vc_ref[0, 0], preferred_element_type=F32)

    lane = lax.broadcasted_iota(jnp.int32, (tq, ncb), 1)
    tpos = t0 + lax.broadcasted_iota(jnp.int32, (tq, ncb), 0)
    pair = imp + pltpu.roll(imp, ncb - 1, 1)
    blk = lane >> 1
    cur = tpos >> 6
    forced = (blk == 0) | (blk == cur) | (blk == cur - 1)
    cand = ((lane & 1) == 0) & (blk <= cur)
    score = jnp.where(cand, jnp.where(forced, jnp.inf, pair), NEG_INF)
    lane_f = lane.astype(F32)
    chosen = jnp.zeros((tq, ncb), F32)
    for _ in range(N_SEL):
        mx = jnp.max(score, axis=-1, keepdims=True)
        first = jnp.min(jnp.where(score == mx, lane_f, float(ncb)), axis=-1, keepdims=True)
        hit = (lane_f == first) & (mx > NEG_INF)
        chosen = jnp.where(hit, 1.0, chosen)
        score = jnp.where(hit, NEG_INF, score)
    blockmask = jnp.where(chosen > 0.0, 0.0, MASK_NEG).astype(BF16)

    for h in range(NSA_QPG):
        qa_ref[h * tq:(h + 1) * tq, 0:ncb] = blockmask
        qa_ref[h * tq:(h + 1) * tq, ncb:ncb + HEAD_DIM] = q_ref[0, 0, h]
        if ka > ncb + HEAD_DIM:
            qa_ref[h * tq:(h + 1) * tq, ncb + HEAD_DIM:ka] = jnp.zeros((tq, ka - ncb - HEAD_DIM), BF16)

    nrb = rows // NSA_RB
    per_head = tq // NSA_RB

    def reset():
        m_ref[...] = jnp.full((rows, LANES), CMP_MASK_NEG, F32)
        acc_ref[...] = jnp.zeros((rows, LANES), F32)

    def attn_step(rb, qrb, k_t, v, bias):
        r0 = rb * NSA_RB
        s = jnp.dot(qrb, k_t, preferred_element_type=F32)
        if bias is not None:
            s = s + bias
        parts = [s[:, i * LANES:(i + 1) * LANES] for i in range(s.shape[1] // LANES)]
        red = parts[0]
        for part in parts[1:]:
            red = jnp.maximum(red, part)
        m_old = m_ref[r0:r0 + NSA_RB, :]
        m_new = jnp.maximum(m_old, jnp.max(red, axis=-1, keepdims=True))
        p = jnp.concatenate([jnp.exp(part - m_new) for part in parts], axis=1).astype(BF16)
        acc_ref[r0:r0 + NSA_RB, :] = (jnp.exp(m_old - m_new) * acc_ref[r0:r0 + NSA_RB, :]
                                      + jnp.dot(p, v, preferred_element_type=F32))
        m_ref[r0:r0 + NSA_RB, :] = m_new

    def finish(dst_ref):
        acc = acc_ref[...]
        dst_ref[...] = acc[:, 0:HEAD_DIM] / acc[:, HEAD_DIM:HEAD_DIM + 1]

    reset()

    def far_body(c, carry):
        col = pl.multiple_of((c + 1) * tq, tq)
        k_t = ksT_ref[0, 0, :, pl.ds(col, tq)]
        v = vs_ref[0, 0, pl.ds(col, tq), :]
        for rb in range(nrb):
            attn_step(rb, qa_ref[rb * NSA_RB:(rb + 1) * NSA_RB, :], k_t, v, None)
        return carry

    lax.fori_loop(0, jnp.maximum(qi - 1, 0), far_body, 0)

    col0 = pl.multiple_of(t0, tq)
    for c in range(2):
        colc = pl.multiple_of(col0 + c * tq, tq)
        k_t = ksT_ref[0, 0, :, pl.ds(colc, tq)]
        v = vs_ref[0, 0, pl.ds(colc, tq), :]
        jn = lax.broadcasted_iota(jnp.int32, (1, tq), 1) + c * tq
        colmask = jnp.where(jn + (t0 - tq) >= 0, 0.0, MASK_NEG)
        for rb in range(nrb):
            h, part = rb // per_head, rb % per_head
            bias = sb_ref[0, h, part * NSA_RB:(part + 1) * NSA_RB, c * tq:(c + 1) * tq] + colmask
            attn_step(rb, qa_ref[rb * NSA_RB:(rb + 1) * NSA_RB, :], k_t, v, bias)
    finish(osel_ref)

    reset()
    for c in range((WINDOW + tq) // tq):
        colc = pl.multiple_of(col0 + c * tq, tq)
        k_t = kwT_ref[0, 0, :, pl.ds(colc, tq)]
        v = vw_ref[0, 0, pl.ds(colc, tq), :]
        jw = lax.broadcasted_iota(jnp.int32, (1, tq), 1) + c * tq
        wmask = jnp.where(jw + (t0 - WINDOW) >= 0, 0.0, MASK_NEG)
        for rb in range(nrb):
            h, part = rb // per_head, rb % per_head
            bias = wb_ref[0, h, part * NSA_RB:(part + 1) * NSA_RB, c * tq:(c + 1) * tq] + wmask
            attn_step(rb, qa_ref[rb * NSA_RB:(rb + 1) * NSA_RB, ncb:ncb + HEAD_DIM], k_t, v, bias)
    finish(owin_ref)

    g = jax.nn.sigmoid(gate_ref[0, 0])
    for h in range(NSA_QPG):
        o_h = (g[:, 3 * h:3 * h + 1] * oc_ref[h]
               + g[:, 3 * h + 1:3 * h + 2] * osel_ref[h * tq:(h + 1) * tq, :]
               + g[:, 3 * h + 2:3 * h + 3] * owin_ref[h * tq:(h + 1) * tq, :])
        o_ref[0, :, h * HEAD_DIM:(h + 1) * HEAD_DIM] = o_h


def nsa_bias_tables(rel_bias, t, tq):
    ncb = t // L_CMP

    def heads_first(tab):
        return tab.transpose(2, 0, 1).reshape(NSA_KV_HEADS, NSA_QPG, tab.shape[0], tab.shape[1])

    pos = jnp.arange(t, dtype=jnp.int32)
    end = jnp.arange(ncb, dtype=jnp.int32) * L_CMP + (L_CMP - 1)
    dist = pos[:, None] - end[None, :]
    cb = jnp.where((dist >= 0)[..., None], rel_bias_lookup(rel_bias, dist), CMP_MASK_NEG)
    i = jnp.arange(tq, dtype=jnp.int32)
    dist = i[:, None] + tq - jnp.arange(2 * tq, dtype=jnp.int32)[None, :]
    sb = jnp.where((dist >= 0)[..., None], rel_bias_lookup(rel_bias, dist) - rel_bias[N_BUCKETS - 1], MASK_NEG)
    dist = i[:, None] + WINDOW - jnp.arange(WINDOW + tq, dtype=jnp.int32)[None, :]
    wb = jnp.where(((dist >= 0) & (dist < WINDOW))[..., None], rel_bias_lookup(rel_bias, dist), MASK_NEG)
    return heads_first(cb), heads_first(sb), heads_first(wb)


def nsa_prompt(q, kv, gates, kc, vc, tables):
    b, t = q.shape[:2]
    tq = NSA_TQ
    ncb = t // L_CMP
    ka = -(-(ncb + HEAD_DIM) // LANES) * LANES
    cb, sb, wb = tables
    q4 = (q * SCALE).astype(BF16).reshape(b, t, NSA_KV_HEADS, NSA_QPG, HEAD_DIM).transpose(0, 2, 3, 1, 4)
    g4 = gates.reshape(b, t, NSA_KV_HEADS, NSA_QPG * 3).transpose(0, 2, 1, 3)
    kcT = kc.astype(BF16).transpose(0, 2, 3, 1)
    vcg = vc.astype(BF16).transpose(0, 2, 1, 3)
    kvb = kv.astype(BF16)
    onehot = (2 * (jnp.arange(t, dtype=jnp.int32) // L_SEL)[None, :] == jnp.arange(ncb, dtype=jnp.int32)[:, None]).astype(BF16)
    ks_t = kvb[:, :, 2].transpose(0, 2, 3, 1)
    ksT = jnp.concatenate([jnp.broadcast_to(onehot, (b, NSA_KV_HEADS, ncb, t)), ks_t,
                           jnp.zeros((b, NSA_KV_HEADS, ka - ncb - HEAD_DIM, t), BF16)], axis=2)
    ksT = jnp.pad(ksT, ((0, 0), (0, 0), (0, 0), (tq, 0)))
    def with_ones(v):
        one = jnp.ones(v.shape[:-1] + (1,), BF16)
        return jnp.concatenate([v, one, jnp.zeros(v.shape[:-1] + (LANES - HEAD_DIM - 1,), BF16)], axis=-1)

    vs = jnp.pad(with_ones(kvb[:, :, 3].transpose(0, 2, 1, 3)), ((0, 0), (0, 0), (tq, 0), (0, 0)))
    kwT = jnp.pad(kvb[:, :, 4].transpose(0, 2, 3, 1), ((0, 0), (0, 0), (0, 0), (WINDOW, 0)))
    vw = jnp.pad(with_ones(kvb[:, :, 5].transpose(0, 2, 1, 3)), ((0, 0), (0, 0), (WINDOW, 0), (0, 0)))
    rows = NSA_QPG * tq

    def per_bg(shape):
        return pl.BlockSpec((1, 1) + shape, lambda bi, gi, qi: (bi, gi, 0, 0))

    return pl.pallas_call(
        _nsa_prompt_kernel,
        out_shape=jax.ShapeDtypeStruct((b, t, NSA_HEADS * HEAD_DIM), F32),
        grid=(b, NSA_KV_HEADS, t // tq),
        in_specs=[pl.BlockSpec((1, 1, NSA_QPG, tq, HEAD_DIM), lambda bi, gi, qi: (bi, gi, 0, qi, 0)),
                  pl.BlockSpec((1, 1, tq, NSA_QPG * 3), lambda bi, gi, qi: (bi, gi, qi, 0)),
                  per_bg((HEAD_DIM, ncb)), per_bg((ncb, HEAD_DIM)),
                  pl.BlockSpec((1, NSA_QPG, tq, ncb), lambda bi, gi, qi: (gi, 0, qi, 0)),
                  per_bg((ka, tq + t)), per_bg((tq + t, LANES)),
                  per_bg((HEAD_DIM, WINDOW + t)), per_bg((WINDOW + t, LANES)),
                  pl.BlockSpec((1, NSA_QPG, tq, 2 * tq), lambda bi, gi, qi: (gi, 0, 0, 0)),
                  pl.BlockSpec((1, NSA_QPG, tq, WINDOW + tq), lambda bi, gi, qi: (gi, 0, 0, 0))],
        out_specs=pl.BlockSpec((1, tq, NSA_QPG * HEAD_DIM), lambda bi, gi, qi: (bi, qi, gi)),
        scratch_shapes=[pltpu.VMEM((rows, ka), BF16),
                        pltpu.VMEM((rows, LANES), F32), pltpu.VMEM((rows, LANES), F32),
                        pltpu.VMEM((NSA_QPG, tq, HEAD_DIM), F32),
                        pltpu.VMEM((rows, HEAD_DIM), F32), pltpu.VMEM((rows, HEAD_DIM), F32)],
        compiler_params=pltpu.CompilerParams(dimension_semantics=("arbitrary", "arbitrary", "arbitrary"),
                                             vmem_limit_bytes=VMEM_LIMIT),
        name="nsa_prompt",
    )(q4, g4, kcT, vcg, cb, ksT, vs, kwT, vw, sb, wb)


def _gather_pages(pt_ref, cache_ref, sem, layer, feat0, dst_of):
    s = pl.program_id(0)
    n_pages = pt_ref.shape[1]
    slot = lax.rem(s, 2)

    def copy(page, sl, p):
        return pltpu.make_async_copy(cache_ref.at[layer, page, pl.ds(feat0, HALF_WIDTH), :], dst_of(sl, p), sem.at[sl])

    def start(seq, sl):
        for p in range(n_pages):
            copy(pt_ref[seq, p], sl, p).start()

    @pl.when(s == 0)
    def _():
        start(0, 0)

    @pl.when(s + 1 < pl.num_programs(0))
    def _():
        start(s + 1, 1 - slot)

    for p in range(n_pages):
        copy(0, slot, p).wait()
    return slot


def _decode_compress_kernel(pt_ref, cache_ref, w_ref, out_ref, buf, rows_ref, sem, *, layer):
    slot = _gather_pages(pt_ref, cache_ref, sem, layer, 0, lambda sl, p: buf.at[sl, p])
    ncb = out_ref.shape[1]
    n_pages = pt_ref.shape[1]
    eye = (lax.broadcasted_iota(jnp.int32, (PAGE_SIZE, PAGE_SIZE), 0)
           == lax.broadcasted_iota(jnp.int32, (PAGE_SIZE, PAGE_SIZE), 1)).astype(F32).astype(BF16)
    for p in range(n_pages):
        x = _dot_nt(eye, buf[slot, p].astype(BF16))
        for part in range(2):
            rows_ref[part, p * PAGE_SIZE:(p + 1) * PAGE_SIZE, :] = x[:, part * KV_WIDTH:(part + 1) * KV_WIDTH]
    for part in range(2):
        acc = jnp.zeros((ncb, KV_WIDTH), F32)
        for r in range(L_CMP):
            x = rows_ref[part, pl.ds(r, ncb, stride=L_CMP), :]
            acc = acc + jnp.dot(x.astype(BF16), w_ref[r, part], preferred_element_type=F32)
        out_ref[0, :, part * KV_WIDTH:(part + 1) * KV_WIDTH] = acc


def decode_compress(page_table, cache, w_bd, layer):
    bs, n_pages = page_table.shape
    past = n_pages * PAGE_SIZE
    ncb = past // L_CMP
    return pl.pallas_call(
        functools.partial(_decode_compress_kernel, layer=layer),
        out_shape=jax.ShapeDtypeStruct((bs, ncb, HALF_WIDTH), F32),
        grid_spec=pltpu.PrefetchScalarGridSpec(
            num_scalar_prefetch=1,
            grid=(bs,),
            in_specs=[pl.BlockSpec(memory_space=pl.ANY),
                      pl.BlockSpec((L_CMP, 2, KV_WIDTH, KV_WIDTH), lambda s, pt: (0, 0, 0, 0))],
            out_specs=pl.BlockSpec((1, ncb, HALF_WIDTH), lambda s, pt: (s, 0, 0)),
            scratch_shapes=[pltpu.VMEM((2, n_pages, HALF_WIDTH, PAGE_SIZE), F32),
                            pltpu.VMEM((2, past, KV_WIDTH), F32), pltpu.SemaphoreType.DMA((2,))]),
        compiler_params=pltpu.CompilerParams(dimension_semantics=("arbitrary",), vmem_limit_bytes=VMEM_LIMIT),
        name="decode_compress",
    )(page_table, cache, w_bd)


def _dot_nt(a, b):
    return lax.dot_general(a, b, (((1,), (1,)), ((), ())), preferred_element_type=F32)


def _decode_attend_kernel(pt_ref, cache_ref, q_ref, gate_ref, kcvc_ref, new_ref, win_ref, cb_ref, sb_ref, wb_ref,
                          rel0_ref, onehot_ref, hmask_ref, o_ref, buf, sem, *, layer):
    slot = _gather_pages(pt_ref, cache_ref, sem, layer, HALF_WIDTH,
                         lambda sl, p: buf.at[sl, :, pl.ds(p * PAGE_SIZE, PAGE_SIZE)])
    q = q_ref[0]
    qf = q.astype(F32)
    ncb = kcvc_ref.shape[1]
    rel0 = rel0_ref[:, 0:1]

    def bf_round(x):
        return x.astype(BF16).astype(F32)

    kc = kcvc_ref[0, :, 0:KV_WIDTH].astype(BF16)
    vc = kcvc_ref[0, :, KV_WIDTH:HALF_WIDTH].astype(BF16)
    lg = _dot_nt(q, kc) + cb_ref[...]
    e = jnp.exp(lg - jnp.max(lg, axis=-1, keepdims=True))
    p = e / jnp.sum(e, axis=-1, keepdims=True)
    o_cmp = jnp.dot(p.astype(BF16), vc, preferred_element_type=F32)

    row = lax.broadcasted_iota(jnp.int32, (NSA_HEADS, ncb), 0)
    lane = lax.broadcasted_iota(jnp.int32, (NSA_HEADS, ncb), 1)
    pg0 = jnp.sum(p[0:NSA_QPG], axis=0, keepdims=True)
    pg1 = jnp.sum(p[NSA_QPG:NSA_HEADS], axis=0, keepdims=True)
    imp = jnp.where(row < NSA_QPG, pg0, pg1)
    pair = imp + pltpu.roll(imp, ncb - 1, 1)
    blk = lane >> 1
    forced = (blk == 0) | (blk == ncb // 2 - 1)
    score = jnp.where((lane & 1) == 0, jnp.where(forced, jnp.inf, pair), NEG_INF)
    lane_f = lane.astype(F32)
    chosen = jnp.zeros((NSA_HEADS, ncb), F32)
    for _ in range(N_SEL - 1):
        mx = jnp.max(score, axis=-1, keepdims=True)
        first = jnp.min(jnp.where(score == mx, lane_f, float(ncb)), axis=-1, keepdims=True)
        hit = lane_f == first
        chosen = jnp.where(hit, 1.0, chosen)
        score = jnp.where(hit, NEG_INF, score)
    blockmask = jnp.where(chosen > 0.0, 0.0, MASK_NEG).astype(BF16)

    ks_t = buf[slot, 0:KV_WIDTH, :].astype(BF16)
    vs_t = buf[slot, KV_WIDTH:HALF_WIDTH, :].astype(BF16)
    s = (jnp.dot(q, ks_t, preferred_element_type=F32)
         + jnp.dot(blockmask, onehot_ref[...], preferred_element_type=F32) + sb_ref[...])
    s_new = jnp.sum(qf * bf_round(new_ref[0, 0:1, :]), axis=-1, keepdims=True) + rel0
    m = jnp.maximum(jnp.max(s, axis=-1, keepdims=True), s_new)
    e = jnp.exp(s - m)
    e_new = jnp.exp(s_new - m)
    den = jnp.sum(e, axis=-1, keepdims=True) + e_new
    o_sel = (_dot_nt(e.astype(BF16), vs_t) + e_new * bf_round(new_ref[0, 1:2, :])) / den

    wk_t = win_ref[0, 0, 0:KV_WIDTH, :].astype(BF16)
    wv_t = win_ref[0, 0, KV_WIDTH:HALF_WIDTH, :].astype(BF16)
    sw = jnp.dot(q, wk_t, preferred_element_type=F32) + wb_ref[...]
    sw_new = jnp.sum(qf * bf_round(new_ref[0, 2:3, :]), axis=-1, keepdims=True) + rel0
    mw = jnp.maximum(jnp.max(sw, axis=-1, keepdims=True), sw_new)
    ew = jnp.exp(sw - mw)
    ew_new = jnp.exp(sw_new - mw)
    denw = jnp.sum(ew, axis=-1, keepdims=True) + ew_new
    o_win = (_dot_nt(ew.astype(BF16), wv_t) + ew_new * bf_round(new_ref[0, 3:4, :])) / denw

    g = jax.nn.sigmoid(gate_ref[0])
    o_ref[0] = (g[:, 0:1] * o_cmp + g[:, 1:2] * o_sel + g[:, 2:3] * o_win) * hmask_ref[...]


def decode_tables(rel_bias, past):
    ncb = past // L_CMP
    end = jnp.arange(ncb, dtype=jnp.int32) * L_CMP + (L_CMP - 1)
    cb = rel_bias_lookup(rel_bias, past - end).T
    sb = rel_bias_lookup(rel_bias, past - jnp.arange(past, dtype=jnp.int32)).T
    dist = WINDOW - jnp.arange(WINDOW, dtype=jnp.int32)
    wb = jnp.where((dist < WINDOW)[None, :], rel_bias_lookup(rel_bias, dist).T, MASK_NEG)
    rel0 = jnp.broadcast_to(rel_bias[0][:, None], (NSA_HEADS, LANES))
    onehot = (2 * (jnp.arange(past, dtype=jnp.int32) // L_SEL)[None, :] == jnp.arange(ncb, dtype=jnp.int32)[:, None]).astype(BF16)
    hmask = (jnp.arange(KV_WIDTH, dtype=jnp.int32)[None, :] // HEAD_DIM == jnp.arange(NSA_HEADS, dtype=jnp.int32)[:, None] // NSA_QPG).astype(F32)
    return cb, sb, wb, rel0, onehot, hmask


def compress_block_weights(w_cmp):
    w = w_cmp.reshape(2, L_CMP, HEAD_DIM, HEAD_DIM).transpose(1, 0, 2, 3)
    z = jnp.zeros_like(w)
    return jnp.concatenate([jnp.concatenate([w, z], axis=3), jnp.concatenate([z, w], axis=3)], axis=2).astype(BF16)


def nsa_decode(nq, nkv, ngate, page_table, cache_nsa, cache_win, w_bd, tables, layer):
    bs = nq.shape[0]
    n_pages = page_table.shape[1]
    past = n_pages * PAGE_SIZE
    ncb = past // L_CMP
    cb, sb, wb, rel0, onehot, hmask = tables
    kcvc = decode_compress(page_table, cache_nsa, w_bd, layer)
    qh = nq.reshape(bs, NSA_HEADS, 1, HEAD_DIM) * SCALE
    own_group = jnp.arange(NSA_KV_HEADS)[None, None, :, None] == (jnp.arange(NSA_HEADS) // NSA_QPG)[None, :, None, None]
    qblk = (qh * own_group).reshape(bs, NSA_HEADS, KV_WIDTH).astype(BF16)
    gpad = jnp.pad(ngate.reshape(bs, NSA_HEADS, 3), ((0, 0), (0, 0), (0, LANES - 3)))
    newr = jnp.pad(nkv.reshape(bs, 6, KV_WIDTH)[:, 2:6], ((0, 0), (0, 4), (0, 0)))

    def const2(shape):
        return pl.BlockSpec(shape, lambda s, pt: (0, 0))

    out = pl.pallas_call(
        functools.partial(_decode_attend_kernel, layer=layer),
        out_shape=jax.ShapeDtypeStruct((bs, NSA_HEADS, KV_WIDTH), F32),
        grid_spec=pltpu.PrefetchScalarGridSpec(
            num_scalar_prefetch=1,
            grid=(bs,),
            in_specs=[pl.BlockSpec(memory_space=pl.ANY),
                      pl.BlockSpec((1, NSA_HEADS, KV_WIDTH), lambda s, pt: (s, 0, 0)),
                      pl.BlockSpec((1, NSA_HEADS, LANES), lambda s, pt: (s, 0, 0)),
                      pl.BlockSpec((1, ncb, HALF_WIDTH), lambda s, pt: (s, 0, 0)),
                      pl.BlockSpec((1, 8, KV_WIDTH), lambda s, pt: (s, 0, 0)),
                      pl.BlockSpec((1, 1, HALF_WIDTH, WINDOW), lambda s, pt: (layer, s, 0, 0)),
                      const2((NSA_HEADS, ncb)), const2((NSA_HEADS, past)), const2((NSA_HEADS, WINDOW)),
                      const2((NSA_HEADS, LANES)), const2((ncb, past)), const2((NSA_HEADS, KV_WIDTH))],
            out_specs=pl.BlockSpec((1, NSA_HEADS, KV_WIDTH), lambda s, pt: (s, 0, 0)),
            scratch_shapes=[pltpu.VMEM((2, HALF_WIDTH, past), F32), pltpu.SemaphoreType.DMA((2,))]),
        compiler_params=pltpu.CompilerParams(dimension_semantics=("arbitrary",), vmem_limit_bytes=VMEM_LIMIT),
        name="decode_attend",
    )(page_table, cache_nsa, qblk, gpad, kcvc, newr, cache_win, cb, sb, wb, rel0, onehot, hmask)
    return (out[..., :HEAD_DIM] + out[..., HEAD_DIM:]).reshape(bs, 1, NSA_HEADS * HEAD_DIM)


_PEER_CANDS = [(a, b) for a in range(PEER_TOPK) for b in range(PEER_TOPK) if (a + 1) * (b + 1) <= PEER_TOPK]
_PEER_NCAND = len(_PEER_CANDS)
_PEER_NCAND_PAD = -(-_PEER_NCAND // 8) * 8
_PEER_GROUP_START = [min(c for c, (a, _) in enumerate(_PEER_CANDS) if a == aa) for aa in range(PEER_TOPK)]
_PEER_GROUP_LEN = [sum(1 for (a, _) in _PEER_CANDS if a == aa) for aa in range(PEER_TOPK)]


def _peer_route_kernel(xT_ref, wqT_ref, keys_ref, r2_ref, beta_ref, alpha_ref, lam_ref,
                       qT_ref, s_ref, rk_ref, vals_ref, cand_ref, sel_ref):
    tl = xT_ref.shape[1]
    qT_ref[...] = jnp.dot(wqT_ref[...], xT_ref[...], preferred_element_type=F32).astype(BF16)
    iota_k = lax.broadcasted_iota(jnp.int32, (N_KEYS, tl), 0).astype(F32)

    def rank_top16(p, stable):
        cur = s_ref[p]
        rk = jnp.full((N_KEYS, tl), float(PEER_TOPK), F32)
        for a in range(PEER_TOPK):
            m = jnp.max(cur, axis=0, keepdims=True)
            if stable:
                idx = jnp.min(jnp.where(cur == m, iota_k, float(N_KEYS)), axis=0, keepdims=True)
                hit = iota_k == idx
            else:
                hit = cur == m
            rk = jnp.where(hit, float(a), rk)
            cur = jnp.where(hit, NEG_INF, cur)
            vals_ref[p, a:a + 1, :] = m
        rk_ref[p] = rk
        return jnp.sum(jnp.where(rk < float(PEER_TOPK), 1.0, 0.0), axis=0, keepdims=True)

    def head_body(h, carry):
        taken = jnp.zeros((1, tl), F32)
        for p in range(2):
            row0 = pl.multiple_of(h * PEER_DK + p * (PEER_DK // 2), PEER_DK // 2)
            qs = qT_ref[pl.ds(row0, PEER_DK // 2), :]
            s_ref[p] = jnp.dot(keys_ref[2 * h + p], qs, preferred_element_type=F32)
            taken = jnp.maximum(taken, rank_top16(p, stable=False))

        @pl.when(jnp.max(taken) > float(PEER_TOPK))
        def _():
            for p in range(2):
                rank_top16(p, stable=True)

        for c, (a, b) in enumerate(_PEER_CANDS):
            cand_ref[c:c + 1, :] = vals_ref[0, a:a + 1, :] + vals_ref[1, b:b + 1, :]
        if _PEER_NCAND_PAD > _PEER_NCAND:
            cand_ref[_PEER_NCAND:_PEER_NCAND_PAD, :] = jnp.full((_PEER_NCAND_PAD - _PEER_NCAND, tl), NEG_INF, F32)
        ngrp = _PEER_NCAND_PAD // 8
        iota8 = lax.broadcasted_iota(jnp.int32, (8, tl), 0)
        ranks = [jnp.zeros((8, tl), F32) for _ in range(ngrp)]
        for cp in range(_PEER_NCAND):
            rowb = cand_ref[cp:cp + 1, :]
            for k in range(ngrp):
                blk = cand_ref[8 * k:8 * k + 8, :]
                if 8 * k > cp:
                    inc = jnp.where(rowb >= blk, 1.0, 0.0)
                elif 8 * k + 7 < cp:
                    inc = jnp.where(rowb > blk, 1.0, 0.0)
                else:
                    inc = jnp.where(iota8 + 8 * k > cp, jnp.where(rowb >= blk, 1.0, 0.0), jnp.where(rowb > blk, 1.0, 0.0))
                ranks[k] = ranks[k] + inc
        top = cand_ref[0:1, :]
        z = jnp.zeros((1, tl), F32)
        for k in range(ngrp):
            blk = cand_ref[8 * k:8 * k + 8, :]
            selk = ranks[k] < float(PEER_TOPK)
            sel_ref[8 * k:8 * k + 8, :] = jnp.where(selk, 1.0, 0.0)
            z = z + jnp.sum(jnp.where(selk, jnp.exp(blk - top), 0.0), axis=0, keepdims=True)
        rk1 = rk_ref[0]
        lam = jnp.full((N_KEYS, tl), -1.0, F32)
        for a in range(PEER_TOPK):
            g0, gl = _PEER_GROUP_START[a], _PEER_GROUP_LEN[a]
            la = jnp.sum(sel_ref[g0:g0 + gl, :], axis=0, keepdims=True) - 1.0
            lam = jnp.where(rk1 == float(a), la, lam)
        alpha = jnp.where(rk1 < float(PEER_TOPK), jnp.exp(s_ref[0] - vals_ref[0, 0:1, :]), 0.0)
        rk2 = rk_ref[1]
        beta = jnp.where(rk2 < float(PEER_TOPK), jnp.exp(s_ref[1] - vals_ref[1, 0:1, :]), 0.0) / z
        r2_ref[h] = rk2.astype(BF16)
        beta_ref[h] = beta.astype(BF16)
        alpha_ref[h] = alpha
        lam_ref[h] = lam
        return carry

    lax.fori_loop(0, PEER_HEADS, head_body, 0)


def peer_route(xT, wqT, keys2):
    d, n = xT.shape
    tl = LANES
    assert n % tl == 0
    nq = PEER_HEADS * PEER_DK
    out_bf = jax.ShapeDtypeStruct((PEER_HEADS, N_KEYS, n), BF16)
    out_f = jax.ShapeDtypeStruct((PEER_HEADS, N_KEYS, n), F32)
    tab_spec = pl.BlockSpec((PEER_HEADS, N_KEYS, tl), lambda i: (0, 0, i))
    return pl.pallas_call(
        _peer_route_kernel,
        out_shape=(out_bf, out_bf, out_f, out_f),
        grid=(n // tl,),
        in_specs=[pl.BlockSpec((d, tl), lambda i: (0, i)),
                  pl.BlockSpec((nq, d), lambda i: (0, 0)),
                  pl.BlockSpec((2 * PEER_HEADS, N_KEYS, PEER_DK // 2), lambda i: (0, 0, 0))],
        out_specs=(tab_spec, tab_spec, tab_spec, tab_spec),
        scratch_shapes=[pltpu.VMEM((nq, tl), BF16),
                        pltpu.VMEM((2, N_KEYS, tl), F32),
                        pltpu.VMEM((2, N_KEYS, tl), F32),
                        pltpu.VMEM((2, PEER_TOPK, tl), F32),
                        pltpu.VMEM((_PEER_NCAND_PAD, tl), F32),
                        pltpu.VMEM((_PEER_NCAND_PAD, tl), F32)],
        compiler_params=pltpu.CompilerParams(dimension_semantics=("arbitrary",), vmem_limit_bytes=VMEM_LIMIT),
        name="peer_route",
    )(xT, wqT, keys2)


def _gelu_tanh(x):
    return 0.5 * x * (1.0 + jnp.tanh(math.sqrt(2.0 / math.pi) * (x + 0.044715 * (x * x * x))))


def _peer_dense_kernel(xT_ref, u_ref, vT_ref, r2_ref, beta_ref, alpha_ref, lam_ref, yT_ref, a_ref, h_ref):
    j = pl.program_id(1)
    te = u_ref.shape[0]
    tl = xT_ref.shape[1]

    @pl.when(j == 0)
    def _():
        yT_ref[...] = jnp.zeros_like(yT_ref)

    n_sub = te // N_KEYS

    def pre_activation(r):
        rows = slice(r * N_KEYS, (r + 1) * N_KEYS)
        a_ref[rows, :] = jnp.dot(u_ref[rows, :], xT_ref[...], preferred_element_type=F32)

    pre_activation(0)
    for r in range(n_sub):
        rows = slice(r * N_KEYS, (r + 1) * N_KEYS)
        if r + 1 < n_sub:
            pre_activation(r + 1)
        g = jnp.zeros((N_KEYS, tl), BF16)
        for h in range(PEER_HEADS):
            lam = lam_ref[h, r:r + 1, :].astype(BF16)
            alp = alpha_ref[h, r:r + 1, :].astype(BF16)
            g = g + jnp.where(r2_ref[h] <= lam, beta_ref[h], jnp.zeros((), BF16)) * alp
        h_ref[rows, :] = _gelu_tanh(a_ref[rows, :].astype(BF16)) * g
        if r % PEER_SUB_PER_DOT == PEER_SUB_PER_DOT - 1:
            cols = slice((r + 1 - PEER_SUB_PER_DOT) * N_KEYS, (r + 1) * N_KEYS)
            yT_ref[...] += jnp.dot(vT_ref[:, cols], h_ref[cols, :], preferred_element_type=F32)


def peer_dense(xT, u_bf, vT_bf, r2, beta, alpha, lam, tl, te):
    d, n = xT.shape
    e = u_bf.shape[0]
    assert n % tl == 0 and e % te == 0 and te % (8 * N_KEYS) == 0
    tab_spec = pl.BlockSpec((PEER_HEADS, N_KEYS, tl), lambda i, j: (0, 0, i))
    row_spec = pl.BlockSpec((PEER_HEADS, te // N_KEYS, tl), lambda i, j: (0, j, i))
    return pl.pallas_call(
        _peer_dense_kernel,
        out_shape=jax.ShapeDtypeStruct((d, n), F32),
        grid=(n // tl, e // te),
        in_specs=[pl.BlockSpec((d, tl), lambda i, j: (0, i)),
                  pl.BlockSpec((te, d), lambda i, j: (j, 0)),
                  pl.BlockSpec((d, te), lambda i, j: (0, j)),
                  tab_spec, tab_spec, row_spec, row_spec],
        out_specs=pl.BlockSpec((d, tl), lambda i, j: (0, i)),
        scratch_shapes=[pltpu.VMEM((te, tl), F32), pltpu.VMEM((te, tl), BF16)],
        compiler_params=pltpu.CompilerParams(dimension_semantics=("arbitrary", "arbitrary"), vmem_limit_bytes=VMEM_LIMIT),
        name="peer_dense",
    )(xT, u_bf, vT_bf, r2, beta, alpha, lam)


def peer_ffn_t(xT, peer_w):
    wqT, keys2, u_bf, vT_bf = peer_w
    n = xT.shape[1]
    tl = PEER_TL if n % PEER_TL == 0 else LANES
    r2, beta, alpha, lam = peer_route(xT, wqT, keys2)
    return peer_dense(xT, u_bf, vT_bf, r2, beta, alpha, lam, tl, PEER_TE)


def peer_ffn(h, peer_w):
    b, t, d = h.shape
    n = b * t
    npad = -(-n // LANES) * LANES
    xT = jnp.pad(h.reshape(n, d).astype(BF16).T, ((0, 0), (0, npad - n)))
    return peer_ffn_t(xT, peer_w).T[:n].reshape(b, t, d)


POST_TM = 512
_POST_ROWS = 8


def _post_kernel(x_ref, y_ref, vec_ref, ox_ref, *oh_ref, alpha, y_transposed, h_transposed):
    y = y_ref[...]
    if y_transposed:
        y = y.T
    vec = vec_ref[0]
    r = alpha * x_ref[...] + vec[0:1] * y
    mu = jnp.mean(r, axis=-1, keepdims=True)
    rc = r - mu
    var = jnp.mean(rc * rc, axis=-1, keepdims=True)
    xn = rc * lax.rsqrt(var + LN_EPS) * vec[3:4] + vec[4:5]
    ox_ref[...] = xn
    if oh_ref:
        hm = xn * vec[1:2] + vec[2:3]
        oh_ref[0][...] = (hm.T if h_transposed else hm).astype(BF16)


def residual_norm_modulate(x, y, gate, ln_g, ln_b, scale, shift, alpha, tokens_per_batch, y_transposed, h_transposed):
    n, d = x.shape
    tm = POST_TM
    assert n % tm == 0 and tokens_per_batch % tm == 0
    nb = gate.shape[0]
    want_h = scale is not None
    if not want_h:
        scale = shift = jnp.zeros_like(gate)
    vec = jnp.stack([gate, scale, shift, jnp.broadcast_to(ln_g, (nb, d)), jnp.broadcast_to(ln_b, (nb, d))], axis=1)
    vec = jnp.pad(vec, ((0, 0), (0, _POST_ROWS - vec.shape[1]), (0, 0)))
    tiles_per_batch = tokens_per_batch // tm
    y_spec = pl.BlockSpec((d, tm), lambda i: (0, i)) if y_transposed else pl.BlockSpec((tm, d), lambda i: (i, 0))
    out_shape = [jax.ShapeDtypeStruct((n, d), F32)]
    out_specs = [pl.BlockSpec((tm, d), lambda i: (i, 0))]
    if want_h:
        out_shape.append(jax.ShapeDtypeStruct((d, n) if h_transposed else (n, d), BF16))
        out_specs.append(pl.BlockSpec((d, tm), lambda i: (0, i)) if h_transposed else pl.BlockSpec((tm, d), lambda i: (i, 0)))
    outs = pl.pallas_call(
        functools.partial(_post_kernel, alpha=alpha, y_transposed=y_transposed, h_transposed=h_transposed),
        out_shape=tuple(out_shape),
        grid=(n // tm,),
        in_specs=[pl.BlockSpec((tm, d), lambda i: (i, 0)), y_spec,
                  pl.BlockSpec((1, _POST_ROWS, d), lambda i: (i // tiles_per_batch, 0, 0))],
        out_specs=tuple(out_specs),
        compiler_params=pltpu.CompilerParams(dimension_semantics=("arbitrary",), vmem_limit_bytes=VMEM_LIMIT),
        name="residual_norm_modulate",
    )(x, y, vec)
    return (outs[0], outs[1]) if want_h else (outs[0], None)


def token_mixers(parts, pos, rel_table, w_cmp, conv_w, conv_b, ret_g, past, nsa_tables):
    nq, nkv, ngate, rq, rk, rv, rg, cb, cc, ch = parts
    b, t = nq.shape[:2]
    q = nq.reshape(b, t, NSA_KV_HEADS, NSA_QPG, HEAD_DIM)
    kv = nkv.reshape(b, t, 6, NSA_KV_HEADS, HEAD_DIM)
    if past is None:
        kc = compress(kv[:, :, 0], w_cmp[0])
        vc = compress(kv[:, :, 1], w_cmp[1])
        o_nsa = nsa_prompt(nq, kv, ngate, kc, vc, nsa_tables)
        win_rows = kv[:, t - min(WINDOW, t):, 4:]
        s0 = jnp.zeros((b, RET_HEADS, RET_DK, RET_DV), jnp.float32)
        zbuf = jnp.zeros((b, CONV_W - 1, CONV_CH), ch.dtype)
        chunk = RET_CHUNK
    else:
        decode_nsa, win_buf, s0, zbuf = past
        o_nsa = decode_nsa(nq, nkv, ngate)
        wrows = jnp.concatenate([win_buf.astype(kv.dtype), kv[:, :, 4:]], axis=1)
        win_rows = wrows[:, wrows.shape[1] - min(WINDOW, wrows.shape[1]):]
        chunk = t
    rqh = rotary(rq.reshape(b, t, RET_HEADS, RET_DK), pos)
    rkh = rotary(rk.reshape(b, t, RET_HEADS, RET_DK), pos) * (RET_DK ** -0.5)
    rvh = rv.reshape(b, t, RET_HEADS, RET_DV)
    o_r, s_new = retention(rqh, rkh, rvh, s0, chunk)
    mu = jnp.mean(o_r, -1, keepdims=True)
    var = jnp.mean(jnp.square(o_r - mu), -1, keepdims=True)
    on = (o_r - mu) * lax.rsqrt(var + LN_EPS) * ret_g.reshape(RET_HEADS, RET_DV)
    o_ret = (on.reshape(b, t, RET_WIDTH) * jax.nn.silu(rg.astype(jnp.float32))).astype(nq.dtype)
    z = cc * ch
    zp = jnp.concatenate([zbuf.astype(z.dtype), z], axis=1)
    y = conv_b + sum(zp[:, j:j + t] * conv_w[j] for j in range(CONV_W))
    o_conv = (cb * y).astype(nq.dtype)
    mix = jnp.concatenate([o_nsa, o_ret, o_conv], axis=-1)
    return mix, (kv[:, :, :4], win_rows, s_new, zp[:, t:])


def kernel(x_prompt, x_sample, cache_nsa_kv, cache_win_kv, state_ret, state_conv, page_table, c_prompt, c_sample, rel_bias, w_ada, b_ada, w_in, w_cmp, conv_w, conv_b, ret_norm_g, w_out, ln1_g, ln1_b, ln2_g, ln2_b, peer_wq, peer_keys, peer_u, peer_v):
    alpha = (2.0 * DEPTH) ** 0.25
    n_pages = page_table.shape[1]
    past_len = n_pages * PAGE_SIZE
    sp = x_prompt.shape[1]
    bs, ts = x_sample.shape[:2]
    pos_p = jnp.arange(sp, dtype=jnp.int32)
    pos_s = past_len + jnp.arange(ts, dtype=jnp.int32)
    nsa_tables = nsa_bias_tables(rel_bias, sp, NSA_TQ)
    dec_tables = decode_tables(rel_bias, past_len)
    cache_nsa = cache_nsa_kv.transpose(0, 1, 3, 4, 5, 2).reshape(cache_nsa_kv.shape[0], cache_nsa_kv.shape[1], ROW_WIDTH, PAGE_SIZE)
    cache_win = cache_win_kv.transpose(0, 1, 3, 4, 5, 2).reshape(cache_win_kv.shape[0], bs, HALF_WIDTH, cache_win_kv.shape[2])

    def ada(c, l):
        return (jax.nn.silu(c) @ w_ada[l] + b_ada[l]).reshape(c.shape[0], 6, D_MODEL)

    def run_layer(x, c, l, pos, past, peer_w):
        m = ada(c, l)[:, :, None, :]
        h = x * (1.0 + m[:, 1]) + m[:, 0]
        parts = split_proj(h @ w_in[l])
        mix, st = token_mixers(parts, pos, rel_bias, w_cmp[l], conv_w[l], conv_b[l], ret_norm_g[l], past, nsa_tables)
        x = layer_norm(alpha * x + (1.0 + m[:, 2]) * (mix @ w_out[l]), ln1_g[l], ln1_b[l])
        h = x * (1.0 + m[:, 4]) + m[:, 3]
        y = peer_ffn(h, peer_w)
        x = layer_norm(alpha * x + (1.0 + m[:, 5]) * y, ln2_g[l], ln2_b[l])
        return x, st

    bp = x_prompt.shape[0]
    m_prompt = [ada(c_prompt, l) for l in range(DEPTH)]

    def run_prompt_layer(x2, h, l, peer_w):
        m = m_prompt[l]
        parts = split_proj(jnp.matmul(h, w_in[l].astype(h.dtype), preferred_element_type=F32))
        mix, st = token_mixers(parts, pos_p, rel_bias, w_cmp[l], conv_w[l], conv_b[l], ret_norm_g[l], None, nsa_tables)
        y = (mix @ w_out[l]).reshape(bp * sp, D_MODEL)
        x2, h_t = residual_norm_modulate(x2, y, 1.0 + m[:, 2], ln1_g[l], ln1_b[l], 1.0 + m[:, 4], m[:, 3], alpha, sp,
                                         y_transposed=False, h_transposed=True)
        y_t = peer_ffn_t(h_t, peer_w)
        if l + 1 < DEPTH:
            nxt = m_prompt[l + 1]
            x2, h = residual_norm_modulate(x2, y_t, 1.0 + m[:, 5], ln2_g[l], ln2_b[l], 1.0 + nxt[:, 1], nxt[:, 0], alpha, sp,
                                           y_transposed=True, h_transposed=False)
            h = h.reshape(bp, sp, D_MODEL)
        else:
            x2, h = residual_norm_modulate(x2, y_t, 1.0 + m[:, 5], ln2_g[l], ln2_b[l], None, None, alpha, sp,
                                           y_transposed=True, h_transposed=False)
        return x2, h, st

    xp2 = x_prompt.reshape(bp * sp, D_MODEL)
    hp = x_prompt * (1.0 + m_prompt[0][:, 1][:, None, :]) + m_prompt[0][:, 0][:, None, :]
    xs = x_sample
    nkv_p, nkv_s, win_p, win_s, ret_p, ret_s, conv_p, conv_s = [], [], [], [], [], [], [], []
    for l in range(DEPTH):
        peer_w = (peer_wq[l].T.astype(BF16),
                  peer_keys[l].reshape(2 * PEER_HEADS, N_KEYS, PEER_DK // 2).astype(BF16),
                  peer_u[l].astype(BF16),
                  peer_v[l].T.astype(BF16))
        xp2, hp, st_p = run_prompt_layer(xp2, hp, l, peer_w)
        decode_nsa = functools.partial(nsa_decode, page_table=page_table, cache_nsa=cache_nsa, cache_win=cache_win,
                                       w_bd=compress_block_weights(w_cmp[l]), tables=dec_tables, layer=l)
        xs, st_s = run_layer(xs, c_sample, l, pos_s, (decode_nsa, cache_win_kv[l], state_ret[l], state_conv[l]), peer_w)
        nkv_p.append(st_p[0])
        win_p.append(st_p[1])
        ret_p.append(st_p[2])
        conv_p.append(st_p[3])
        nkv_s.append(st_s[0])
        win_s.append(st_s[1])
        ret_s.append(st_s[2])
        conv_s.append(st_s[3])
    xp = xp2.reshape(bp, sp, D_MODEL)
    return (xp, xs, jnp.stack(nkv_p), jnp.stack(nkv_s), jnp.stack(win_p), jnp.stack(win_s), jnp.stack(ret_p), jnp.stack(ret_s), jnp.stack(conv_p), jnp.stack(conv_s))
```

```python
import math
import functools
import jax
import jax.numpy as jnp
from jax import lax
import numpy as np
from jax.experimental import pallas as pl
from jax.experimental.pallas import tpu as pltpu

D_MODEL = 1024
BATCH = 4
SEQ = 4096
DEPTH = 2
DEC_BATCH = 32
DEC_SEQ = 1
PAST_LEN = 8192
PAGE_SIZE = 128

HEAD_DIM = 64
NSA_WIDTH = D_MODEL // 2
NSA_HEADS = NSA_WIDTH // HEAD_DIM
NSA_KV_HEADS = 2
NSA_QPG = NSA_HEADS // NSA_KV_HEADS
KV_WIDTH = NSA_KV_HEADS * HEAD_DIM
SCALE = HEAD_DIM ** -0.5
L_CMP = 32
L_SEL = 64
N_SEL = 16
WINDOW = 512
Q_BLOCK = 128
SEL_Q_BLOCK = 64
N_BUCKETS = 32
REL_MAX_DIST = 128
RET_WIDTH = D_MODEL // 4
RET_DK = 64
RET_DV = 64
RET_HEADS = RET_WIDTH // RET_DV
RET_CHUNK = 128
ROPE_BASE = 10000.0
CONV_CH = D_MODEL // 4
CONV_W = 3
MIX_WIDTH = NSA_WIDTH + RET_WIDTH + CONV_CH
PEER_HEADS = 8
PEER_DK = 256
N_KEYS = 128
N_EXPERTS = N_KEYS * N_KEYS
PEER_TOPK = 16
PEER_CHUNK = 256
LN_EPS = 1e-5
SPLIT_SIZES = (NSA_WIDTH, 6 * KV_WIDTH, 3 * NSA_HEADS, RET_HEADS * RET_DK, RET_HEADS * RET_DK, RET_WIDTH, RET_WIDTH, CONV_CH, CONV_CH, CONV_CH)
N_IN = sum(SPLIT_SIZES)

F32 = jnp.float32
BF16 = jnp.bfloat16
NEG_INF = float('-inf')
MASK_NEG = -1e9
CMP_MASK_NEG = -1e30
NSA_TQ = 256
NSA_RB = 128
ROW_WIDTH = 4 * KV_WIDTH
HALF_WIDTH = 2 * KV_WIDTH
PEER_TL = 512
PEER_TE = 1024
PEER_SUB_PER_DOT = 2
LANES = 128
VMEM_LIMIT = 56 * 1024 * 1024


def _ln_kernel(x_ref, g_ref, b_ref, o_ref):
    x = x_ref[...]
    mu = jnp.mean(x, -1, keepdims=True)
    xc = x - mu
    var = jnp.mean(xc * xc, -1, keepdims=True)
    o_ref[...] = xc * lax.rsqrt(var + LN_EPS) * g_ref[...] + b_ref[...]


def layer_norm(x, g, b):
    shp = x.shape
    x2 = x.reshape(-1, shp[-1])
    n = x2.shape[0]
    tm = min(n, 512)
    out = pl.pallas_call(
        _ln_kernel,
        out_shape=jax.ShapeDtypeStruct(x2.shape, jnp.float32),
        grid=(n // tm,),
        in_specs=[pl.BlockSpec((tm, shp[-1]), lambda i: (i, 0)),
                  pl.BlockSpec((1, shp[-1]), lambda i: (0, 0)),
                  pl.BlockSpec((1, shp[-1]), lambda i: (0, 0))],
        out_specs=pl.BlockSpec((tm, shp[-1]), lambda i: (i, 0)),
        name="layer_norm",
    )(x2, g.reshape(1, -1), b.reshape(1, -1))
    return out.reshape(shp)


def t5_bucket(dist):
    n = jnp.maximum(dist, 0)
    exact = N_BUCKETS // 2
    nf = jnp.maximum(n, exact).astype(jnp.float32)
    big = exact + (jnp.log(nf / exact) / math.log(REL_MAX_DIST / exact) * (N_BUCKETS - exact)).astype(jnp.int32)
    return jnp.where(n < exact, n, jnp.minimum(big, N_BUCKETS - 1))


def rel_bias_lookup(rel_bias, dist):
    onehot = jax.nn.one_hot(t5_bucket(dist), N_BUCKETS, dtype=jnp.float32)
    return jnp.einsum('...k,kh->...h', onehot, rel_bias, precision=lax.Precision.HIGHEST)


def rotary(x, pos):
    half = x.shape[-1] // 2
    inv = ROPE_BASE ** (-jnp.arange(half, dtype=jnp.float32) / half)
    ang = pos.astype(jnp.float32)[:, None] * inv[None, :]
    cos = jnp.cos(ang)[None, :, None, :]
    sin = jnp.sin(ang)[None, :, None, :]
    x1, x2 = x[..., :half], x[..., half:]
    return jnp.concatenate([x1 * cos - x2 * sin, x1 * sin + x2 * cos], axis=-1)


def split_proj(p):
    outs, start = [], 0
    for size in SPLIT_SIZES:
        outs.append(p[..., start:start + size])
        start += size
    return outs


def compress(rows, w):
    b, lp, g, d = rows.shape
    blk = rows.reshape(b, lp // L_CMP, L_CMP, g, d).transpose(0, 1, 3, 2, 4).reshape(b, lp // L_CMP, g, L_CMP * d)
    return blk @ w


def retention(q, k, v, s0, chunk):
    b, t = q.shape[:2]
    nc = t // chunk
    lg = jnp.log(1.0 - 2.0 ** (-5.0 - jnp.arange(RET_HEADS, dtype=jnp.float32)))
    i = jnp.arange(chunk, dtype=jnp.float32)
    diff = i[:, None] - i[None, :]
    dmat = jnp.where(diff >= 0, jnp.exp(jnp.maximum(diff, 0.0)[None] * lg[:, None, None]), 0.0)
    q_dec = jnp.exp((i + 1.0)[:, None] * lg[None, :])[None, :, :, None]
    k_dec = jnp.exp((chunk - 1.0 - i)[:, None] * lg[None, :])[None, :, :, None]
    s_dec = jnp.exp(chunk * lg)[None, :, None, None]

    def to_chunks(a):
        return jnp.moveaxis(a.astype(jnp.float32).reshape(b, nc, chunk, RET_HEADS, a.shape[-1]), 1, 0)

    def step(s, xs):
        qc, kc, vc = xs
        att = jnp.einsum('bihd,bjhd->bhij', qc, kc) * dmat
        o = jnp.einsum('bhij,bjhe->bihe', att, vc) + jnp.einsum('bihd,bhde->bihe', qc * q_dec, s)
        s = s * s_dec + jnp.einsum('bjhd,bjhe->bhde', kc * k_dec, vc)
        return s, o

    s, o = lax.scan(step, s0.astype(jnp.float32), (to_chunks(q), to_chunks(k), to_chunks(v)))
    return jnp.moveaxis(o, 0, 1).reshape(b, t, RET_HEADS, RET_DV), s


def _retention_kernel(q_ref, k_ref, v_ref, g_ref, cos_ref, sin_ref, dmat_ref, qdec_ref, kdec_ref, sdec_ref, gain_ref,
                      o_ref, sout_ref, s_scr):
    c = pl.program_id(1)

    @pl.when(c == 0)
    def _():
        s_scr[...] = jnp.zeros_like(s_scr)

    chunk, width = q_ref.shape[1], q_ref.shape[2]
    half = RET_DK // 2
    lane = lax.broadcasted_iota(jnp.int32, (chunk, width), 1)
    first_half = (lane % RET_DK) < half

    def rot(x):
        swapped = jnp.where(first_half, pltpu.roll(x, width - half, 1), pltpu.roll(x, half, 1))
        return x * cos_ref[...] + swapped * sin_ref[...]

    q = rot(q_ref[0])
    k = rot(k_ref[0]) * (RET_DK ** -0.5)
    v = v_ref[0]
    gate = g_ref[0]
    eye = (lax.broadcasted_iota(jnp.int32, (RET_DK, RET_DK), 0)
           == lax.broadcasted_iota(jnp.int32, (RET_DK, RET_DK), 1)).astype(F32).astype(BF16)
    for h in range(RET_HEADS):
        sl = slice(h * RET_DK, (h + 1) * RET_DK)
        qh, kh, vh = q[:, sl], k[:, sl], v[:, sl].astype(BF16)
        att = _dot_nt(qh.astype(BF16), kh.astype(BF16)) * dmat_ref[h]
        s_old = s_scr[h]
        o = (jnp.dot(att.astype(BF16), vh, preferred_element_type=F32)
             + jnp.dot((qh * qdec_ref[:, sl]).astype(BF16), s_old.astype(BF16), preferred_element_type=F32))
        kd_t = _dot_nt(eye, (kh * kdec_ref[:, sl]).astype(BF16)).astype(BF16)
        s_scr[h] = s_old * sdec_ref[h, 0:1, 0:RET_DV] + jnp.dot(kd_t, vh, preferred_element_type=F32)
        mu = jnp.mean(o, axis=-1, keepdims=True)
        oc = o - mu
        var = jnp.mean(oc * oc, axis=-1, keepdims=True)
        gh = gate[:, sl]
        o_ref[0, :, sl] = oc * lax.rsqrt(var + LN_EPS) * gain_ref[:, sl] * (gh * jax.nn.sigmoid(gh))
    sout_ref[0] = s_scr[...]


def retention_prompt(rq, rk, rv, rg, ret_g):
    b, t, width = rq.shape
    chunk = RET_CHUNK
    half = RET_DK // 2
    pos = jnp.arange(t, dtype=jnp.float32)
    inv = ROPE_BASE ** (-jnp.arange(half, dtype=jnp.float32) / half)
    ang = pos[:, None] * inv[None, :]
    cos = jnp.tile(jnp.cos(ang), (1, 2 * RET_HEADS))
    sin = jnp.tile(jnp.concatenate([-jnp.sin(ang), jnp.sin(ang)], axis=1), (1, RET_HEADS))
    lg = jnp.log(1.0 - 2.0 ** (-5.0 - jnp.arange(RET_HEADS, dtype=jnp.float32)))
    i = jnp.arange(chunk, dtype=jnp.float32)
    diff = i[:, None] - i[None, :]
    dmat = jnp.where(diff >= 0, jnp.exp(jnp.maximum(diff, 0.0)[None] * lg[:, None, None]), 0.0)
    qdec = jnp.repeat(jnp.exp((i + 1.0)[:, None] * lg[None, :]), RET_DK, axis=1)
    kdec = jnp.repeat(jnp.exp((chunk - 1.0 - i)[:, None] * lg[None, :]), RET_DK, axis=1)
    sdec = jnp.broadcast_to(jnp.exp(chunk * lg)[:, None, None], (RET_HEADS, 8, LANES))
    tok = pl.BlockSpec((1, chunk, width), lambda bi, ci: (bi, ci, 0))
    tab = pl.BlockSpec((chunk, width), lambda bi, ci: (ci, 0))

    def const(shape):
        return pl.BlockSpec(shape, lambda bi, ci: (0,) * len(shape))

    return pl.pallas_call(
        _retention_kernel,
        out_shape=(jax.ShapeDtypeStruct((b, t, width), F32),
                   jax.ShapeDtypeStruct((b, RET_HEADS, RET_DK, RET_DV), F32)),
        grid=(b, t // chunk),
        in_specs=[tok, tok, tok, tok, tab, tab, const((RET_HEADS, chunk, chunk)), const((chunk, width)),
                  const((chunk, width)), const((RET_HEADS, 8, LANES)), const((1, width))],
        out_specs=(tok, pl.BlockSpec((1, RET_HEADS, RET_DK, RET_DV), lambda bi, ci: (bi, 0, 0, 0))),
        scratch_shapes=[pltpu.VMEM((RET_HEADS, RET_DK, RET_DV), F32)],
        compiler_params=pltpu.CompilerParams(dimension_semantics=("arbitrary", "arbitrary"), vmem_limit_bytes=VMEM_LIMIT),
        name="retention_prompt",
    )(rq, rk, rv, rg, cos, sin, dmat, qdec, kdec, sdec, ret_g.reshape(1, width))


def _nsa_prompt_kernel(q_ref, gate_ref, kcT_ref, vc_ref, cb_ref, ksT_ref, vs_ref, kwT_ref, vw_ref, sb_ref, wb_ref,
                       o_ref, qa_ref, m_ref, acc_ref, oc_ref, osel_ref, owin_ref):
    qi = pl.program_id(2)
    tq = q_ref.shape[3]
    ncb = kcT_ref.shape[3]
    ka = ksT_ref.shape[2]
    t0 = qi * tq
    rows = NSA_QPG * tq

    imp = jnp.zeros((tq, ncb), F32)
    for h in range(NSA_QPG):
        bias = cb_ref[0, h]
        lg = jnp.dot(q_ref[0, 0, h], kcT_ref[0, 0], preferred_element_type=F32) + bias
        mx = jnp.max(lg, axis=-1, keepdims=True)
        e = jnp.where(bias > 0.5 * CMP_MASK_NEG, jnp.exp(lg - mx), 0.0)
        p = e / jnp.maximum(jnp.sum(e, axis=-1, keepdims=True), 1e-30)
        imp = imp + p
        oc_ref[h] = jnp.dot(p.astype(BF16), vc_ref[0, 0], preferred_element_type=F32)

    lane = lax.broadcasted_iota(jnp.int32, (tq, ncb), 1)
    tpos = t0 + lax.broadcasted_iota(jnp.int32, (tq, ncb), 0)
    pair = imp + pltpu.roll(imp, ncb - 1, 1)
    blk = lane >> 1
    cur = tpos >> 6
    forced = (blk == 0) | (blk == cur) | (blk == cur - 1)
    cand = ((lane & 1) == 0) & (blk <= cur)
    score = jnp.where(cand, jnp.where(forced, jnp.inf, pair), NEG_INF)
    lane_f = lane.astype(F32)
    chosen = jnp.zeros((tq, ncb), F32)
    for _ in range(N_SEL):
        mx = jnp.max(score, axis=-1, keepdims=True)
        first = jnp.min(jnp.where(score == mx, lane_f, float(ncb)), axis=-1, keepdims=True)
        hit = (lane_f == first) & (mx > NEG_INF)
        chosen = jnp.where(hit, 1.0, chosen)
        score = jnp.where(hit, NEG_INF, score)
    blockmask = jnp.where(chosen > 0.0, 0.0, MASK_NEG).astype(BF16)

    for h in range(NSA_QPG):
        qa_ref[h * tq:(h + 1) * tq, 0:ncb] = blockmask
        qa_ref[h * tq:(h + 1) * tq, ncb:ncb + HEAD_DIM] = q_ref[0, 0, h]
        if ka > ncb + HEAD_DIM:
            qa_ref[h * tq:(h + 1) * tq, ncb + HEAD_DIM:ka] = jnp.zeros((tq, ka - ncb - HEAD_DIM), BF16)

    nrb = rows // NSA_RB
    per_head = tq // NSA_RB

    def reset():
        m_ref[...] = jnp.full((rows, LANES), CMP_MASK_NEG, F32)
        acc_ref[...] = jnp.zeros((rows, LANES), F32)

    def attn_step(rb, qrb, k_t, v, bias):
        r0 = rb * NSA_RB
        s = jnp.dot(qrb, k_t, preferred_element_type=F32)
        if bias is not None:
            s = s + bias
        parts = [s[:, i * LANES:(i + 1) * LANES] for i in range(s.shape[1] // LANES)]
        red = parts[0]
        for part in parts[1:]:
            red = jnp.maximum(red, part)
        m_old = m_ref[r0:r0 + NSA_RB, :]
        m_new = jnp.maximum(m_old, jnp.max(red, axis=-1, keepdims=True))
        p = jnp.concatenate([jnp.exp(part - m_new) for part in parts], axis=1).astype(BF16)
        acc_ref[r0:r0 + NSA_RB, :] = (jnp.exp(m_old - m_new) * acc_ref[r0:r0 + NSA_RB, :]
                                      + jnp.dot(p, v, preferred_element_type=F32))
        m_ref[r0:r0 + NSA_RB, :] = m_new

    def finish(dst_ref):
        acc = acc_ref[...]
        dst_ref[...] = acc[:, 0:HEAD_DIM] / acc[:, HEAD_DIM:HEAD_DIM + 1]

    reset()

    def far_body(c, carry):
        col = pl.multiple_of((c + 1) * tq, tq)
        k_t = ksT_ref[0, 0, :, pl.ds(col, tq)]
        v = vs_ref[0, 0, pl.ds(col, tq), :]
        for rb in range(nrb):
            attn_step(rb, qa_ref[rb * NSA_RB:(rb + 1) * NSA_RB, :], k_t, v, None)
        return carry

    lax.fori_loop(0, jnp.maximum(qi - 1, 0), far_body, 0)

    col0 = pl.multiple_of(t0, tq)
    for c in range(2):
        colc = pl.multiple_of(col0 + c * tq, tq)
        k_t = ksT_ref[0, 0, :, pl.ds(colc, tq)]
        v = vs_ref[0, 0, pl.ds(colc, tq), :]
        jn = lax.broadcasted_iota(jnp.int32, (1, tq), 1) + c * tq
        colmask = jnp.where(jn + (t0 - tq) >= 0, 0.0, MASK_NEG)
        for rb in range(nrb):
            h, part = rb // per_head, rb % per_head
            bias = sb_ref[0, h, part * NSA_RB:(part + 1) * NSA_RB, c * tq:(c + 1) * tq] + colmask
            attn_step(rb, qa_ref[rb * NSA_RB:(rb + 1) * NSA_RB, :], k_t, v, bias)
    finish(osel_ref)

    reset()
    for c in range((WINDOW + tq) // tq):
        colc = pl.multiple_of(col0 + c * tq, tq)
        k_t = kwT_ref[0, 0, :, pl.ds(colc, tq)]
        v = vw_ref[0, 0, pl.ds(colc, tq), :]
        jw = lax.broadcasted_iota(jnp.int32, (1, tq), 1) + c * tq
        wmask = jnp.where(jw + (t0 - WINDOW) >= 0, 0.0, MASK_NEG)
        for rb in range(nrb):
            h, part = rb // per_head, rb % per_head
            bias = wb_ref[0, h, part * NSA_RB:(part + 1) * NSA_RB, c * tq:(c + 1) * tq] + wmask
            attn_step(rb, qa_ref[rb * NSA_RB:(rb + 1) * NSA_RB, ncb:ncb + HEAD_DIM], k_t, v, bias)
    finish(owin_ref)

    g = jax.nn.sigmoid(gate_ref[0, 0])
    for h in range(NSA_QPG):
        o_h = (g[:, 3 * h:3 * h + 1] * oc_ref[h]
               + g[:, 3 * h + 1:3 * h + 2] * osel_ref[h * tq:(h + 1) * tq, :]
               + g[:, 3 * h + 2:3 * h + 3] * owin_ref[h * tq:(h + 1) * tq, :])
        o_ref[0, :, h * HEAD_DIM:(h + 1) * HEAD_DIM] = o_h


def nsa_bias_tables(rel_bias, t, tq):
    ncb = t // L_CMP

    def heads_first(tab):
        return tab.transpose(2, 0, 1).reshape(NSA_KV_HEADS, NSA_QPG, tab.shape[0], tab.shape[1])

    pos = jnp.arange(t, dtype=jnp.int32)
    end = jnp.arange(ncb, dtype=jnp.int32) * L_CMP + (L_CMP - 1)
    dist = pos[:, None] - end[None, :]
    cb = jnp.where((dist >= 0)[..., None], rel_bias_lookup(rel_bias, dist), CMP_MASK_NEG)
    i = jnp.arange(tq, dtype=jnp.int32)
    dist = i[:, None] + tq - jnp.arange(2 * tq, dtype=jnp.int32)[None, :]
    sb = jnp.where((dist >= 0)[..., None], rel_bias_lookup(rel_bias, dist) - rel_bias[N_BUCKETS - 1], MASK_NEG)
    dist = i[:, None] + WINDOW - jnp.arange(WINDOW + tq, dtype=jnp.int32)[None, :]
    wb = jnp.where(((dist >= 0) & (dist < WINDOW))[..., None], rel_bias_lookup(rel_bias, dist), MASK_NEG)
    return heads_first(cb), heads_first(sb), heads_first(wb)


def nsa_prompt(q, kv, gates, kc, vc, tables):
    b, t = q.shape[:2]
    tq = NSA_TQ
    ncb = t // L_CMP
    ka = -(-(ncb + HEAD_DIM) // LANES) * LANES
    cb, sb, wb = tables
    q4 = (q * SCALE).astype(BF16).reshape(b, t, NSA_KV_HEADS, NSA_QPG, HEAD_DIM).transpose(0, 2, 3, 1, 4)
    g4 = gates.reshape(b, t, NSA_KV_HEADS, NSA_QPG * 3).transpose(0, 2, 1, 3)
    kcT = kc.astype(BF16).transpose(0, 2, 3, 1)
    vcg = vc.astype(BF16).transpose(0, 2, 1, 3)
    kvb = kv.astype(BF16)
    onehot = (2 * (jnp.arange(t, dtype=jnp.int32) // L_SEL)[None, :] == jnp.arange(ncb, dtype=jnp.int32)[:, None]).astype(BF16)
    ks_t = kvb[:, :, 2].transpose(0, 2, 3, 1)
    ksT = jnp.concatenate([jnp.broadcast_to(onehot, (b, NSA_KV_HEADS, ncb, t)), ks_t,
                           jnp.zeros((b, NSA_KV_HEADS, ka - ncb - HEAD_DIM, t), BF16)], axis=2)
    ksT = jnp.pad(ksT, ((0, 0), (0, 0), (0, 0), (tq, 0)))
    def with_ones(v):
        one = jnp.ones(v.shape[:-1] + (1,), BF16)
        return jnp.concatenate([v, one, jnp.zeros(v.shape[:-1] + (LANES - HEAD_DIM - 1,), BF16)], axis=-1)

    vs = jnp.pad(with_ones(kvb[:, :, 3].transpose(0, 2, 1, 3)), ((0, 0), (0, 0), (tq, 0), (0, 0)))
    kwT = jnp.pad(kvb[:, :, 4].transpose(0, 2, 3, 1), ((0, 0), (0, 0), (0, 0), (WINDOW, 0)))
    vw = jnp.pad(with_ones(kvb[:, :, 5].transpose(0, 2, 1, 3)), ((0, 0), (0, 0), (WINDOW, 0), (0, 0)))
    rows = NSA_QPG * tq

    def per_bg(shape):
        return pl.BlockSpec((1, 1) + shape, lambda bi, gi, qi: (bi, gi, 0, 0))

    return pl.pallas_call(
        _nsa_prompt_kernel,
        out_shape=jax.ShapeDtypeStruct((b, t, NSA_HEADS * HEAD_DIM), F32),
        grid=(b, NSA_KV_HEADS, t // tq),
        in_specs=[pl.BlockSpec((1, 1, NSA_QPG, tq, HEAD_DIM), lambda bi, gi, qi: (bi, gi, 0, qi, 0)),
                  pl.BlockSpec((1, 1, tq, NSA_QPG * 3), lambda bi, gi, qi: (bi, gi, qi, 0)),
                  per_bg((HEAD_DIM, ncb)), per_bg((ncb, HEAD_DIM)),
                  pl.BlockSpec((1, NSA_QPG, tq, ncb), lambda bi, gi, qi: (gi, 0, qi, 0)),
                  per_bg((ka, tq + t)), per_bg((tq + t, LANES)),
                  per_bg((HEAD_DIM, WINDOW + t)), per_bg((WINDOW + t, LANES)),
                  pl.BlockSpec((1, NSA_QPG, tq, 2 * tq), lambda bi, gi, qi: (gi, 0, 0, 0)),
                  pl.BlockSpec((1, NSA_QPG, tq, WINDOW + tq), lambda bi, gi, qi: (gi, 0, 0, 0))],
        out_specs=pl.BlockSpec((1, tq, NSA_QPG * HEAD_DIM), lambda bi, gi, qi: (bi, qi, gi)),
        scratch_shapes=[pltpu.VMEM((rows, ka), BF16),
                        pltpu.VMEM((rows, LANES), F32), pltpu.VMEM((rows, LANES), F32),
                        pltpu.VMEM((NSA_QPG, tq, HEAD_DIM), F32),
                        pltpu.VMEM((rows, HEAD_DIM), F32), pltpu.VMEM((rows, HEAD_DIM), F32)],
        compiler_params=pltpu.CompilerParams(dimension_semantics=("arbitrary", "arbitrary", "arbitrary"),
                                             vmem_limit_bytes=VMEM_LIMIT),
        name="nsa_prompt",
    )(q4, g4, kcT, vcg, cb, ksT, vs, kwT, vw, sb, wb)


def _gather_pages(pt_ref, cache_ref, sem, layer, feat0, dst_of):
    s = pl.program_id(0)
    n_pages = pt_ref.shape[1]
    slot = lax.rem(s, 2)

    def copy(page, sl, p):
        return pltpu.make_async_copy(cache_ref.at[layer, page, pl.ds(feat0, HALF_WIDTH), :], dst_of(sl, p), sem.at[sl])

    def start(seq, sl):
        for p in range(n_pages):
            copy(pt_ref[seq, p], sl, p).start()

    @pl.when(s == 0)
    def _():
        start(0, 0)

    @pl.when(s + 1 < pl.num_programs(0))
    def _():
        start(s + 1, 1 - slot)

    for p in range(n_pages):
        copy(0, slot, p).wait()
    return slot


def _decode_compress_kernel(pt_ref, cache_ref, w_ref, out_ref, buf, rows_ref, sem, *, layer):
    slot = _gather_pages(pt_ref, cache_ref, sem, layer, 0, lambda sl, p: buf.at[sl, p])
    ncb = out_ref.shape[1]
    n_pages = pt_ref.shape[1]
    pair = 2 * PAGE_SIZE
    blocks_per_pair = pair // L_CMP
    row = lax.broadcasted_iota(jnp.int32, (pair, pair), 0)
    col = lax.broadcasted_iota(jnp.int32, (pair, pair), 1)
    perm = (col == (row % blocks_per_pair) * L_CMP + row // blocks_per_pair).astype(F32).astype(BF16)
    for pp in range(n_pages // 2):
        pages = jnp.concatenate([buf[slot, 2 * pp], buf[slot, 2 * pp + 1]], axis=1).astype(BF16)
        x = _dot_nt(perm, pages)
        for part in range(2):
            for r in range(L_CMP):
                rows_ref[part, r, pp * blocks_per_pair:(pp + 1) * blocks_per_pair, :] = (
                    x[r * blocks_per_pair:(r + 1) * blocks_per_pair, part * KV_WIDTH:(part + 1) * KV_WIDTH])
    for part in range(2):
        acc = jnp.zeros((ncb, KV_WIDTH), F32)
        for r in range(L_CMP):
            acc = acc + jnp.dot(rows_ref[part, r].astype(BF16), w_ref[r, part], preferred_element_type=F32)
        out_ref[0, :, part * KV_WIDTH:(part + 1) * KV_WIDTH] = acc


def decode_compress(page_table, cache, w_bd, layer):
    bs, n_pages = page_table.shape
    past = n_pages * PAGE_SIZE
    ncb = past // L_CMP
    return pl.pallas_call(
        functools.partial(_decode_compress_kernel, layer=layer),
        out_shape=jax.ShapeDtypeStruct((bs, ncb, HALF_WIDTH), F32),
        grid_spec=pltpu.PrefetchScalarGridSpec(
            num_scalar_prefetch=1,
            grid=(bs,),
            in_specs=[pl.BlockSpec(memory_space=pl.ANY),
                      pl.BlockSpec((L_CMP, 2, KV_WIDTH, KV_WIDTH), lambda s, pt: (0, 0, 0, 0))],
            out_specs=pl.BlockSpec((1, ncb, HALF_WIDTH), lambda s, pt: (s, 0, 0)),
            scratch_shapes=[pltpu.VMEM((2, n_pages, HALF_WIDTH, PAGE_SIZE), F32),
                            pltpu.VMEM((2, L_CMP, ncb, KV_WIDTH), F32), pltpu.SemaphoreType.DMA((2,))]),
        compiler_params=pltpu.CompilerParams(dimension_semantics=("arbitrary",), vmem_limit_bytes=VMEM_LIMIT),
        name="decode_compress",
    )(page_table, cache, w_bd)


def _dot_nt(a, b):
    return lax.dot_general(a, b, (((1,), (1,)), ((), ())), preferred_element_type=F32)


def _decode_attend_kernel(pt_ref, cache_ref, q_ref, gate_ref, kcvc_ref, new_ref, win_ref, cb_ref, sb_ref, wb_ref,
                          rel0_ref, onehot_ref, hmask_ref, o_ref, buf, sem, *, layer):
    slot = _gather_pages(pt_ref, cache_ref, sem, layer, HALF_WIDTH,
                         lambda sl, p: buf.at[sl, :, pl.ds(p * PAGE_SIZE, PAGE_SIZE)])
    q = q_ref[0]
    qf = q.astype(F32)
    ncb = kcvc_ref.shape[1]
    rel0 = rel0_ref[:, 0:1]

    def bf_round(x):
        return x.astype(BF16).astype(F32)

    kc = kcvc_ref[0, :, 0:KV_WIDTH].astype(BF16)
    vc = kcvc_ref[0, :, KV_WIDTH:HALF_WIDTH].astype(BF16)
    lg = _dot_nt(q, kc) + cb_ref[...]
    e = jnp.exp(lg - jnp.max(lg, axis=-1, keepdims=True))
    p = e / jnp.sum(e, axis=-1, keepdims=True)
    o_cmp = jnp.dot(p.astype(BF16), vc, preferred_element_type=F32)

    row = lax.broadcasted_iota(jnp.int32, (NSA_HEADS, ncb), 0)
    lane = lax.broadcasted_iota(jnp.int32, (NSA_HEADS, ncb), 1)
    pg0 = jnp.sum(p[0:NSA_QPG], axis=0, keepdims=True)
    pg1 = jnp.sum(p[NSA_QPG:NSA_HEADS], axis=0, keepdims=True)
    imp = jnp.where(row < NSA_QPG, pg0, pg1)
    pair = imp + pltpu.roll(imp, ncb - 1, 1)
    blk = lane >> 1
    forced = (blk == 0) | (blk == ncb // 2 - 1)
    score = jnp.where((lane & 1) == 0, jnp.where(forced, jnp.inf, pair), NEG_INF)
    lane_f = lane.astype(F32)
    chosen = jnp.zeros((NSA_HEADS, ncb), F32)
    for _ in range(N_SEL - 1):
        mx = jnp.max(score, axis=-1, keepdims=True)
        first = jnp.min(jnp.where(score == mx, lane_f, float(ncb)), axis=-1, keepdims=True)
        hit = lane_f == first
        chosen = jnp.where(hit, 1.0, chosen)
        score = jnp.where(hit, NEG_INF, score)
    blockmask = jnp.where(chosen > 0.0, 0.0, MASK_NEG).astype(BF16)

    ks_t = buf[slot, 0:KV_WIDTH, :].astype(BF16)
    vs_t = buf[slot, KV_WIDTH:HALF_WIDTH, :].astype(BF16)
    s = (jnp.dot(q, ks_t, preferred_element_type=F32)
         + jnp.dot(blockmask, onehot_ref[...], preferred_element_type=F32) + sb_ref[...])
    s_new = jnp.sum(qf * bf_round(new_ref[0, 0:1, :]), axis=-1, keepdims=True) + rel0
    m = jnp.maximum(jnp.max(s, axis=-1, keepdims=True), s_new)
    e = jnp.exp(s - m)
    e_new = jnp.exp(s_new - m)
    den = jnp.sum(e, axis=-1, keepdims=True) + e_new
    o_sel = (_dot_nt(e.astype(BF16), vs_t) + e_new * bf_round(new_ref[0, 1:2, :])) / den

    wk_t = win_ref[0, 0, 0:KV_WIDTH, :].astype(BF16)
    wv_t = win_ref[0, 0, KV_WIDTH:HALF_WIDTH, :].astype(BF16)
    sw = jnp.dot(q, wk_t, preferred_element_type=F32) + wb_ref[...]
    sw_new = jnp.sum(qf * bf_round(new_ref[0, 2:3, :]), axis=-1, keepdims=True) + rel0
    mw = jnp.maximum(jnp.max(sw, axis=-1, keepdims=True), sw_new)
    ew = jnp.exp(sw - mw)
    ew_new = jnp.exp(sw_new - mw)
    denw = jnp.sum(ew, axis=-1, keepdims=True) + ew_new
    o_win = (_dot_nt(ew.astype(BF16), wv_t) + ew_new * bf_round(new_ref[0, 3:4, :])) / denw

    g = jax.nn.sigmoid(gate_ref[0])
    o_ref[0] = (g[:, 0:1] * o_cmp + g[:, 1:2] * o_sel + g[:, 2:3] * o_win) * hmask_ref[...]


def decode_tables(rel_bias, past):
    ncb = past // L_CMP
    end = jnp.arange(ncb, dtype=jnp.int32) * L_CMP + (L_CMP - 1)
    cb = rel_bias_lookup(rel_bias, past - end).T
    sb = rel_bias_lookup(rel_bias, past - jnp.arange(past, dtype=jnp.int32)).T
    dist = WINDOW - jnp.arange(WINDOW, dtype=jnp.int32)
    wb = jnp.where((dist < WINDOW)[None, :], rel_bias_lookup(rel_bias, dist).T, MASK_NEG)
    rel0 = jnp.broadcast_to(rel_bias[0][:, None], (NSA_HEADS, LANES))
    onehot = (2 * (jnp.arange(past, dtype=jnp.int32) // L_SEL)[None, :] == jnp.arange(ncb, dtype=jnp.int32)[:, None]).astype(BF16)
    hmask = (jnp.arange(KV_WIDTH, dtype=jnp.int32)[None, :] // HEAD_DIM == jnp.arange(NSA_HEADS, dtype=jnp.int32)[:, None] // NSA_QPG).astype(F32)
    return cb, sb, wb, rel0, onehot, hmask


def compress_block_weights(w_cmp):
    w = w_cmp.reshape(2, L_CMP, HEAD_DIM, HEAD_DIM).transpose(1, 0, 2, 3)
    z = jnp.zeros_like(w)
    return jnp.concatenate([jnp.concatenate([w, z], axis=3), jnp.concatenate([z, w], axis=3)], axis=2).astype(BF16)


def nsa_decode(nq, nkv, ngate, page_table, cache_nsa, cache_win, w_bd, tables, layer):
    bs = nq.shape[0]
    n_pages = page_table.shape[1]
    past = n_pages * PAGE_SIZE
    ncb = past // L_CMP
    cb, sb, wb, rel0, onehot, hmask = tables
    kcvc = decode_compress(page_table, cache_nsa, w_bd, layer)
    qh = nq.reshape(bs, NSA_HEADS, 1, HEAD_DIM) * SCALE
    own_group = jnp.arange(NSA_KV_HEADS)[None, None, :, None] == (jnp.arange(NSA_HEADS) // NSA_QPG)[None, :, None, None]
    qblk = (qh * own_group).reshape(bs, NSA_HEADS, KV_WIDTH).astype(BF16)
    gpad = jnp.pad(ngate.reshape(bs, NSA_HEADS, 3), ((0, 0), (0, 0), (0, LANES - 3)))
    newr = jnp.pad(nkv.reshape(bs, 6, KV_WIDTH)[:, 2:6], ((0, 0), (0, 4), (0, 0)))

    def const2(shape):
        return pl.BlockSpec(shape, lambda s, pt: (0, 0))

    out = pl.pallas_call(
        functools.partial(_decode_attend_kernel, layer=layer),
        out_shape=jax.ShapeDtypeStruct((bs, NSA_HEADS, KV_WIDTH), F32),
        grid_spec=pltpu.PrefetchScalarGridSpec(
            num_scalar_prefetch=1,
            grid=(bs,),
            in_specs=[pl.BlockSpec(memory_space=pl.ANY),
                      pl.BlockSpec((1, NSA_HEADS, KV_WIDTH), lambda s, pt: (s, 0, 0)),
                      pl.BlockSpec((1, NSA_HEADS, LANES), lambda s, pt: (s, 0, 0)),
                      pl.BlockSpec((1, ncb, HALF_WIDTH), lambda s, pt: (s, 0, 0)),
                      pl.BlockSpec((1, 8, KV_WIDTH), lambda s, pt: (s, 0, 0)),
                      pl.BlockSpec((1, 1, HALF_WIDTH, WINDOW), lambda s, pt: (layer, s, 0, 0)),
                      const2((NSA_HEADS, ncb)), const2((NSA_HEADS, past)), const2((NSA_HEADS, WINDOW)),
                      const2((NSA_HEADS, LANES)), const2((ncb, past)), const2((NSA_HEADS, KV_WIDTH))],
            out_specs=pl.BlockSpec((1, NSA_HEADS, KV_WIDTH), lambda s, pt: (s, 0, 0)),
            scratch_shapes=[pltpu.VMEM((2, HALF_WIDTH, past), F32), pltpu.SemaphoreType.DMA((2,))]),
        compiler_params=pltpu.CompilerParams(dimension_semantics=("arbitrary",), vmem_limit_bytes=VMEM_LIMIT),
        name="decode_attend",
    )(page_table, cache_nsa, qblk, gpad, kcvc, newr, cache_win, cb, sb, wb, rel0, onehot, hmask)
    return (out[..., :HEAD_DIM] + out[..., HEAD_DIM:]).reshape(bs, 1, NSA_HEADS * HEAD_DIM)


_PEER_CANDS = [(a, b) for a in range(PEER_TOPK) for b in range(PEER_TOPK) if (a + 1) * (b + 1) <= PEER_TOPK]
_PEER_NCAND = len(_PEER_CANDS)
_PEER_NCAND_PAD = -(-_PEER_NCAND // 8) * 8
_PEER_GROUP_START = [min(c for c, (a, _) in enumerate(_PEER_CANDS) if a == aa) for aa in range(PEER_TOPK)]
_PEER_GROUP_LEN = [sum(1 for (a, _) in _PEER_CANDS if a == aa) for aa in range(PEER_TOPK)]


def _peer_route_kernel(xT_ref, wqT_ref, keys_ref, r2_ref, beta_ref, alpha_ref, lam_ref,
                       qT_ref, s_ref, rk_ref, vals_ref, cand_ref, sel_ref):
    tl = xT_ref.shape[1]
    qT_ref[...] = jnp.dot(wqT_ref[...], xT_ref[...], preferred_element_type=F32).astype(BF16)
    iota_k = lax.broadcasted_iota(jnp.int32, (N_KEYS, tl), 0).astype(F32)

    def rank_top16(p, stable):
        cur = s_ref[p]
        rk = jnp.full((N_KEYS, tl), float(PEER_TOPK), F32)
        for a in range(PEER_TOPK):
            m = jnp.max(cur, axis=0, keepdims=True)
            if stable:
                idx = jnp.min(jnp.where(cur == m, iota_k, float(N_KEYS)), axis=0, keepdims=True)
                hit = iota_k == idx
            else:
                hit = cur == m
            rk = jnp.where(hit, float(a), rk)
            cur = jnp.where(hit, NEG_INF, cur)
            vals_ref[p, a:a + 1, :] = m
        rk_ref[p] = rk
        return jnp.sum(jnp.where(rk < float(PEER_TOPK), 1.0, 0.0), axis=0, keepdims=True)

    def head_body(h, carry):
        taken = jnp.zeros((1, tl), F32)
        for p in range(2):
            row0 = pl.multiple_of(h * PEER_DK + p * (PEER_DK // 2), PEER_DK // 2)
            qs = qT_ref[pl.ds(row0, PEER_DK // 2), :]
            s_ref[p] = jnp.dot(keys_ref[2 * h + p], qs, preferred_element_type=F32)
            taken = jnp.maximum(taken, rank_top16(p, stable=False))

        @pl.when(jnp.max(taken) > float(PEER_TOPK))
        def _():
            for p in range(2):
                rank_top16(p, stable=True)

        for c, (a, b) in enumerate(_PEER_CANDS):
            cand_ref[c:c + 1, :] = vals_ref[0, a:a + 1, :] + vals_ref[1, b:b + 1, :]
        if _PEER_NCAND_PAD > _PEER_NCAND:
            cand_ref[_PEER_NCAND:_PEER_NCAND_PAD, :] = jnp.full((_PEER_NCAND_PAD - _PEER_NCAND, tl), NEG_INF, F32)
        ngrp = _PEER_NCAND_PAD // 8
        iota8 = lax.broadcasted_iota(jnp.int32, (8, tl), 0)
        ranks = [jnp.zeros((8, tl), F32) for _ in range(ngrp)]
        for cp in range(_PEER_NCAND):
            rowb = cand_ref[cp:cp + 1, :]
            for k in range(ngrp):
                blk = cand_ref[8 * k:8 * k + 8, :]
                if 8 * k > cp:
                    inc = jnp.where(rowb >= blk, 1.0, 0.0)
                elif 8 * k + 7 < cp:
                    inc = jnp.where(rowb > blk, 1.0, 0.0)
                else:
                    inc = jnp.where(iota8 + 8 * k > cp, jnp.where(rowb >= blk, 1.0, 0.0), jnp.where(rowb > blk, 1.0, 0.0))
                ranks[k] = ranks[k] + inc
        top = cand_ref[0:1, :]
        z = jnp.zeros((1, tl), F32)
        for k in range(ngrp):
            blk = cand_ref[8 * k:8 * k + 8, :]
            selk = ranks[k] < float(PEER_TOPK)
            sel_ref[8 * k:8 * k + 8, :] = jnp.where(selk, 1.0, 0.0)
            z = z + jnp.sum(jnp.where(selk, jnp.exp(blk - top), 0.0), axis=0, keepdims=True)
        rk1 = rk_ref[0]
        lam = jnp.full((N_KEYS, tl), -1.0, F32)
        for a in range(PEER_TOPK):
            g0, gl = _PEER_GROUP_START[a], _PEER_GROUP_LEN[a]
            la = jnp.sum(sel_ref[g0:g0 + gl, :], axis=0, keepdims=True) - 1.0
            lam = jnp.where(rk1 == float(a), la, lam)
        alpha = jnp.where(rk1 < float(PEER_TOPK), jnp.exp(s_ref[0] - vals_ref[0, 0:1, :]), 0.0)
        rk2 = rk_ref[1]
        beta = jnp.where(rk2 < float(PEER_TOPK), jnp.exp(s_ref[1] - vals_ref[1, 0:1, :]), 0.0) / z
        r2_ref[h] = rk2.astype(BF16)
        beta_ref[h] = beta.astype(BF16)
        alpha_ref[h] = alpha
        lam_ref[h] = lam
        return carry

    lax.fori_loop(0, PEER_HEADS, head_body, 0)


def peer_route(xT, wqT, keys2):
    d, n = xT.shape
    tl = LANES
    assert n % tl == 0
    nq = PEER_HEADS * PEER_DK
    out_bf = jax.ShapeDtypeStruct((PEER_HEADS, N_KEYS, n), BF16)
    out_f = jax.ShapeDtypeStruct((PEER_HEADS, N_KEYS, n), F32)
    tab_spec = pl.BlockSpec((PEER_HEADS, N_KEYS, tl), lambda i: (0, 0, i))
    return pl.pallas_call(
        _peer_route_kernel,
        out_shape=(out_bf, out_bf, out_f, out_f),
        grid=(n // tl,),
        in_specs=[pl.BlockSpec((d, tl), lambda i: (0, i)),
                  pl.BlockSpec((nq, d), lambda i: (0, 0)),
                  pl.BlockSpec((2 * PEER_HEADS, N_KEYS, PEER_DK // 2), lambda i: (0, 0, 0))],
        out_specs=(tab_spec, tab_spec, tab_spec, tab_spec),
        scratch_shapes=[pltpu.VMEM((nq, tl), BF16),
                        pltpu.VMEM((2, N_KEYS, tl), F32),
                        pltpu.VMEM((2, N_KEYS, tl), F32),
                        pltpu.VMEM((2, PEER_TOPK, tl), F32),
                        pltpu.VMEM((_PEER_NCAND_PAD, tl), F32),
                        pltpu.VMEM((_PEER_NCAND_PAD, tl), F32)],
        compiler_params=pltpu.CompilerParams(dimension_semantics=("arbitrary",), vmem_limit_bytes=VMEM_LIMIT),
        name="peer_route",
    )(xT, wqT, keys2)


def _gelu_tanh(x):
    return 0.5 * x * (1.0 + jnp.tanh(math.sqrt(2.0 / math.pi) * (x + 0.044715 * (x * x * x))))


def _peer_dense_kernel(xT_ref, u_ref, vT_ref, r2_ref, beta_ref, alpha_ref, lam_ref, yT_ref, a_ref, h_ref):
    j = pl.program_id(1)
    te = u_ref.shape[0]
    tl = xT_ref.shape[1]

    @pl.when(j == 0)
    def _():
        yT_ref[...] = jnp.zeros_like(yT_ref)

    n_sub = te // N_KEYS

    def pre_activation(r):
        rows = slice(r * N_KEYS, (r + 1) * N_KEYS)
        a_ref[rows, :] = jnp.dot(u_ref[rows, :], xT_ref[...], preferred_element_type=F32)

    pre_activation(0)
    for r in range(n_sub):
        rows = slice(r * N_KEYS, (r + 1) * N_KEYS)
        if r + 1 < n_sub:
            pre_activation(r + 1)
        g = jnp.zeros((N_KEYS, tl), BF16)
        for h in range(PEER_HEADS):
            lam = lam_ref[h, r:r + 1, :].astype(BF16)
            alp = alpha_ref[h, r:r + 1, :].astype(BF16)
            g = g + jnp.where(r2_ref[h] <= lam, beta_ref[h], jnp.zeros((), BF16)) * alp
        h_ref[rows, :] = _gelu_tanh(a_ref[rows, :].astype(BF16)) * g
        if r % PEER_SUB_PER_DOT == PEER_SUB_PER_DOT - 1:
            cols = slice((r + 1 - PEER_SUB_PER_DOT) * N_KEYS, (r + 1) * N_KEYS)
            yT_ref[...] += jnp.dot(vT_ref[:, cols], h_ref[cols, :], preferred_element_type=F32)


def peer_dense(xT, u_bf, vT_bf, r2, beta, alpha, lam, tl, te):
    d, n = xT.shape
    e = u_bf.shape[0]
    assert n % tl == 0 and e % te == 0 and te % (8 * N_KEYS) == 0
    tab_spec = pl.BlockSpec((PEER_HEADS, N_KEYS, tl), lambda i, j: (0, 0, i))
    row_spec = pl.BlockSpec((PEER_HEADS, te // N_KEYS, tl), lambda i, j: (0, j, i))
    return pl.pallas_call(
        _peer_dense_kernel,
        out_shape=jax.ShapeDtypeStruct((d, n), F32),
        grid=(n // tl, e // te),
        in_specs=[pl.BlockSpec((d, tl), lambda i, j: (0, i)),
                  pl.BlockSpec((te, d), lambda i, j: (j, 0)),
                  pl.BlockSpec((d, te), lambda i, j: (0, j)),
                  tab_spec, tab_spec, row_spec, row_spec],
        out_specs=pl.BlockSpec((d, tl), lambda i, j: (0, i)),
        scratch_shapes=[pltpu.VMEM((te, tl), F32), pltpu.VMEM((te, tl), BF16)],
        compiler_params=pltpu.CompilerParams(dimension_semantics=("arbitrary", "arbitrary"), vmem_limit_bytes=VMEM_LIMIT),
        name="peer_dense",
    )(xT, u_bf, vT_bf, r2, beta, alpha, lam)


def peer_ffn_t(xT, peer_w):
    wqT, keys2, u_bf, vT_bf = peer_w
    n = xT.shape[1]
    tl = PEER_TL if n % PEER_TL == 0 else LANES
    r2, beta, alpha, lam = peer_route(xT, wqT, keys2)
    return peer_dense(xT, u_bf, vT_bf, r2, beta, alpha, lam, tl, PEER_TE)


def peer_ffn(h, peer_w):
    b, t, d = h.shape
    n = b * t
    npad = -(-n // LANES) * LANES
    xT = jnp.pad(h.reshape(n, d).astype(BF16).T, ((0, 0), (0, npad - n)))
    return peer_ffn_t(xT, peer_w).T[:n].reshape(b, t, d)


POST_TM = 512
_POST_ROWS = 8


def _post_kernel(x_ref, y_ref, vec_ref, ox_ref, *oh_ref, alpha, y_transposed, h_transposed):
    y = y_ref[...]
    if y_transposed:
        y = y.T
    vec = vec_ref[0]
    r = alpha * x_ref[...] + vec[0:1] * y
    mu = jnp.mean(r, axis=-1, keepdims=True)
    rc = r - mu
    var = jnp.mean(rc * rc, axis=-1, keepdims=True)
    xn = rc * lax.rsqrt(var + LN_EPS) * vec[3:4] + vec[4:5]
    ox_ref[...] = xn
    if oh_ref:
        hm = xn * vec[1:2] + vec[2:3]
        oh_ref[0][...] = (hm.T if h_transposed else hm).astype(BF16)


def residual_norm_modulate(x, y, gate, ln_g, ln_b, scale, shift, alpha, tokens_per_batch, y_transposed, h_transposed):
    n, d = x.shape
    tm = POST_TM
    assert n % tm == 0 and tokens_per_batch % tm == 0
    nb = gate.shape[0]
    want_h = scale is not None
    if not want_h:
        scale = shift = jnp.zeros_like(gate)
    vec = jnp.stack([gate, scale, shift, jnp.broadcast_to(ln_g, (nb, d)), jnp.broadcast_to(ln_b, (nb, d))], axis=1)
    vec = jnp.pad(vec, ((0, 0), (0, _POST_ROWS - vec.shape[1]), (0, 0)))
    tiles_per_batch = tokens_per_batch // tm
    y_spec = pl.BlockSpec((d, tm), lambda i: (0, i)) if y_transposed else pl.BlockSpec((tm, d), lambda i: (i, 0))
    out_shape = [jax.ShapeDtypeStruct((n, d), F32)]
    out_specs = [pl.BlockSpec((tm, d), lambda i: (i, 0))]
    if want_h:
        out_shape.append(jax.ShapeDtypeStruct((d, n) if h_transposed else (n, d), BF16))
        out_specs.append(pl.BlockSpec((d, tm), lambda i: (0, i)) if h_transposed else pl.BlockSpec((tm, d), lambda i: (i, 0)))
    outs = pl.pallas_call(
        functools.partial(_post_kernel, alpha=alpha, y_transposed=y_transposed, h_transposed=h_transposed),
        out_shape=tuple(out_shape),
        grid=(n // tm,),
        in_specs=[pl.BlockSpec((tm, d), lambda i: (i, 0)), y_spec,
                  pl.BlockSpec((1, _POST_ROWS, d), lambda i: (i // tiles_per_batch, 0, 0))],
        out_specs=tuple(out_specs),
        compiler_params=pltpu.CompilerParams(dimension_semantics=("arbitrary",), vmem_limit_bytes=VMEM_LIMIT),
        name="residual_norm_modulate",
    )(x, y, vec)
    return (outs[0], outs[1]) if want_h else (outs[0], None)


def token_mixers(parts, pos, rel_table, w_cmp, conv_w, conv_b, ret_g, past, nsa_tables):
    nq, nkv, ngate, rq, rk, rv, rg, cb, cc, ch = parts
    b, t = nq.shape[:2]
    q = nq.reshape(b, t, NSA_KV_HEADS, NSA_QPG, HEAD_DIM)
    kv = nkv.reshape(b, t, 6, NSA_KV_HEADS, HEAD_DIM)
    if past is None:
        kc = compress(kv[:, :, 0], w_cmp[0])
        vc = compress(kv[:, :, 1], w_cmp[1])
        o_nsa = nsa_prompt(nq, kv, ngate, kc, vc, nsa_tables)
        win_rows = kv[:, t - min(WINDOW, t):, 4:]
        zbuf = jnp.zeros((b, CONV_W - 1, CONV_CH), ch.dtype)
        o_ret, s_new = retention_prompt(rq, rk, rv, rg, ret_g)
    else:
        decode_nsa, win_buf, s0, zbuf = past
        o_nsa = decode_nsa(nq, nkv, ngate)
        wrows = jnp.concatenate([win_buf.astype(kv.dtype), kv[:, :, 4:]], axis=1)
        win_rows = wrows[:, wrows.shape[1] - min(WINDOW, wrows.shape[1]):]
        rqh = rotary(rq.reshape(b, t, RET_HEADS, RET_DK), pos)
        rkh = rotary(rk.reshape(b, t, RET_HEADS, RET_DK), pos) * (RET_DK ** -0.5)
        rvh = rv.reshape(b, t, RET_HEADS, RET_DV)
        o_r, s_new = retention(rqh, rkh, rvh, s0, t)
        mu = jnp.mean(o_r, -1, keepdims=True)
        var = jnp.mean(jnp.square(o_r - mu), -1, keepdims=True)
        on = (o_r - mu) * lax.rsqrt(var + LN_EPS) * ret_g.reshape(RET_HEADS, RET_DV)
        o_ret = (on.reshape(b, t, RET_WIDTH) * jax.nn.silu(rg.astype(jnp.float32))).astype(nq.dtype)
    z = cc * ch
    zp = jnp.concatenate([zbuf.astype(z.dtype), z], axis=1)
    y = conv_b + sum(zp[:, j:j + t] * conv_w[j] for j in range(CONV_W))
    o_conv = (cb * y).astype(nq.dtype)
    mix = jnp.concatenate([o_nsa, o_ret, o_conv], axis=-1)
    return mix, (kv[:, :, :4], win_rows, s_new, zp[:, t:])


def kernel(x_prompt, x_sample, cache_nsa_kv, cache_win_kv, state_ret, state_conv, page_table, c_prompt, c_sample, rel_bias, w_ada, b_ada, w_in, w_cmp, conv_w, conv_b, ret_norm_g, w_out, ln1_g, ln1_b, ln2_g, ln2_b, peer_wq, peer_keys, peer_u, peer_v):
    alpha = (2.0 * DEPTH) ** 0.25
    n_pages = page_table.shape[1]
    past_len = n_pages * PAGE_SIZE
    sp = x_prompt.shape[1]
    bs, ts = x_sample.shape[:2]
    pos_p = jnp.arange(sp, dtype=jnp.int32)
    pos_s = past_len + jnp.arange(ts, dtype=jnp.int32)
    nsa_tables = nsa_bias_tables(rel_bias, sp, NSA_TQ)
    dec_tables = decode_tables(rel_bias, past_len)
    cache_nsa = cache_nsa_kv.transpose(0, 1, 3, 4, 5, 2).reshape(cache_nsa_kv.shape[0], cache_nsa_kv.shape[1], ROW_WIDTH, PAGE_SIZE)
    cache_win = cache_win_kv.transpose(0, 1, 3, 4, 5, 2).reshape(cache_win_kv.shape[0], bs, HALF_WIDTH, cache_win_kv.shape[2])

    def ada(c, l):
        return (jax.nn.silu(c) @ w_ada[l] + b_ada[l]).reshape(c.shape[0], 6, D_MODEL)

    def run_layer(x, c, l, pos, past, peer_w):
        m = ada(c, l)[:, :, None, :]
        h = x * (1.0 + m[:, 1]) + m[:, 0]
        parts = split_proj(h @ w_in[l])
        mix, st = token_mixers(parts, pos, rel_bias, w_cmp[l], conv_w[l], conv_b[l], ret_norm_g[l], past, nsa_tables)
        x = layer_norm(alpha * x + (1.0 + m[:, 2]) * (mix @ w_out[l]), ln1_g[l], ln1_b[l])
        h = x * (1.0 + m[:, 4]) + m[:, 3]
        y = peer_ffn(h, peer_w)
        x = layer_norm(alpha * x + (1.0 + m[:, 5]) * y, ln2_g[l], ln2_b[l])
        return x, st

    bp = x_prompt.shape[0]
    m_prompt = [ada(c_prompt, l) for l in range(DEPTH)]

    def run_prompt_layer(x2, h, l, peer_w):
        m = m_prompt[l]
        parts = split_proj(jnp.matmul(h, w_in[l].astype(h.dtype), preferred_element_type=F32))
        mix, st = token_mixers(parts, pos_p, rel_bias, w_cmp[l], conv_w[l], conv_b[l], ret_norm_g[l], None, nsa_tables)
        y = (mix @ w_out[l]).reshape(bp * sp, D_MODEL)
        x2, h_t = residual_norm_modulate(x2, y, 1.0 + m[:, 2], ln1_g[l], ln1_b[l], 1.0 + m[:, 4], m[:, 3], alpha, sp,
                                         y_transposed=False, h_transposed=True)
        y_t = peer_ffn_t(h_t, peer_w)
        if l + 1 < DEPTH:
            nxt = m_prompt[l + 1]
            x2, h = residual_norm_modulate(x2, y_t, 1.0 + m[:, 5], ln2_g[l], ln2_b[l], 1.0 + nxt[:, 1], nxt[:, 0], alpha, sp,
                                           y_transposed=True, h_transposed=False)
            h = h.reshape(bp, sp, D_MODEL)
        else:
            x2, h = residual_norm_modulate(x2, y_t, 1.0 + m[:, 5], ln2_g[l], ln2_b[l], None, None, alpha, sp,
                                           y_transposed=True, h_transposed=False)
        return x2, h, st

    xp2 = x_prompt.reshape(bp * sp, D_MODEL)
    hp = x_prompt * (1.0 + m_prompt[0][:, 1][:, None, :]) + m_prompt[0][:, 0][:, None, :]
    xs = x_sample
    nkv_p, nkv_s, win_p, win_s, ret_p, ret_s, conv_p, conv_s = [], [], [], [], [], [], [], []
    for l in range(DEPTH):
        peer_w = (peer_wq[l].T.astype(BF16),
                  peer_keys[l].reshape(2 * PEER_HEADS, N_KEYS, PEER_DK // 2).astype(BF16),
                  peer_u[l].astype(BF16),
                  peer_v[l].T.astype(BF16))
        xp2, hp, st_p = run_prompt_layer(xp2, hp, l, peer_w)
        decode_nsa = functools.partial(nsa_decode, page_table=page_table, cache_nsa=cache_nsa, cache_win=cache_win,
                                       w_bd=compress_block_weights(w_cmp[l]), tables=dec_tables, layer=l)
        xs, st_s = run_layer(xs, c_sample, l, pos_s, (decode_nsa, cache_win_kv[l], state_ret[l], state_conv[l]), peer_w)
        nkv_p.append(st_p[0])
        win_p.append(st_p[1])
        ret_p.append(st_p[2])
        conv_p.append(st_p[3])
        nkv_s.append(st_s[0])
        win_s.append(st_s[1])
        ret_s.append(st_s[2])
        conv_s.append(st_s[3])
    xp = xp2.reshape(bp, sp, D_MODEL)
    return (xp, xs, jnp.stack(nkv_p), jnp.stack(nkv_s), jnp.stack(win_p), jnp.stack(win_s), jnp.stack(ret_p), jnp.stack(ret_s), jnp.stack(conv_p), jnp.stack(conv_s))
```

```python
import math
import functools
import jax
import jax.numpy as jnp
from jax import lax
import numpy as np
from jax.experimental import pallas as pl
from jax.experimental.pallas import tpu as pltpu

D_MODEL = 1024
BATCH = 4
SEQ = 4096
DEPTH = 2
DEC_BATCH = 32
DEC_SEQ = 1
PAST_LEN = 8192
PAGE_SIZE = 128

HEAD_DIM = 64
NSA_WIDTH = D_MODEL // 2
NSA_HEADS = NSA_WIDTH // HEAD_DIM
NSA_KV_HEADS = 2
NSA_QPG = NSA_HEADS // NSA_KV_HEADS
KV_WIDTH = NSA_KV_HEADS * HEAD_DIM
SCALE = HEAD_DIM ** -0.5
L_CMP = 32
L_SEL = 64
N_SEL = 16
WINDOW = 512
Q_BLOCK = 128
SEL_Q_BLOCK = 64
N_BUCKETS = 32
REL_MAX_DIST = 128
RET_WIDTH = D_MODEL // 4
RET_DK = 64
RET_DV = 64
RET_HEADS = RET_WIDTH // RET_DV
RET_CHUNK = 128
ROPE_BASE = 10000.0
CONV_CH = D_MODEL // 4
CONV_W = 3
MIX_WIDTH = NSA_WIDTH + RET_WIDTH + CONV_CH
PEER_HEADS = 8
PEER_DK = 256
N_KEYS = 128
N_EXPERTS = N_KEYS * N_KEYS
PEER_TOPK = 16
PEER_CHUNK = 256
LN_EPS = 1e-5
SPLIT_SIZES = (NSA_WIDTH, 6 * KV_WIDTH, 3 * NSA_HEADS, RET_HEADS * RET_DK, RET_HEADS * RET_DK, RET_WIDTH, RET_WIDTH, CONV_CH, CONV_CH, CONV_CH)
N_IN = sum(SPLIT_SIZES)

F32 = jnp.float32
BF16 = jnp.bfloat16
NEG_INF = float('-inf')
MASK_NEG = -1e9
CMP_MASK_NEG = -1e30
NSA_TQ = 256
NSA_RB = 128
NSA_FAR_UNROLL = 4
ROW_WIDTH = 4 * KV_WIDTH
HALF_WIDTH = 2 * KV_WIDTH
PEER_TL = 512
PEER_TE = 1024
PEER_ROUTE_TL = 256
PEER_SUB_PER_DOT = 2
LANES = 128
VMEM_LIMIT = 56 * 1024 * 1024


def _ln_kernel(x_ref, g_ref, b_ref, o_ref):
    x = x_ref[...]
    mu = jnp.mean(x, -1, keepdims=True)
    xc = x - mu
    var = jnp.mean(xc * xc, -1, keepdims=True)
    o_ref[...] = xc * lax.rsqrt(var + LN_EPS) * g_ref[...] + b_ref[...]


def layer_norm(x, g, b):
    shp = x.shape
    x2 = x.reshape(-1, shp[-1])
    n = x2.shape[0]
    tm = min(n, 512)
    out = pl.pallas_call(
        _ln_kernel,
        out_shape=jax.ShapeDtypeStruct(x2.shape, jnp.float32),
        grid=(n // tm,),
        in_specs=[pl.BlockSpec((tm, shp[-1]), lambda i: (i, 0)),
                  pl.BlockSpec((1, shp[-1]), lambda i: (0, 0)),
                  pl.BlockSpec((1, shp[-1]), lambda i: (0, 0))],
        out_specs=pl.BlockSpec((tm, shp[-1]), lambda i: (i, 0)),
        name="layer_norm",
    )(x2, g.reshape(1, -1), b.reshape(1, -1))
    return out.reshape(shp)


def t5_bucket(dist):
    n = jnp.maximum(dist, 0)
    exact = N_BUCKETS // 2
    nf = jnp.maximum(n, exact).astype(jnp.float32)
    big = exact + (jnp.log(nf / exact) / math.log(REL_MAX_DIST / exact) * (N_BUCKETS - exact)).astype(jnp.int32)
    return jnp.where(n < exact, n, jnp.minimum(big, N_BUCKETS - 1))


def rel_bias_lookup(rel_bias, dist):
    onehot = jax.nn.one_hot(t5_bucket(dist), N_BUCKETS, dtype=jnp.float32)
    return jnp.einsum('...k,kh->...h', onehot, rel_bias, precision=lax.Precision.HIGHEST)


def rotary(x, pos):
    half = x.shape[-1] // 2
    inv = ROPE_BASE ** (-jnp.arange(half, dtype=jnp.float32) / half)
    ang = pos.astype(jnp.float32)[:, None] * inv[None, :]
    cos = jnp.cos(ang)[None, :, None, :]
    sin = jnp.sin(ang)[None, :, None, :]
    x1, x2 = x[..., :half], x[..., half:]
    return jnp.concatenate([x1 * cos - x2 * sin, x1 * sin + x2 * cos], axis=-1)


def split_proj(p):
    outs, start = [], 0
    for size in SPLIT_SIZES:
        outs.append(p[..., start:start + size])
        start += size
    return outs


def compress(rows, w):
    b, lp, g, d = rows.shape
    blk = rows.reshape(b, lp // L_CMP, L_CMP, g, d).transpose(0, 1, 3, 2, 4).reshape(b, lp // L_CMP, g, L_CMP * d)
    return blk @ w


def retention(q, k, v, s0, chunk):
    b, t = q.shape[:2]
    nc = t // chunk
    lg = jnp.log(1.0 - 2.0 ** (-5.0 - jnp.arange(RET_HEADS, dtype=jnp.float32)))
    i = jnp.arange(chunk, dtype=jnp.float32)
    diff = i[:, None] - i[None, :]
    dmat = jnp.where(diff >= 0, jnp.exp(jnp.maximum(diff, 0.0)[None] * lg[:, None, None]), 0.0)
    q_dec = jnp.exp((i + 1.0)[:, None] * lg[None, :])[None, :, :, None]
    k_dec = jnp.exp((chunk - 1.0 - i)[:, None] * lg[None, :])[None, :, :, None]
    s_dec = jnp.exp(chunk * lg)[None, :, None, None]

    def to_chunks(a):
        return jnp.moveaxis(a.astype(jnp.float32).reshape(b, nc, chunk, RET_HEADS, a.shape[-1]), 1, 0)

    def step(s, xs):
        qc, kc, vc = xs
        att = jnp.einsum('bihd,bjhd->bhij', qc, kc) * dmat
        o = jnp.einsum('bhij,bjhe->bihe', att, vc) + jnp.einsum('bihd,bhde->bihe', qc * q_dec, s)
        s = s * s_dec + jnp.einsum('bjhd,bjhe->bhde', kc * k_dec, vc)
        return s, o

    s, o = lax.scan(step, s0.astype(jnp.float32), (to_chunks(q), to_chunks(k), to_chunks(v)))
    return jnp.moveaxis(o, 0, 1).reshape(b, t, RET_HEADS, RET_DV), s


def _retention_kernel(q_ref, k_ref, v_ref, g_ref, cos_ref, sin_ref, dmat_ref, qdec_ref, kdec_ref, sdec_ref, gain_ref,
                      o_ref, sout_ref, s_scr):
    c = pl.program_id(1)

    @pl.when(c == 0)
    def _():
        s_scr[...] = jnp.zeros_like(s_scr)

    chunk, width = q_ref.shape[1], q_ref.shape[2]
    half = RET_DK // 2
    lane = lax.broadcasted_iota(jnp.int32, (chunk, width), 1)
    first_half = (lane % RET_DK) < half

    def rot(x):
        swapped = jnp.where(first_half, pltpu.roll(x, width - half, 1), pltpu.roll(x, half, 1))
        return x * cos_ref[...] + swapped * sin_ref[...]

    q = rot(q_ref[0])
    k = rot(k_ref[0]) * (RET_DK ** -0.5)
    v = v_ref[0]
    gate = g_ref[0]
    eye = (lax.broadcasted_iota(jnp.int32, (RET_DK, RET_DK), 0)
           == lax.broadcasted_iota(jnp.int32, (RET_DK, RET_DK), 1)).astype(F32).astype(BF16)
    for h in range(RET_HEADS):
        sl = slice(h * RET_DK, (h + 1) * RET_DK)
        qh, kh, vh = q[:, sl], k[:, sl], v[:, sl].astype(BF16)
        att = _dot_nt(qh.astype(BF16), kh.astype(BF16)) * dmat_ref[h]
        s_old = s_scr[h]
        o = (jnp.dot(att.astype(BF16), vh, preferred_element_type=F32)
             + jnp.dot((qh * qdec_ref[:, sl]).astype(BF16), s_old.astype(BF16), preferred_element_type=F32))
        kd_t = _dot_nt(eye, (kh * kdec_ref[:, sl]).astype(BF16)).astype(BF16)
        s_scr[h] = s_old * sdec_ref[h, 0:1, 0:RET_DV] + jnp.dot(kd_t, vh, preferred_element_type=F32)
        mu = jnp.mean(o, axis=-1, keepdims=True)
        oc = o - mu
        var = jnp.mean(oc * oc, axis=-1, keepdims=True)
        gh = gate[:, sl]
        o_ref[0, :, sl] = oc * lax.rsqrt(var + LN_EPS) * gain_ref[:, sl] * (gh * jax.nn.sigmoid(gh))
    sout_ref[0] = s_scr[...]


def retention_prompt(rq, rk, rv, rg, ret_g):
    b, t, width = rq.shape
    chunk = RET_CHUNK
    half = RET_DK // 2
    pos = jnp.arange(t, dtype=jnp.float32)
    inv = ROPE_BASE ** (-jnp.arange(half, dtype=jnp.float32) / half)
    ang = pos[:, None] * inv[None, :]
    cos = jnp.tile(jnp.cos(ang), (1, 2 * RET_HEADS))
    sin = jnp.tile(jnp.concatenate([-jnp.sin(ang), jnp.sin(ang)], axis=1), (1, RET_HEADS))
    lg = jnp.log(1.0 - 2.0 ** (-5.0 - jnp.arange(RET_HEADS, dtype=jnp.float32)))
    i = jnp.arange(chunk, dtype=jnp.float32)
    diff = i[:, None] - i[None, :]
    dmat = jnp.where(diff >= 0, jnp.exp(jnp.maximum(diff, 0.0)[None] * lg[:, None, None]), 0.0)
    qdec = jnp.repeat(jnp.exp((i + 1.0)[:, None] * lg[None, :]), RET_DK, axis=1)
    kdec = jnp.repeat(jnp.exp((chunk - 1.0 - i)[:, None] * lg[None, :]), RET_DK, axis=1)
    sdec = jnp.broadcast_to(jnp.exp(chunk * lg)[:, None, None], (RET_HEADS, 8, LANES))
    tok = pl.BlockSpec((1, chunk, width), lambda bi, ci: (bi, ci, 0))
    tab = pl.BlockSpec((chunk, width), lambda bi, ci: (ci, 0))

    def const(shape):
        return pl.BlockSpec(shape, lambda bi, ci: (0,) * len(shape))

    return pl.pallas_call(
        _retention_kernel,
        out_shape=(jax.ShapeDtypeStruct((b, t, width), F32),
                   jax.ShapeDtypeStruct((b, RET_HEADS, RET_DK, RET_DV), F32)),
        grid=(b, t // chunk),
        in_specs=[tok, tok, tok, tok, tab, tab, const((RET_HEADS, chunk, chunk)), const((chunk, width)),
                  const((chunk, width)), const((RET_HEADS, 8, LANES)), const((1, width))],
        out_specs=(tok, pl.BlockSpec((1, RET_HEADS, RET_DK, RET_DV), lambda bi, ci: (bi, 0, 0, 0))),
        scratch_shapes=[pltpu.VMEM((RET_HEADS, RET_DK, RET_DV), F32)],
        compiler_params=pltpu.CompilerParams(dimension_semantics=("arbitrary", "arbitrary"), vmem_limit_bytes=VMEM_LIMIT),
        name="retention_prompt",
    )(rq, rk, rv, rg, cos, sin, dmat, qdec, kdec, sdec, ret_g.reshape(1, width))


def _nsa_prompt_kernel(q_ref, gate_ref, kcT_ref, vc_ref, cb_ref, ksT_ref, vs_ref, kwT_ref, vw_ref, sb_ref, wb_ref,
                       o_ref, qa_ref, m_ref, acc_ref, oc_ref, osel_ref, owin_ref):
    qi = pl.program_id(2)
    tq = q_ref.shape[3]
    ncb = kcT_ref.shape[3]
    ka = ksT_ref.shape[2]
    t0 = qi * tq
    rows = NSA_QPG * tq

    imp = jnp.zeros((tq, ncb), F32)
    for h in range(NSA_QPG):
        bias = cb_ref[0, h]
        lg = jnp.dot(q_ref[0, 0, h], kcT_ref[0, 0], preferred_element_type=F32) + bias
        mx = jnp.max(lg, axis=-1, keepdims=True)
        e = jnp.where(bias > 0.5 * CMP_MASK_NEG, jnp.exp(lg - mx), 0.0)
        p = e / jnp.maximum(jnp.sum(e, axis=-1, keepdims=True), 1e-30)
        imp = imp + p
        oc_ref[h] = jnp.dot(p.astype(BF16), vc_ref[0, 0], preferred_element_type=F32)

    lane = lax.broadcasted_iota(jnp.int32, (tq, ncb), 1)
    tpos = t0 + lax.broadcasted_iota(jnp.int32, (tq, ncb), 0)
    pair = imp + pltpu.roll(imp, ncb - 1, 1)
    blk = lane >> 1
    cur = tpos >> 6
    forced = (blk == 0) | (blk == cur) | (blk == cur - 1)
    cand = ((lane & 1) == 0) & (blk <= cur)
    score = jnp.where(cand, jnp.where(forced, jnp.inf, pair), NEG_INF)
    lane_f = lane.astype(F32)
    chosen = jnp.zeros((tq, ncb), F32)
    for _ in range(N_SEL):
        mx = jnp.max(score, axis=-1, keepdims=True)
        first = jnp.min(jnp.where(score == mx, lane_f, float(ncb)), axis=-1, keepdims=True)
        hit = (lane_f == first) & (mx > NEG_INF)
        chosen = jnp.where(hit, 1.0, chosen)
        score = jnp.where(hit, NEG_INF, score)
    blockmask = jnp.where(chosen > 0.0, 0.0, MASK_NEG).astype(BF16)

    for h in range(NSA_QPG):
        qa_ref[h * tq:(h + 1) * tq, 0:ncb] = blockmask
        qa_ref[h * tq:(h + 1) * tq, ncb:ncb + HEAD_DIM] = q_ref[0, 0, h]
        if ka > ncb + HEAD_DIM:
            qa_ref[h * tq:(h + 1) * tq, ncb + HEAD_DIM:ka] = jnp.zeros((tq, ka - ncb - HEAD_DIM), BF16)

    nrb = rows // NSA_RB
    per_head = tq // NSA_RB

    def reset():
        m_ref[...] = jnp.full((rows, LANES), CMP_MASK_NEG, F32)
        acc_ref[...] = jnp.zeros((rows, LANES), F32)

    def attn_step(rb, qrb, k_t, v, bias):
        r0 = rb * NSA_RB
        s = jnp.dot(qrb, k_t, preferred_element_type=F32)
        if bias is not None:
            s = s + bias
        parts = [s[:, i * LANES:(i + 1) * LANES] for i in range(s.shape[1] // LANES)]
        red = parts[0]
        for part in parts[1:]:
            red = jnp.maximum(red, part)
        m_old = m_ref[r0:r0 + NSA_RB, :]
        m_new = jnp.maximum(m_old, jnp.max(red, axis=-1, keepdims=True))
        p = jnp.concatenate([jnp.exp(part - m_new) for part in parts], axis=1).astype(BF16)
        acc_ref[r0:r0 + NSA_RB, :] = (jnp.exp(m_old - m_new) * acc_ref[r0:r0 + NSA_RB, :]
                                      + jnp.dot(p, v, preferred_element_type=F32))
        m_ref[r0:r0 + NSA_RB, :] = m_new

    def finish(dst_ref):
        acc = acc_ref[...]
        dst_ref[...] = acc[:, 0:HEAD_DIM] / acc[:, HEAD_DIM:HEAD_DIM + 1]

    reset()

    def far_chunk(c):
        col = pl.multiple_of((c + 1) * tq, tq)
        k_t = ksT_ref[0, 0, :, pl.ds(col, tq)]
        v = vs_ref[0, 0, pl.ds(col, tq), :]
        for rb in range(nrb):
            attn_step(rb, qa_ref[rb * NSA_RB:(rb + 1) * NSA_RB, :], k_t, v, None)

    def far_group(c, carry):
        for i in range(NSA_FAR_UNROLL):
            far_chunk(NSA_FAR_UNROLL * c + i)
        return carry

    n_far = jnp.maximum(qi - 1, 0)
    lax.fori_loop(0, n_far // NSA_FAR_UNROLL, far_group, 0)
    done = (n_far // NSA_FAR_UNROLL) * NSA_FAR_UNROLL
    width = NSA_FAR_UNROLL // 2
    while width >= 1:
        @pl.when(((n_far - done) & width) != 0)
        def _(done=done, width=width):
            for i in range(width):
                far_chunk(done + i)
        done = done + ((n_far - done) & width)
        width //= 2

    col0 = pl.multiple_of(t0, tq)
    for c in range(2):
        colc = pl.multiple_of(col0 + c * tq, tq)
        k_t = ksT_ref[0, 0, :, pl.ds(colc, tq)]
        v = vs_ref[0, 0, pl.ds(colc, tq), :]
        jn = lax.broadcasted_iota(jnp.int32, (1, tq), 1) + c * tq
        colmask = jnp.where(jn + (t0 - tq) >= 0, 0.0, MASK_NEG)
        for rb in range(nrb):
            h, part = rb // per_head, rb % per_head
            bias = sb_ref[0, h, part * NSA_RB:(part + 1) * NSA_RB, c * tq:(c + 1) * tq] + colmask
            attn_step(rb, qa_ref[rb * NSA_RB:(rb + 1) * NSA_RB, :], k_t, v, bias)
    finish(osel_ref)

    reset()
    for c in range((WINDOW + tq) // tq):
        colc = pl.multiple_of(col0 + c * tq, tq)
        k_t = kwT_ref[0, 0, :, pl.ds(colc, tq)]
        v = vw_ref[0, 0, pl.ds(colc, tq), :]
        jw = lax.broadcasted_iota(jnp.int32, (1, tq), 1) + c * tq
        wmask = jnp.where(jw + (t0 - WINDOW) >= 0, 0.0, MASK_NEG)
        for rb in range(nrb):
            h, part = rb // per_head, rb % per_head
            bias = wb_ref[0, h, part * NSA_RB:(part + 1) * NSA_RB, c * tq:(c + 1) * tq] + wmask
            attn_step(rb, qa_ref[rb * NSA_RB:(rb + 1) * NSA_RB, ncb:ncb + HEAD_DIM], k_t, v, bias)
    finish(owin_ref)

    g = jax.nn.sigmoid(gate_ref[0, 0])
    for h in range(NSA_QPG):
        o_h = (g[:, 3 * h:3 * h + 1] * oc_ref[h]
               + g[:, 3 * h + 1:3 * h + 2] * osel_ref[h * tq:(h + 1) * tq, :]
               + g[:, 3 * h + 2:3 * h + 3] * owin_ref[h * tq:(h + 1) * tq, :])
        o_ref[0, :, h * HEAD_DIM:(h + 1) * HEAD_DIM] = o_h


def nsa_bias_tables(rel_bias, t, tq):
    ncb = t // L_CMP

    def heads_first(tab):
        return tab.transpose(2, 0, 1).reshape(NSA_KV_HEADS, NSA_QPG, tab.shape[0], tab.shape[1])

    pos = jnp.arange(t, dtype=jnp.int32)
    end = jnp.arange(ncb, dtype=jnp.int32) * L_CMP + (L_CMP - 1)
    dist = pos[:, None] - end[None, :]
    cb = jnp.where((dist >= 0)[..., None], rel_bias_lookup(rel_bias, dist), CMP_MASK_NEG)
    i = jnp.arange(tq, dtype=jnp.int32)
    dist = i[:, None] + tq - jnp.arange(2 * tq, dtype=jnp.int32)[None, :]
    sb = jnp.where((dist >= 0)[..., None], rel_bias_lookup(rel_bias, dist) - rel_bias[N_BUCKETS - 1], MASK_NEG)
    dist = i[:, None] + WINDOW - jnp.arange(WINDOW + tq, dtype=jnp.int32)[None, :]
    wb = jnp.where(((dist >= 0) & (dist < WINDOW))[..., None], rel_bias_lookup(rel_bias, dist), MASK_NEG)
    return heads_first(cb), heads_first(sb), heads_first(wb)


def nsa_prompt(q, kv, gates, kc, vc, tables):
    b, t = q.shape[:2]
    tq = NSA_TQ
    ncb = t // L_CMP
    ka = -(-(ncb + HEAD_DIM) // LANES) * LANES
    cb, sb, wb = tables
    q4 = (q * SCALE).astype(BF16).reshape(b, t, NSA_KV_HEADS, NSA_QPG, HEAD_DIM).transpose(0, 2, 3, 1, 4)
    g4 = gates.reshape(b, t, NSA_KV_HEADS, NSA_QPG * 3).transpose(0, 2, 1, 3)
    kcT = kc.astype(BF16).transpose(0, 2, 3, 1)
    vcg = vc.astype(BF16).transpose(0, 2, 1, 3)
    kvb = kv.astype(BF16)
    onehot = (2 * (jnp.arange(t, dtype=jnp.int32) // L_SEL)[None, :] == jnp.arange(ncb, dtype=jnp.int32)[:, None]).astype(BF16)
    ks_t = kvb[:, :, 2].transpose(0, 2, 3, 1)
    ksT = jnp.concatenate([jnp.broadcast_to(onehot, (b, NSA_KV_HEADS, ncb, t)), ks_t,
                           jnp.zeros((b, NSA_KV_HEADS, ka - ncb - HEAD_DIM, t), BF16)], axis=2)
    ksT = jnp.pad(ksT, ((0, 0), (0, 0), (0, 0), (tq, 0)))
    def with_ones(v):
        one = jnp.ones(v.shape[:-1] + (1,), BF16)
        return jnp.concatenate([v, one, jnp.zeros(v.shape[:-1] + (LANES - HEAD_DIM - 1,), BF16)], axis=-1)

    vs = jnp.pad(with_ones(kvb[:, :, 3].transpose(0, 2, 1, 3)), ((0, 0), (0, 0), (tq, 0), (0, 0)))
    kwT = jnp.pad(kvb[:, :, 4].transpose(0, 2, 3, 1), ((0, 0), (0, 0), (0, 0), (WINDOW, 0)))
    vw = jnp.pad(with_ones(kvb[:, :, 5].transpose(0, 2, 1, 3)), ((0, 0), (0, 0), (WINDOW, 0), (0, 0)))
    rows = NSA_QPG * tq

    def per_bg(shape):
        return pl.BlockSpec((1, 1) + shape, lambda bi, gi, qi: (bi, gi, 0, 0))

    return pl.pallas_call(
        _nsa_prompt_kernel,
        out_shape=jax.ShapeDtypeStruct((b, t, NSA_HEADS * HEAD_DIM), F32),
        grid=(b, NSA_KV_HEADS, t // tq),
        in_specs=[pl.BlockSpec((1, 1, NSA_QPG, tq, HEAD_DIM), lambda bi, gi, qi: (bi, gi, 0, qi, 0)),
                  pl.BlockSpec((1, 1, tq, NSA_QPG * 3), lambda bi, gi, qi: (bi, gi, qi, 0)),
                  per_bg((HEAD_DIM, ncb)), per_bg((ncb, HEAD_DIM)),
                  pl.BlockSpec((1, NSA_QPG, tq, ncb), lambda bi, gi, qi: (gi, 0, qi, 0)),
                  per_bg((ka, tq + t)), per_bg((tq + t, LANES)),
                  per_bg((HEAD_DIM, WINDOW + t)), per_bg((WINDOW + t, LANES)),
                  pl.BlockSpec((1, NSA_QPG, tq, 2 * tq), lambda bi, gi, qi: (gi, 0, 0, 0)),
                  pl.BlockSpec((1, NSA_QPG, tq, WINDOW + tq), lambda bi, gi, qi: (gi, 0, 0, 0))],
        out_specs=pl.BlockSpec((1, tq, NSA_QPG * HEAD_DIM), lambda bi, gi, qi: (bi, qi, gi)),
        scratch_shapes=[pltpu.VMEM((rows, ka), BF16),
                        pltpu.VMEM((rows, LANES), F32), pltpu.VMEM((rows, LANES), F32),
                        pltpu.VMEM((NSA_QPG, tq, HEAD_DIM), F32),
                        pltpu.VMEM((rows, HEAD_DIM), F32), pltpu.VMEM((rows, HEAD_DIM), F32)],
        compiler_params=pltpu.CompilerParams(dimension_semantics=("arbitrary", "arbitrary", "arbitrary"),
                                             vmem_limit_bytes=VMEM_LIMIT),
        name="nsa_prompt",
    )(q4, g4, kcT, vcg, cb, ksT, vs, kwT, vw, sb, wb)


def _gather_pages(pt_ref, cache_ref, sem, layer, feat0, dst_of):
    s = pl.program_id(0)
    n_pages = pt_ref.shape[1]
    slot = lax.rem(s, 2)

    def copy(page, sl, p):
        return pltpu.make_async_copy(cache_ref.at[layer, page, pl.ds(feat0, HALF_WIDTH), :], dst_of(sl, p), sem.at[sl])

    def start(seq, sl):
        for p in range(n_pages):
            copy(pt_ref[seq, p], sl, p).start()

    @pl.when(s == 0)
    def _():
        start(0, 0)

    @pl.when(s + 1 < pl.num_programs(0))
    def _():
        start(s + 1, 1 - slot)

    for p in range(n_pages):
        copy(0, slot, p).wait()
    return slot


def _decode_compress_kernel(pt_ref, cache_ref, w_ref, out_ref, buf, rows_ref, sem, *, layer):
    slot = _gather_pages(pt_ref, cache_ref, sem, layer, 0, lambda sl, p: buf.at[sl, p])
    ncb = out_ref.shape[1]
    n_pages = pt_ref.shape[1]
    pair = 2 * PAGE_SIZE
    blocks_per_pair = pair // L_CMP
    row = lax.broadcasted_iota(jnp.int32, (pair, pair), 0)
    col = lax.broadcasted_iota(jnp.int32, (pair, pair), 1)
    perm = (col == (row % blocks_per_pair) * L_CMP + row // blocks_per_pair).astype(F32).astype(BF16)
    for pp in range(n_pages // 2):
        pages = jnp.concatenate([buf[slot, 2 * pp], buf[slot, 2 * pp + 1]], axis=1).astype(BF16)
        x = _dot_nt(perm, pages)
        for part in range(2):
            for r in range(L_CMP):
                rows_ref[part, r, pp * blocks_per_pair:(pp + 1) * blocks_per_pair, :] = (
                    x[r * blocks_per_pair:(r + 1) * blocks_per_pair, part * KV_WIDTH:(part + 1) * KV_WIDTH])
    for part in range(2):
        acc = jnp.zeros((ncb, KV_WIDTH), F32)
        for r in range(L_CMP):
            acc = acc + jnp.dot(rows_ref[part, r].astype(BF16), w_ref[r, part], preferred_element_type=F32)
        out_ref[0, :, part * KV_WIDTH:(part + 1) * KV_WIDTH] = acc


def decode_compress(page_table, cache, w_bd, layer):
    bs, n_pages = page_table.shape
    past = n_pages * PAGE_SIZE
    ncb = past // L_CMP
    return pl.pallas_call(
        functools.partial(_decode_compress_kernel, layer=layer),
        out_shape=jax.ShapeDtypeStruct((bs, ncb, HALF_WIDTH), F32),
        grid_spec=pltpu.PrefetchScalarGridSpec(
            num_scalar_prefetch=1,
            grid=(bs,),
            in_specs=[pl.BlockSpec(memory_space=pl.ANY),
                      pl.BlockSpec((L_CMP, 2, KV_WIDTH, KV_WIDTH), lambda s, pt: (0, 0, 0, 0))],
            out_specs=pl.BlockSpec((1, ncb, HALF_WIDTH), lambda s, pt: (s, 0, 0)),
            scratch_shapes=[pltpu.VMEM((2, n_pages, HALF_WIDTH, PAGE_SIZE), F32),
                            pltpu.VMEM((2, L_CMP, ncb, KV_WIDTH), F32), pltpu.SemaphoreType.DMA((2,))]),
        compiler_params=pltpu.CompilerParams(dimension_semantics=("arbitrary",), vmem_limit_bytes=VMEM_LIMIT),
        name="decode_compress",
    )(page_table, cache, w_bd)


def _dot_nt(a, b):
    return lax.dot_general(a, b, (((1,), (1,)), ((), ())), preferred_element_type=F32)


def _decode_attend_kernel(pt_ref, cache_ref, q_ref, gate_ref, kcvc_ref, new_ref, win_ref, cb_ref, sb_ref, wb_ref,
                          rel0_ref, onehot_ref, hmask_ref, o_ref, buf, sem, *, layer):
    slot = _gather_pages(pt_ref, cache_ref, sem, layer, HALF_WIDTH,
                         lambda sl, p: buf.at[sl, :, pl.ds(p * PAGE_SIZE, PAGE_SIZE)])
    q = q_ref[0]
    qf = q.astype(F32)
    ncb = kcvc_ref.shape[1]
    rel0 = rel0_ref[:, 0:1]

    def bf_round(x):
        return x.astype(BF16).astype(F32)

    kc = kcvc_ref[0, :, 0:KV_WIDTH].astype(BF16)
    vc = kcvc_ref[0, :, KV_WIDTH:HALF_WIDTH].astype(BF16)
    lg = _dot_nt(q, kc) + cb_ref[...]
    e = jnp.exp(lg - jnp.max(lg, axis=-1, keepdims=True))
    p = e / jnp.sum(e, axis=-1, keepdims=True)
    o_cmp = jnp.dot(p.astype(BF16), vc, preferred_element_type=F32)

    row = lax.broadcasted_iota(jnp.int32, (NSA_HEADS, ncb), 0)
    lane = lax.broadcasted_iota(jnp.int32, (NSA_HEADS, ncb), 1)
    pg0 = jnp.sum(p[0:NSA_QPG], axis=0, keepdims=True)
    pg1 = jnp.sum(p[NSA_QPG:NSA_HEADS], axis=0, keepdims=True)
    imp = jnp.where(row < NSA_QPG, pg0, pg1)
    pair = imp + pltpu.roll(imp, ncb - 1, 1)
    blk = lane >> 1
    forced = (blk == 0) | (blk == ncb // 2 - 1)
    score = jnp.where((lane & 1) == 0, jnp.where(forced, jnp.inf, pair), NEG_INF)
    lane_f = lane.astype(F32)
    chosen = jnp.zeros((NSA_HEADS, ncb), F32)
    for _ in range(N_SEL - 1):
        mx = jnp.max(score, axis=-1, keepdims=True)
        first = jnp.min(jnp.where(score == mx, lane_f, float(ncb)), axis=-1, keepdims=True)
        hit = lane_f == first
        chosen = jnp.where(hit, 1.0, chosen)
        score = jnp.where(hit, NEG_INF, score)
    blockmask = jnp.where(chosen > 0.0, 0.0, MASK_NEG).astype(BF16)

    ks_t = buf[slot, 0:KV_WIDTH, :].astype(BF16)
    vs_t = buf[slot, KV_WIDTH:HALF_WIDTH, :].astype(BF16)
    s = (jnp.dot(q, ks_t, preferred_element_type=F32)
         + jnp.dot(blockmask, onehot_ref[...], preferred_element_type=F32) + sb_ref[...])
    s_new = jnp.sum(qf * bf_round(new_ref[0, 0:1, :]), axis=-1, keepdims=True) + rel0
    m = jnp.maximum(jnp.max(s, axis=-1, keepdims=True), s_new)
    e = jnp.exp(s - m)
    e_new = jnp.exp(s_new - m)
    den = jnp.sum(e, axis=-1, keepdims=True) + e_new
    o_sel = (_dot_nt(e.astype(BF16), vs_t) + e_new * bf_round(new_ref[0, 1:2, :])) / den

    wk_t = win_ref[0, 0, 0:KV_WIDTH, :].astype(BF16)
    wv_t = win_ref[0, 0, KV_WIDTH:HALF_WIDTH, :].astype(BF16)
    sw = jnp.dot(q, wk_t, preferred_element_type=F32) + wb_ref[...]
    sw_new = jnp.sum(qf * bf_round(new_ref[0, 2:3, :]), axis=-1, keepdims=True) + rel0
    mw = jnp.maximum(jnp.max(sw, axis=-1, keepdims=True), sw_new)
    ew = jnp.exp(sw - mw)
    ew_new = jnp.exp(sw_new - mw)
    denw = jnp.sum(ew, axis=-1, keepdims=True) + ew_new
    o_win = (_dot_nt(ew.astype(BF16), wv_t) + ew_new * bf_round(new_ref[0, 3:4, :])) / denw

    g = jax.nn.sigmoid(gate_ref[0])
    o_ref[0] = (g[:, 0:1] * o_cmp + g[:, 1:2] * o_sel + g[:, 2:3] * o_win) * hmask_ref[...]


def decode_tables(rel_bias, past):
    ncb = past // L_CMP
    end = jnp.arange(ncb, dtype=jnp.int32) * L_CMP + (L_CMP - 1)
    cb = rel_bias_lookup(rel_bias, past - end).T
    sb = rel_bias_lookup(rel_bias, past - jnp.arange(past, dtype=jnp.int32)).T
    dist = WINDOW - jnp.arange(WINDOW, dtype=jnp.int32)
    wb = jnp.where((dist < WINDOW)[None, :], rel_bias_lookup(rel_bias, dist).T, MASK_NEG)
    rel0 = jnp.broadcast_to(rel_bias[0][:, None], (NSA_HEADS, LANES))
    onehot = (2 * (jnp.arange(past, dtype=jnp.int32) // L_SEL)[None, :] == jnp.arange(ncb, dtype=jnp.int32)[:, None]).astype(BF16)
    hmask = (jnp.arange(KV_WIDTH, dtype=jnp.int32)[None, :] // HEAD_DIM == jnp.arange(NSA_HEADS, dtype=jnp.int32)[:, None] // NSA_QPG).astype(F32)
    return cb, sb, wb, rel0, onehot, hmask


def compress_block_weights(w_cmp):
    w = w_cmp.reshape(2, L_CMP, HEAD_DIM, HEAD_DIM).transpose(1, 0, 2, 3)
    z = jnp.zeros_like(w)
    return jnp.concatenate([jnp.concatenate([w, z], axis=3), jnp.concatenate([z, w], axis=3)], axis=2).astype(BF16)


def nsa_decode(nq, nkv, ngate, page_table, cache_nsa, cache_win, w_bd, tables, layer):
    bs = nq.shape[0]
    n_pages = page_table.shape[1]
    past = n_pages * PAGE_SIZE
    ncb = past // L_CMP
    cb, sb, wb, rel0, onehot, hmask = tables
    kcvc = decode_compress(page_table, cache_nsa, w_bd, layer)
    qh = nq.reshape(bs, NSA_HEADS, 1, HEAD_DIM) * SCALE
    own_group = jnp.arange(NSA_KV_HEADS)[None, None, :, None] == (jnp.arange(NSA_HEADS) // NSA_QPG)[None, :, None, None]
    qblk = (qh * own_group).reshape(bs, NSA_HEADS, KV_WIDTH).astype(BF16)
    gpad = jnp.pad(ngate.reshape(bs, NSA_HEADS, 3), ((0, 0), (0, 0), (0, LANES - 3)))
    newr = jnp.pad(nkv.reshape(bs, 6, KV_WIDTH)[:, 2:6], ((0, 0), (0, 4), (0, 0)))

    def const2(shape):
        return pl.BlockSpec(shape, lambda s, pt: (0, 0))

    out = pl.pallas_call(
        functools.partial(_decode_attend_kernel, layer=layer),
        out_shape=jax.ShapeDtypeStruct((bs, NSA_HEADS, KV_WIDTH), F32),
        grid_spec=pltpu.PrefetchScalarGridSpec(
            num_scalar_prefetch=1,
            grid=(bs,),
            in_specs=[pl.BlockSpec(memory_space=pl.ANY),
                      pl.BlockSpec((1, NSA_HEADS, KV_WIDTH), lambda s, pt: (s, 0, 0)),
                      pl.BlockSpec((1, NSA_HEADS, LANES), lambda s, pt: (s, 0, 0)),
                      pl.BlockSpec((1, ncb, HALF_WIDTH), lambda s, pt: (s, 0, 0)),
                      pl.BlockSpec((1, 8, KV_WIDTH), lambda s, pt: (s, 0, 0)),
                      pl.BlockSpec((1, 1, HALF_WIDTH, WINDOW), lambda s, pt: (layer, s, 0, 0)),
                      const2((NSA_HEADS, ncb)), const2((NSA_HEADS, past)), const2((NSA_HEADS, WINDOW)),
                      const2((NSA_HEADS, LANES)), const2((ncb, past)), const2((NSA_HEADS, KV_WIDTH))],
            out_specs=pl.BlockSpec((1, NSA_HEADS, KV_WIDTH), lambda s, pt: (s, 0, 0)),
            scratch_shapes=[pltpu.VMEM((2, HALF_WIDTH, past), F32), pltpu.SemaphoreType.DMA((2,))]),
        compiler_params=pltpu.CompilerParams(dimension_semantics=("arbitrary",), vmem_limit_bytes=VMEM_LIMIT),
        name="decode_attend",
    )(page_table, cache_nsa, qblk, gpad, kcvc, newr, cache_win, cb, sb, wb, rel0, onehot, hmask)
    return (out[..., :HEAD_DIM] + out[..., HEAD_DIM:]).reshape(bs, 1, NSA_HEADS * HEAD_DIM)


_PEER_CANDS = [(a, b) for a in range(PEER_TOPK) for b in range(PEER_TOPK) if (a + 1) * (b + 1) <= PEER_TOPK]
_PEER_NCAND = len(_PEER_CANDS)
_PEER_NCAND_PAD = -(-_PEER_NCAND // 8) * 8
_PEER_GROUP_START = [min(c for c, (a, _) in enumerate(_PEER_CANDS) if a == aa) for aa in range(PEER_TOPK)]
_PEER_GROUP_LEN = [sum(1 for (a, _) in _PEER_CANDS if a == aa) for aa in range(PEER_TOPK)]


def _peer_route_kernel(xT_ref, wqT_ref, keys_ref, r2_ref, beta_ref, alpha_ref, lam_ref,
                       qT_ref, s_ref, rk_ref, vals_ref, cand_ref, sel_ref):
    tl = xT_ref.shape[1]
    qT_ref[...] = jnp.dot(wqT_ref[...], xT_ref[...], preferred_element_type=F32).astype(BF16)
    iota_k = lax.broadcasted_iota(jnp.int32, (N_KEYS, tl), 0).astype(F32)

    def rank_top16(p, stable):
        cur = s_ref[p]
        rk = jnp.full((N_KEYS, tl), float(PEER_TOPK), F32)
        for a in range(PEER_TOPK):
            m = jnp.max(cur, axis=0, keepdims=True)
            if stable:
                idx = jnp.min(jnp.where(cur == m, iota_k, float(N_KEYS)), axis=0, keepdims=True)
                hit = iota_k == idx
            else:
                hit = cur == m
            rk = jnp.where(hit, float(a), rk)
            cur = jnp.where(hit, NEG_INF, cur)
            vals_ref[p, a:a + 1, :] = m
        rk_ref[p] = rk
        return jnp.sum(jnp.where(rk < float(PEER_TOPK), 1.0, 0.0), axis=0, keepdims=True)

    def head_body(h, carry):
        taken = jnp.zeros((1, tl), F32)
        for p in range(2):
            row0 = pl.multiple_of(h * PEER_DK + p * (PEER_DK // 2), PEER_DK // 2)
            qs = qT_ref[pl.ds(row0, PEER_DK // 2), :]
            s_ref[p] = jnp.dot(keys_ref[2 * h + p], qs, preferred_element_type=F32)
            taken = jnp.maximum(taken, rank_top16(p, stable=False))

        @pl.when(jnp.max(taken) > float(PEER_TOPK))
        def _():
            for p in range(2):
                rank_top16(p, stable=True)

        for c, (a, b) in enumerate(_PEER_CANDS):
            cand_ref[c:c + 1, :] = vals_ref[0, a:a + 1, :] + vals_ref[1, b:b + 1, :]
        if _PEER_NCAND_PAD > _PEER_NCAND:
            cand_ref[_PEER_NCAND:_PEER_NCAND_PAD, :] = jnp.full((_PEER_NCAND_PAD - _PEER_NCAND, tl), NEG_INF, F32)
        ngrp = _PEER_NCAND_PAD // 8
        iota8 = lax.broadcasted_iota(jnp.int32, (8, tl), 0)
        ranks = [jnp.zeros((8, tl), F32) for _ in range(ngrp)]
        for cp in range(_PEER_NCAND):
            rowb = cand_ref[cp:cp + 1, :]
            for k in range(ngrp):
                blk = cand_ref[8 * k:8 * k + 8, :]
                if 8 * k > cp:
                    inc = jnp.where(rowb >= blk, 1.0, 0.0)
                elif 8 * k + 7 < cp:
                    inc = jnp.where(rowb > blk, 1.0, 0.0)
                else:
                    inc = jnp.where(iota8 + 8 * k > cp, jnp.where(rowb >= blk, 1.0, 0.0), jnp.where(rowb > blk, 1.0, 0.0))
                ranks[k] = ranks[k] + inc
        top = cand_ref[0:1, :]
        z = jnp.zeros((1, tl), F32)
        for k in range(ngrp):
            blk = cand_ref[8 * k:8 * k + 8, :]
            selk = ranks[k] < float(PEER_TOPK)
            sel_ref[8 * k:8 * k + 8, :] = jnp.where(selk, 1.0, 0.0)
            z = z + jnp.sum(jnp.where(selk, jnp.exp(blk - top), 0.0), axis=0, keepdims=True)
        rk1 = rk_ref[0]
        lam = jnp.full((N_KEYS, tl), -1.0, F32)
        for a in range(PEER_TOPK):
            g0, gl = _PEER_GROUP_START[a], _PEER_GROUP_LEN[a]
            la = jnp.sum(sel_ref[g0:g0 + gl, :], axis=0, keepdims=True) - 1.0
            lam = jnp.where(rk1 == float(a), la, lam)
        alpha = jnp.where(rk1 < float(PEER_TOPK), jnp.exp(s_ref[0] - vals_ref[0, 0:1, :]), 0.0)
        rk2 = rk_ref[1]
        beta = jnp.where(rk2 < float(PEER_TOPK), jnp.exp(s_ref[1] - vals_ref[1, 0:1, :]), 0.0) / z
        r2_ref[h] = rk2.astype(BF16)
        beta_ref[h] = beta.astype(BF16)
        alpha_ref[h] = alpha
        lam_ref[h] = lam
        return carry

    lax.fori_loop(0, PEER_HEADS, head_body, 0)


def peer_route(xT, wqT, keys2):
    d, n = xT.shape
    tl = PEER_ROUTE_TL if n % PEER_ROUTE_TL == 0 else LANES
    assert n % tl == 0
    nq = PEER_HEADS * PEER_DK
    out_bf = jax.ShapeDtypeStruct((PEER_HEADS, N_KEYS, n), BF16)
    out_f = jax.ShapeDtypeStruct((PEER_HEADS, N_KEYS, n), F32)
    tab_spec = pl.BlockSpec((PEER_HEADS, N_KEYS, tl), lambda i: (0, 0, i))
    return pl.pallas_call(
        _peer_route_kernel,
        out_shape=(out_bf, out_bf, out_f, out_f),
        grid=(n // tl,),
        in_specs=[pl.BlockSpec((d, tl), lambda i: (0, i)),
                  pl.BlockSpec((nq, d), lambda i: (0, 0)),
                  pl.BlockSpec((2 * PEER_HEADS, N_KEYS, PEER_DK // 2), lambda i: (0, 0, 0))],
        out_specs=(tab_spec, tab_spec, tab_spec, tab_spec),
        scratch_shapes=[pltpu.VMEM((nq, tl), BF16),
                        pltpu.VMEM((2, N_KEYS, tl), F32),
                        pltpu.VMEM((2, N_KEYS, tl), F32),
                        pltpu.VMEM((2, PEER_TOPK, tl), F32),
                        pltpu.VMEM((_PEER_NCAND_PAD, tl), F32),
                        pltpu.VMEM((_PEER_NCAND_PAD, tl), F32)],
        compiler_params=pltpu.CompilerParams(dimension_semantics=("arbitrary",), vmem_limit_bytes=VMEM_LIMIT),
        name="peer_route",
    )(xT, wqT, keys2)


def _gelu_tanh(x):
    return 0.5 * x * (1.0 + jnp.tanh(math.sqrt(2.0 / math.pi) * (x + 0.044715 * (x * x * x))))


def _peer_dense_kernel(xT_ref, u_ref, vT_ref, r2_ref, beta_ref, alpha_ref, lam_ref, yT_ref, a_ref, h_ref):
    j = pl.program_id(1)
    te = u_ref.shape[0]
    tl = xT_ref.shape[1]

    @pl.when(j == 0)
    def _():
        yT_ref[...] = jnp.zeros_like(yT_ref)

    n_sub = te // N_KEYS

    def pre_activation(r):
        rows = slice(r * N_KEYS, (r + 1) * N_KEYS)
        a_ref[rows, :] = jnp.dot(u_ref[rows, :], xT_ref[...], preferred_element_type=F32)

    pre_activation(0)
    for r in range(n_sub):
        rows = slice(r * N_KEYS, (r + 1) * N_KEYS)
        if r + 1 < n_sub:
            pre_activation(r + 1)
        g = jnp.zeros((N_KEYS, tl), BF16)
        for h in range(PEER_HEADS):
            lam = lam_ref[h, r:r + 1, :].astype(BF16)
            alp = alpha_ref[h, r:r + 1, :].astype(BF16)
            g = g + jnp.where(r2_ref[h] <= lam, beta_ref[h], jnp.zeros((), BF16)) * alp
        h_ref[rows, :] = _gelu_tanh(a_ref[rows, :].astype(BF16)) * g
        if r % PEER_SUB_PER_DOT == PEER_SUB_PER_DOT - 1:
            cols = slice((r + 1 - PEER_SUB_PER_DOT) * N_KEYS, (r + 1) * N_KEYS)
            yT_ref[...] += jnp.dot(vT_ref[:, cols], h_ref[cols, :], preferred_element_type=F32)


def peer_dense(xT, u_bf, vT_bf, r2, beta, alpha, lam, tl, te):
    d, n = xT.shape
    e = u_bf.shape[0]
    assert n % tl == 0 and e % te == 0 and te % (8 * N_KEYS) == 0
    tab_spec = pl.BlockSpec((PEER_HEADS, N_KEYS, tl), lambda i, j: (0, 0, i))
    row_spec = pl.BlockSpec((PEER_HEADS, te // N_KEYS, tl), lambda i, j: (0, j, i))
    return pl.pallas_call(
        _peer_dense_kernel,
        out_shape=jax.ShapeDtypeStruct((d, n), F32),
        grid=(n // tl, e // te),
        in_specs=[pl.BlockSpec((d, tl), lambda i, j: (0, i)),
                  pl.BlockSpec((te, d), lambda i, j: (j, 0)),
                  pl.BlockSpec((d, te), lambda i, j: (0, j)),
                  tab_spec, tab_spec, row_spec, row_spec],
        out_specs=pl.BlockSpec((d, tl), lambda i, j: (0, i)),
        scratch_shapes=[pltpu.VMEM((te, tl), F32), pltpu.VMEM((te, tl), BF16)],
        compiler_params=pltpu.CompilerParams(dimension_semantics=("arbitrary", "arbitrary"), vmem_limit_bytes=VMEM_LIMIT),
        name="peer_dense",
    )(xT, u_bf, vT_bf, r2, beta, alpha, lam)


def peer_ffn_t(xT, peer_w):
    wqT, keys2, u_bf, vT_bf = peer_w
    n = xT.shape[1]
    tl = PEER_TL if n % PEER_TL == 0 else LANES
    r2, beta, alpha, lam = peer_route(xT, wqT, keys2)
    return peer_dense(xT, u_bf, vT_bf, r2, beta, alpha, lam, tl, PEER_TE)


def peer_ffn(h, peer_w):
    b, t, d = h.shape
    n = b * t
    npad = -(-n // LANES) * LANES
    xT = jnp.pad(h.reshape(n, d).astype(BF16).T, ((0, 0), (0, npad - n)))
    return peer_ffn_t(xT, peer_w).T[:n].reshape(b, t, d)


POST_TM = 512
_POST_ROWS = 8


def _post_kernel(x_ref, y_ref, vec_ref, ox_ref, *oh_ref, alpha, y_transposed, h_transposed):
    y = y_ref[...]
    if y_transposed:
        y = y.T
    vec = vec_ref[0]
    r = alpha * x_ref[...] + vec[0:1] * y
    mu = jnp.mean(r, axis=-1, keepdims=True)
    rc = r - mu
    var = jnp.mean(rc * rc, axis=-1, keepdims=True)
    xn = rc * lax.rsqrt(var + LN_EPS) * vec[3:4] + vec[4:5]
    ox_ref[...] = xn
    if oh_ref:
        hm = xn * vec[1:2] + vec[2:3]
        oh_ref[0][...] = (hm.T if h_transposed else hm).astype(BF16)


def residual_norm_modulate(x, y, gate, ln_g, ln_b, scale, shift, alpha, tokens_per_batch, y_transposed, h_transposed):
    n, d = x.shape
    tm = POST_TM
    assert n % tm == 0 and tokens_per_batch % tm == 0
    nb = gate.shape[0]
    want_h = scale is not None
    if not want_h:
        scale = shift = jnp.zeros_like(gate)
    vec = jnp.stack([gate, scale, shift, jnp.broadcast_to(ln_g, (nb, d)), jnp.broadcast_to(ln_b, (nb, d))], axis=1)
    vec = jnp.pad(vec, ((0, 0), (0, _POST_ROWS - vec.shape[1]), (0, 0)))
    tiles_per_batch = tokens_per_batch // tm
    y_spec = pl.BlockSpec((d, tm), lambda i: (0, i)) if y_transposed else pl.BlockSpec((tm, d), lambda i: (i, 0))
    out_shape = [jax.ShapeDtypeStruct((n, d), F32)]
    out_specs = [pl.BlockSpec((tm, d), lambda i: (i, 0))]
    if want_h:
        out_shape.append(jax.ShapeDtypeStruct((d, n) if h_transposed else (n, d), BF16))
        out_specs.append(pl.BlockSpec((d, tm), lambda i: (0, i)) if h_transposed else pl.BlockSpec((tm, d), lambda i: (i, 0)))
    outs = pl.pallas_call(
        functools.partial(_post_kernel, alpha=alpha, y_transposed=y_transposed, h_transposed=h_transposed),
        out_shape=tuple(out_shape),
        grid=(n // tm,),
        in_specs=[pl.BlockSpec((tm, d), lambda i: (i, 0)), y_spec,
                  pl.BlockSpec((1, _POST_ROWS, d), lambda i: (i // tiles_per_batch, 0, 0))],
        out_specs=tuple(out_specs),
        compiler_params=pltpu.CompilerParams(dimension_semantics=("arbitrary",), vmem_limit_bytes=VMEM_LIMIT),
        name="residual_norm_modulate",
    )(x, y, vec)
    return (outs[0], outs[1]) if want_h else (outs[0], None)


def token_mixers(parts, pos, rel_table, w_cmp, conv_w, conv_b, ret_g, past, nsa_tables):
    nq, nkv, ngate, rq, rk, rv, rg, cb, cc, ch = parts
    b, t = nq.shape[:2]
    q = nq.reshape(b, t, NSA_KV_HEADS, NSA_QPG, HEAD_DIM)
    kv = nkv.reshape(b, t, 6, NSA_KV_HEADS, HEAD_DIM)
    if past is None:
        kc = compress(kv[:, :, 0], w_cmp[0])
        vc = compress(kv[:, :, 1], w_cmp[1])
        o_nsa = nsa_prompt(nq, kv, ngate, kc, vc, nsa_tables)
        win_rows = kv[:, t - min(WINDOW, t):, 4:]
        zbuf = jnp.zeros((b, CONV_W - 1, CONV_CH), ch.dtype)
        o_ret, s_new = retention_prompt(rq, rk, rv, rg, ret_g)
    else:
        decode_nsa, win_buf, s0, zbuf = past
        o_nsa = decode_nsa(nq, nkv, ngate)
        wrows = jnp.concatenate([win_buf.astype(kv.dtype), kv[:, :, 4:]], axis=1)
        win_rows = wrows[:, wrows.shape[1] - min(WINDOW, wrows.shape[1]):]
        rqh = rotary(rq.reshape(b, t, RET_HEADS, RET_DK), pos)
        rkh = rotary(rk.reshape(b, t, RET_HEADS, RET_DK), pos) * (RET_DK ** -0.5)
        rvh = rv.reshape(b, t, RET_HEADS, RET_DV)
        o_r, s_new = retention(rqh, rkh, rvh, s0, t)
        mu = jnp.mean(o_r, -1, keepdims=True)
        var = jnp.mean(jnp.square(o_r - mu), -1, keepdims=True)
        on = (o_r - mu) * lax.rsqrt(var + LN_EPS) * ret_g.reshape(RET_HEADS, RET_DV)
        o_ret = (on.reshape(b, t, RET_WIDTH) * jax.nn.silu(rg.astype(jnp.float32))).astype(nq.dtype)
    z = cc * ch
    zp = jnp.concatenate([zbuf.astype(z.dtype), z], axis=1)
    y = conv_b + sum(zp[:, j:j + t] * conv_w[j] for j in range(CONV_W))
    o_conv = (cb * y).astype(nq.dtype)
    mix = jnp.concatenate([o_nsa, o_ret, o_conv], axis=-1)
    return mix, (kv[:, :, :4], win_rows, s_new, zp[:, t:])


def kernel(x_prompt, x_sample, cache_nsa_kv, cache_win_kv, state_ret, state_conv, page_table, c_prompt, c_sample, rel_bias, w_ada, b_ada, w_in, w_cmp, conv_w, conv_b, ret_norm_g, w_out, ln1_g, ln1_b, ln2_g, ln2_b, peer_wq, peer_keys, peer_u, peer_v):
    alpha = (2.0 * DEPTH) ** 0.25
    n_pages = page_table.shape[1]
    past_len = n_pages * PAGE_SIZE
    sp = x_prompt.shape[1]
    bs, ts = x_sample.shape[:2]
    pos_p = jnp.arange(sp, dtype=jnp.int32)
    pos_s = past_len + jnp.arange(ts, dtype=jnp.int32)
    nsa_tables = nsa_bias_tables(rel_bias, sp, NSA_TQ)
    dec_tables = decode_tables(rel_bias, past_len)
    cache_nsa = cache_nsa_kv.transpose(0, 1, 3, 4, 5, 2).reshape(cache_nsa_kv.shape[0], cache_nsa_kv.shape[1], ROW_WIDTH, PAGE_SIZE)
    cache_win = cache_win_kv.transpose(0, 1, 3, 4, 5, 2).reshape(cache_win_kv.shape[0], bs, HALF_WIDTH, cache_win_kv.shape[2])

    def ada(c, l):
        return (jax.nn.silu(c) @ w_ada[l] + b_ada[l]).reshape(c.shape[0], 6, D_MODEL)

    def run_layer(x, c, l, pos, past, peer_w):
        m = ada(c, l)[:, :, None, :]
        h = x * (1.0 + m[:, 1]) + m[:, 0]
        parts = split_proj(h @ w_in[l])
        mix, st = token_mixers(parts, pos, rel_bias, w_cmp[l], conv_w[l], conv_b[l], ret_norm_g[l], past, nsa_tables)
        x = layer_norm(alpha * x + (1.0 + m[:, 2]) * (mix @ w_out[l]), ln1_g[l], ln1_b[l])
        h = x * (1.0 + m[:, 4]) + m[:, 3]
        y = peer_ffn(h, peer_w)
        x = layer_norm(alpha * x + (1.0 + m[:, 5]) * y, ln2_g[l], ln2_b[l])
        return x, st

    bp = x_prompt.shape[0]
    m_prompt = [ada(c_prompt, l) for l in range(DEPTH)]

    def run_prompt_layer(x2, h, l, peer_w):
        m = m_prompt[l]
        parts = split_proj(jnp.matmul(h, w_in[l].astype(h.dtype), preferred_element_type=F32))
        mix, st = token_mixers(parts, pos_p, rel_bias, w_cmp[l], conv_w[l], conv_b[l], ret_norm_g[l], None, nsa_tables)
        y = (mix @ w_out[l]).reshape(bp * sp, D_MODEL)
        x2, h_t = residual_norm_modulate(x2, y, 1.0 + m[:, 2], ln1_g[l], ln1_b[l], 1.0 + m[:, 4], m[:, 3], alpha, sp,
                                         y_transposed=False, h_transposed=True)
        y_t = peer_ffn_t(h_t, peer_w)
        if l + 1 < DEPTH:
            nxt = m_prompt[l + 1]
            x2, h = residual_norm_modulate(x2, y_t, 1.0 + m[:, 5], ln2_g[l], ln2_b[l], 1.0 + nxt[:, 1], nxt[:, 0], alpha, sp,
                                           y_transposed=True, h_transposed=False)
            h = h.reshape(bp, sp, D_MODEL)
        else:
            x2, h = residual_norm_modulate(x2, y_t, 1.0 + m[:, 5], ln2_g[l], ln2_b[l], None, None, alpha, sp,
                                           y_transposed=True, h_transposed=False)
        return x2, h, st

    xp2 = x_prompt.reshape(bp * sp, D_MODEL)
    hp = x_prompt * (1.0 + m_prompt[0][:, 1][:, None, :]) + m_prompt[0][:, 0][:, None, :]
    xs = x_sample
    nkv_p, nkv_s, win_p, win_s, ret_p, ret_s, conv_p, conv_s = [], [], [], [], [], [], [], []
    for l in range(DEPTH):
        peer_w = (peer_wq[l].T.astype(BF16),
                  peer_keys[l].reshape(2 * PEER_HEADS, N_KEYS, PEER_DK // 2).astype(BF16),
                  peer_u[l].astype(BF16),
                  peer_v[l].T.astype(BF16))
        xp2, hp, st_p = run_prompt_layer(xp2, hp, l, peer_w)
        decode_nsa = functools.partial(nsa_decode, page_table=page_table, cache_nsa=cache_nsa, cache_win=cache_win,
                                       w_bd=compress_block_weights(w_cmp[l]), tables=dec_tables, layer=l)
        xs, st_s = run_layer(xs, c_sample, l, pos_s, (decode_nsa, cache_win_kv[l], state_ret[l], state_conv[l]), peer_w)
        nkv_p.append(st_p[0])
        win_p.append(st_p[1])
        ret_p.append(st_p[2])
        conv_p.append(st_p[3])
        nkv_s.append(st_s[0])
        win_s.append(st_s[1])
        ret_s.append(st_s[2])
        conv_s.append(st_s[3])
    xp = xp2.reshape(bp, sp, D_MODEL)
    return (xp, xs, jnp.stack(nkv_p), jnp.stack(nkv_s), jnp.stack(win_p), jnp.stack(win_s), jnp.stack(ret_p), jnp.stack(ret_s), jnp.stack(conv_p), jnp.stack(conv_s))
```

```python
import math
import functools
import jax
import jax.numpy as jnp
from jax import lax
import numpy as np
from jax.experimental import pallas as pl
from jax.experimental.pallas import tpu as pltpu

D_MODEL = 1024
BATCH = 4
SEQ = 4096
DEPTH = 2
DEC_BATCH = 32
DEC_SEQ = 1
PAST_LEN = 8192
PAGE_SIZE = 128

HEAD_DIM = 64
NSA_WIDTH = D_MODEL // 2
NSA_HEADS = NSA_WIDTH // HEAD_DIM
NSA_KV_HEADS = 2
NSA_QPG = NSA_HEADS // NSA_KV_HEADS
KV_WIDTH = NSA_KV_HEADS * HEAD_DIM
SCALE = HEAD_DIM ** -0.5
L_CMP = 32
L_SEL = 64
N_SEL = 16
WINDOW = 512
Q_BLOCK = 128
SEL_Q_BLOCK = 64
N_BUCKETS = 32
REL_MAX_DIST = 128
RET_WIDTH = D_MODEL // 4
RET_DK = 64
RET_DV = 64
RET_HEADS = RET_WIDTH // RET_DV
RET_CHUNK = 128
ROPE_BASE = 10000.0
CONV_CH = D_MODEL // 4
CONV_W = 3
MIX_WIDTH = NSA_WIDTH + RET_WIDTH + CONV_CH
PEER_HEADS = 8
PEER_DK = 256
N_KEYS = 128
N_EXPERTS = N_KEYS * N_KEYS
PEER_TOPK = 16
PEER_CHUNK = 256
LN_EPS = 1e-5
SPLIT_SIZES = (NSA_WIDTH, 6 * KV_WIDTH, 3 * NSA_HEADS, RET_HEADS * RET_DK, RET_HEADS * RET_DK, RET_WIDTH, RET_WIDTH, CONV_CH, CONV_CH, CONV_CH)
N_IN = sum(SPLIT_SIZES)

F32 = jnp.float32
BF16 = jnp.bfloat16
NEG_INF = float('-inf')
MASK_NEG = -1e9
CMP_MASK_NEG = -1e30
NSA_TQ = 256
NSA_RB = 128
NSA_FAR_UNROLL = 4
ROW_WIDTH = 4 * KV_WIDTH
HALF_WIDTH = 2 * KV_WIDTH
PEER_TL = 512
PEER_TE = 4096
PEER_ROUTE_TL = 256
LANES = 128
VMEM_LIMIT = 56 * 1024 * 1024


def _ln_kernel(x_ref, g_ref, b_ref, o_ref):
    x = x_ref[...]
    mu = jnp.mean(x, -1, keepdims=True)
    xc = x - mu
    var = jnp.mean(xc * xc, -1, keepdims=True)
    o_ref[...] = xc * lax.rsqrt(var + LN_EPS) * g_ref[...] + b_ref[...]


def layer_norm(x, g, b):
    shp = x.shape
    x2 = x.reshape(-1, shp[-1])
    n = x2.shape[0]
    tm = min(n, 512)
    out = pl.pallas_call(
        _ln_kernel,
        out_shape=jax.ShapeDtypeStruct(x2.shape, jnp.float32),
        grid=(n // tm,),
        in_specs=[pl.BlockSpec((tm, shp[-1]), lambda i: (i, 0)),
                  pl.BlockSpec((1, shp[-1]), lambda i: (0, 0)),
                  pl.BlockSpec((1, shp[-1]), lambda i: (0, 0))],
        out_specs=pl.BlockSpec((tm, shp[-1]), lambda i: (i, 0)),
        name="layer_norm",
    )(x2, g.reshape(1, -1), b.reshape(1, -1))
    return out.reshape(shp)


def t5_bucket(dist):
    n = jnp.maximum(dist, 0)
    exact = N_BUCKETS // 2
    nf = jnp.maximum(n, exact).astype(jnp.float32)
    big = exact + (jnp.log(nf / exact) / math.log(REL_MAX_DIST / exact) * (N_BUCKETS - exact)).astype(jnp.int32)
    return jnp.where(n < exact, n, jnp.minimum(big, N_BUCKETS - 1))


def rel_bias_lookup(rel_bias, dist):
    onehot = jax.nn.one_hot(t5_bucket(dist), N_BUCKETS, dtype=jnp.float32)
    return jnp.einsum('...k,kh->...h', onehot, rel_bias, precision=lax.Precision.HIGHEST)


def rotary(x, pos):
    half = x.shape[-1] // 2
    inv = ROPE_BASE ** (-jnp.arange(half, dtype=jnp.float32) / half)
    ang = pos.astype(jnp.float32)[:, None] * inv[None, :]
    cos = jnp.cos(ang)[None, :, None, :]
    sin = jnp.sin(ang)[None, :, None, :]
    x1, x2 = x[..., :half], x[..., half:]
    return jnp.concatenate([x1 * cos - x2 * sin, x1 * sin + x2 * cos], axis=-1)


def split_proj(p):
    outs, start = [], 0
    for size in SPLIT_SIZES:
        outs.append(p[..., start:start + size])
        start += size
    return outs


def compress(rows, w):
    b, lp, g, d = rows.shape
    blk = rows.reshape(b, lp // L_CMP, L_CMP, g, d).transpose(0, 1, 3, 2, 4).reshape(b, lp // L_CMP, g, L_CMP * d)
    return blk @ w


def retention(q, k, v, s0, chunk):
    b, t = q.shape[:2]
    nc = t // chunk
    lg = jnp.log(1.0 - 2.0 ** (-5.0 - jnp.arange(RET_HEADS, dtype=jnp.float32)))
    i = jnp.arange(chunk, dtype=jnp.float32)
    diff = i[:, None] - i[None, :]
    dmat = jnp.where(diff >= 0, jnp.exp(jnp.maximum(diff, 0.0)[None] * lg[:, None, None]), 0.0)
    q_dec = jnp.exp((i + 1.0)[:, None] * lg[None, :])[None, :, :, None]
    k_dec = jnp.exp((chunk - 1.0 - i)[:, None] * lg[None, :])[None, :, :, None]
    s_dec = jnp.exp(chunk * lg)[None, :, None, None]

    def to_chunks(a):
        return jnp.moveaxis(a.astype(jnp.float32).reshape(b, nc, chunk, RET_HEADS, a.shape[-1]), 1, 0)

    def step(s, xs):
        qc, kc, vc = xs
        att = jnp.einsum('bihd,bjhd->bhij', qc, kc) * dmat
        o = jnp.einsum('bhij,bjhe->bihe', att, vc) + jnp.einsum('bihd,bhde->bihe', qc * q_dec, s)
        s = s * s_dec + jnp.einsum('bjhd,bjhe->bhde', kc * k_dec, vc)
        return s, o

    s, o = lax.scan(step, s0.astype(jnp.float32), (to_chunks(q), to_chunks(k), to_chunks(v)))
    return jnp.moveaxis(o, 0, 1).reshape(b, t, RET_HEADS, RET_DV), s


def _retention_kernel(q_ref, k_ref, v_ref, g_ref, cos_ref, sin_ref, dmat_ref, qdec_ref, kdec_ref, sdec_ref, gain_ref,
                      o_ref, sout_ref, s_scr):
    c = pl.program_id(1)

    @pl.when(c == 0)
    def _():
        s_scr[...] = jnp.zeros_like(s_scr)

    chunk, width = q_ref.shape[1], q_ref.shape[2]
    half = RET_DK // 2
    lane = lax.broadcasted_iota(jnp.int32, (chunk, width), 1)
    first_half = (lane % RET_DK) < half

    def rot(x):
        swapped = jnp.where(first_half, pltpu.roll(x, width - half, 1), pltpu.roll(x, half, 1))
        return x * cos_ref[...] + swapped * sin_ref[...]

    q = rot(q_ref[0])
    k = rot(k_ref[0]) * (RET_DK ** -0.5)
    v = v_ref[0]
    gate = g_ref[0]
    eye = (lax.broadcasted_iota(jnp.int32, (RET_DK, RET_DK), 0)
           == lax.broadcasted_iota(jnp.int32, (RET_DK, RET_DK), 1)).astype(F32).astype(BF16)
    for h in range(RET_HEADS):
        sl = slice(h * RET_DK, (h + 1) * RET_DK)
        qh, kh, vh = q[:, sl], k[:, sl], v[:, sl].astype(BF16)
        att = _dot_nt(qh.astype(BF16), kh.astype(BF16)) * dmat_ref[h]
        s_old = s_scr[h]
        o = (jnp.dot(att.astype(BF16), vh, preferred_element_type=F32)
             + jnp.dot((qh * qdec_ref[:, sl]).astype(BF16), s_old.astype(BF16), preferred_element_type=F32))
        kd_t = _dot_nt(eye, (kh * kdec_ref[:, sl]).astype(BF16)).astype(BF16)
        s_scr[h] = s_old * sdec_ref[h, 0:1, 0:RET_DV] + jnp.dot(kd_t, vh, preferred_element_type=F32)
        mu = jnp.mean(o, axis=-1, keepdims=True)
        oc = o - mu
        var = jnp.mean(oc * oc, axis=-1, keepdims=True)
        gh = gate[:, sl]
        o_ref[0, :, sl] = oc * lax.rsqrt(var + LN_EPS) * gain_ref[:, sl] * (gh * jax.nn.sigmoid(gh))
    sout_ref[0] = s_scr[...]


def retention_prompt(rq, rk, rv, rg, ret_g):
    b, t, width = rq.shape
    chunk = RET_CHUNK
    half = RET_DK // 2
    pos = jnp.arange(t, dtype=jnp.float32)
    inv = ROPE_BASE ** (-jnp.arange(half, dtype=jnp.float32) / half)
    ang = pos[:, None] * inv[None, :]
    cos = jnp.tile(jnp.cos(ang), (1, 2 * RET_HEADS))
    sin = jnp.tile(jnp.concatenate([-jnp.sin(ang), jnp.sin(ang)], axis=1), (1, RET_HEADS))
    lg = jnp.log(1.0 - 2.0 ** (-5.0 - jnp.arange(RET_HEADS, dtype=jnp.float32)))
    i = jnp.arange(chunk, dtype=jnp.float32)
    diff = i[:, None] - i[None, :]
    dmat = jnp.where(diff >= 0, jnp.exp(jnp.maximum(diff, 0.0)[None] * lg[:, None, None]), 0.0)
    qdec = jnp.repeat(jnp.exp((i + 1.0)[:, None] * lg[None, :]), RET_DK, axis=1)
    kdec = jnp.repeat(jnp.exp((chunk - 1.0 - i)[:, None] * lg[None, :]), RET_DK, axis=1)
    sdec = jnp.broadcast_to(jnp.exp(chunk * lg)[:, None, None], (RET_HEADS, 8, LANES))
    tok = pl.BlockSpec((1, chunk, width), lambda bi, ci: (bi, ci, 0))
    tab = pl.BlockSpec((chunk, width), lambda bi, ci: (ci, 0))

    def const(shape):
        return pl.BlockSpec(shape, lambda bi, ci: (0,) * len(shape))

    return pl.pallas_call(
        _retention_kernel,
        out_shape=(jax.ShapeDtypeStruct((b, t, width), F32),
                   jax.ShapeDtypeStruct((b, RET_HEADS, RET_DK, RET_DV), F32)),
        grid=(b, t // chunk),
        in_specs=[tok, tok, tok, tok, tab, tab, const((RET_HEADS, chunk, chunk)), const((chunk, width)),
                  const((chunk, width)), const((RET_HEADS, 8, LANES)), const((1, width))],
        out_specs=(tok, pl.BlockSpec((1, RET_HEADS, RET_DK, RET_DV), lambda bi, ci: (bi, 0, 0, 0))),
        scratch_shapes=[pltpu.VMEM((RET_HEADS, RET_DK, RET_DV), F32)],
        compiler_params=pltpu.CompilerParams(dimension_semantics=("arbitrary", "arbitrary"), vmem_limit_bytes=VMEM_LIMIT),
        name="retention_prompt",
    )(rq, rk, rv, rg, cos, sin, dmat, qdec, kdec, sdec, ret_g.reshape(1, width))


def _nsa_prompt_kernel(q_ref, gate_ref, kcT_ref, vc_ref, cb_ref, ksT_ref, vs_ref, kwT_ref, vw_ref, sb_ref, wb_ref,
                       o_ref, qa_ref, m_ref, acc_ref, oc_ref, osel_ref, owin_ref):
    qi = pl.program_id(2)
    tq = q_ref.shape[3]
    ncb = kcT_ref.shape[3]
    ka = ksT_ref.shape[2]
    t0 = qi * tq
    rows = NSA_QPG * tq

    imp = jnp.zeros((tq, ncb), F32)
    for h in range(NSA_QPG):
        bias = cb_ref[0, h]
        lg = jnp.dot(q_ref[0, 0, h], kcT_ref[0, 0], preferred_element_type=F32) + bias
        mx = jnp.max(lg, axis=-1, keepdims=True)
        e = jnp.where(bias > 0.5 * CMP_MASK_NEG, jnp.exp(lg - mx), 0.0)
        p = e / jnp.maximum(jnp.sum(e, axis=-1, keepdims=True), 1e-30)
        imp = imp + p
        oc_ref[h] = jnp.dot(p.astype(BF16), vc_ref[0, 0], preferred_element_type=F32)

    nrb = rows // NSA_RB
    per_head = tq // NSA_RB

    def reset():
        m_ref[...] = jnp.full((rows, LANES), CMP_MASK_NEG, F32)
        acc_ref[...] = jnp.zeros((rows, LANES), F32)

    def attn_step(rb, qrb, k_t, v, bias):
        r0 = rb * NSA_RB
        s = jnp.dot(qrb, k_t, preferred_element_type=F32)
        if bias is not None:
            s = s + bias
        parts = [s[:, i * LANES:(i + 1) * LANES] for i in range(s.shape[1] // LANES)]
        red = parts[0]
        for part in parts[1:]:
            red = jnp.maximum(red, part)
        m_old = m_ref[r0:r0 + NSA_RB, :]
        m_new = jnp.maximum(m_old, jnp.max(red, axis=-1, keepdims=True))
        p = jnp.concatenate([jnp.exp(part - m_new) for part in parts], axis=1).astype(BF16)
        acc_ref[r0:r0 + NSA_RB, :] = (jnp.exp(m_old - m_new) * acc_ref[r0:r0 + NSA_RB, :]
                                      + jnp.dot(p, v, preferred_element_type=F32))
        m_ref[r0:r0 + NSA_RB, :] = m_new

    def finish(dst_ref):
        acc = acc_ref[...]
        dst_ref[...] = acc[:, 0:HEAD_DIM] / acc[:, HEAD_DIM:HEAD_DIM + 1]

    col0 = pl.multiple_of(t0, tq)

    reset()
    for c in range((WINDOW + tq) // tq):
        colc = pl.multiple_of(col0 + c * tq, tq)
        k_t = kwT_ref[0, 0, :, pl.ds(colc, tq)]
        v = vw_ref[0, 0, pl.ds(colc, tq), :]
        jw = lax.broadcasted_iota(jnp.int32, (1, tq), 1) + c * tq
        wmask = jnp.where(jw + (t0 - WINDOW) >= 0, 0.0, MASK_NEG)
        for rb in range(nrb):
            h, part = rb // per_head, rb % per_head
            bias = wb_ref[0, h, part * NSA_RB:(part + 1) * NSA_RB, c * tq:(c + 1) * tq] + wmask
            attn_step(rb, q_ref[0, 0, h, part * NSA_RB:(part + 1) * NSA_RB, :], k_t, v, bias)
    finish(owin_ref)

    lane = lax.broadcasted_iota(jnp.int32, (tq, ncb), 1)
    tpos = t0 + lax.broadcasted_iota(jnp.int32, (tq, ncb), 0)
    pair = imp + pltpu.roll(imp, ncb - 1, 1)
    blk = lane >> 1
    cur = tpos >> 6
    forced = (blk == 0) | (blk == cur) | (blk == cur - 1)
    cand = ((lane & 1) == 0) & (blk <= cur)
    score = jnp.where(cand, jnp.where(forced, jnp.inf, pair), NEG_INF)
    lane_f = lane.astype(F32)
    chosen = jnp.zeros((tq, ncb), F32)
    for _ in range(N_SEL):
        mx = jnp.max(score, axis=-1, keepdims=True)
        first = jnp.min(jnp.where(score == mx, lane_f, float(ncb)), axis=-1, keepdims=True)
        hit = (lane_f == first) & (mx > NEG_INF)
        chosen = jnp.where(hit, 1.0, chosen)
        score = jnp.where(hit, NEG_INF, score)
    blockmask = jnp.where(chosen > 0.0, 0.0, MASK_NEG).astype(BF16)

    for h in range(NSA_QPG):
        qa_ref[h * tq:(h + 1) * tq, 0:ncb] = blockmask
        qa_ref[h * tq:(h + 1) * tq, ncb:ncb + HEAD_DIM] = q_ref[0, 0, h]
        if ka > ncb + HEAD_DIM:
            qa_ref[h * tq:(h + 1) * tq, ncb + HEAD_DIM:ka] = jnp.zeros((tq, ka - ncb - HEAD_DIM), BF16)

    reset()

    def far_chunk(c):
        col = pl.multiple_of((c + 1) * tq, tq)
        k_t = ksT_ref[0, 0, :, pl.ds(col, tq)]
        v = vs_ref[0, 0, pl.ds(col, tq), :]
        for rb in range(nrb):
            attn_step(rb, qa_ref[rb * NSA_RB:(rb + 1) * NSA_RB, :], k_t, v, None)

    def far_group(c, carry):
        for i in range(NSA_FAR_UNROLL):
            far_chunk(NSA_FAR_UNROLL * c + i)
        return carry

    n_far = jnp.maximum(qi - 1, 0)
    lax.fori_loop(0, n_far // NSA_FAR_UNROLL, far_group, 0)
    done = (n_far // NSA_FAR_UNROLL) * NSA_FAR_UNROLL
    width = NSA_FAR_UNROLL // 2
    while width >= 1:
        @pl.when(((n_far - done) & width) != 0)
        def _(done=done, width=width):
            for i in range(width):
                far_chunk(done + i)
        done = done + ((n_far - done) & width)
        width //= 2

    for c in range(2):
        colc = pl.multiple_of(col0 + c * tq, tq)
        k_t = ksT_ref[0, 0, :, pl.ds(colc, tq)]
        v = vs_ref[0, 0, pl.ds(colc, tq), :]
        jn = lax.broadcasted_iota(jnp.int32, (1, tq), 1) + c * tq
        colmask = jnp.where(jn + (t0 - tq) >= 0, 0.0, MASK_NEG)
        for rb in range(nrb):
            h, part = rb // per_head, rb % per_head
            bias = sb_ref[0, h, part * NSA_RB:(part + 1) * NSA_RB, c * tq:(c + 1) * tq] + colmask
            attn_step(rb, qa_ref[rb * NSA_RB:(rb + 1) * NSA_RB, :], k_t, v, bias)
    finish(osel_ref)

    g = jax.nn.sigmoid(gate_ref[0, 0])
    for h in range(NSA_QPG):
        o_h = (g[:, 3 * h:3 * h + 1] * oc_ref[h]
               + g[:, 3 * h + 1:3 * h + 2] * osel_ref[h * tq:(h + 1) * tq, :]
               + g[:, 3 * h + 2:3 * h + 3] * owin_ref[h * tq:(h + 1) * tq, :])
        o_ref[0, :, h * HEAD_DIM:(h + 1) * HEAD_DIM] = o_h


def nsa_bias_tables(rel_bias, t, tq):
    ncb = t // L_CMP

    def heads_first(tab):
        return tab.transpose(2, 0, 1).reshape(NSA_KV_HEADS, NSA_QPG, tab.shape[0], tab.shape[1])

    pos = jnp.arange(t, dtype=jnp.int32)
    end = jnp.arange(ncb, dtype=jnp.int32) * L_CMP + (L_CMP - 1)
    dist = pos[:, None] - end[None, :]
    cb = jnp.where((dist >= 0)[..., None], rel_bias_lookup(rel_bias, dist), CMP_MASK_NEG)
    i = jnp.arange(tq, dtype=jnp.int32)
    dist = i[:, None] + tq - jnp.arange(2 * tq, dtype=jnp.int32)[None, :]
    sb = jnp.where((dist >= 0)[..., None], rel_bias_lookup(rel_bias, dist) - rel_bias[N_BUCKETS - 1], MASK_NEG)
    dist = i[:, None] + WINDOW - jnp.arange(WINDOW + tq, dtype=jnp.int32)[None, :]
    wb = jnp.where(((dist >= 0) & (dist < WINDOW))[..., None], rel_bias_lookup(rel_bias, dist), MASK_NEG)
    return heads_first(cb), heads_first(sb), heads_first(wb)


def nsa_prompt(q, kv, gates, kc, vc, tables):
    b, t = q.shape[:2]
    tq = NSA_TQ
    ncb = t // L_CMP
    ka = -(-(ncb + HEAD_DIM) // LANES) * LANES
    cb, sb, wb = tables
    q4 = (q * SCALE).astype(BF16).reshape(b, t, NSA_KV_HEADS, NSA_QPG, HEAD_DIM).transpose(0, 2, 3, 1, 4)
    g4 = gates.reshape(b, t, NSA_KV_HEADS, NSA_QPG * 3).transpose(0, 2, 1, 3)
    kcT = kc.astype(BF16).transpose(0, 2, 3, 1)
    vcg = vc.astype(BF16).transpose(0, 2, 1, 3)
    kvb = kv.astype(BF16)
    onehot = (2 * (jnp.arange(t, dtype=jnp.int32) // L_SEL)[None, :] == jnp.arange(ncb, dtype=jnp.int32)[:, None]).astype(BF16)
    ks_t = kvb[:, :, 2].transpose(0, 2, 3, 1)
    ksT = jnp.concatenate([jnp.broadcast_to(onehot, (b, NSA_KV_HEADS, ncb, t)), ks_t,
                           jnp.zeros((b, NSA_KV_HEADS, ka - ncb - HEAD_DIM, t), BF16)], axis=2)
    ksT = jnp.pad(ksT, ((0, 0), (0, 0), (0, 0), (tq, 0)))
    def with_ones(v):
        one = jnp.ones(v.shape[:-1] + (1,), BF16)
        return jnp.concatenate([v, one, jnp.zeros(v.shape[:-1] + (LANES - HEAD_DIM - 1,), BF16)], axis=-1)

    vs = jnp.pad(with_ones(kvb[:, :, 3].transpose(0, 2, 1, 3)), ((0, 0), (0, 0), (tq, 0), (0, 0)))
    kwT = jnp.pad(kvb[:, :, 4].transpose(0, 2, 3, 1), ((0, 0), (0, 0), (0, 0), (WINDOW, 0)))
    vw = jnp.pad(with_ones(kvb[:, :, 5].transpose(0, 2, 1, 3)), ((0, 0), (0, 0), (WINDOW, 0), (0, 0)))
    rows = NSA_QPG * tq

    def per_bg(shape):
        return pl.BlockSpec((1, 1) + shape, lambda bi, gi, qi: (bi, gi, 0, 0))

    return pl.pallas_call(
        _nsa_prompt_kernel,
        out_shape=jax.ShapeDtypeStruct((b, t, NSA_HEADS * HEAD_DIM), F32),
        grid=(b, NSA_KV_HEADS, t // tq),
        in_specs=[pl.BlockSpec((1, 1, NSA_QPG, tq, HEAD_DIM), lambda bi, gi, qi: (bi, gi, 0, qi, 0)),
                  pl.BlockSpec((1, 1, tq, NSA_QPG * 3), lambda bi, gi, qi: (bi, gi, qi, 0)),
                  per_bg((HEAD_DIM, ncb)), per_bg((ncb, HEAD_DIM)),
                  pl.BlockSpec((1, NSA_QPG, tq, ncb), lambda bi, gi, qi: (gi, 0, qi, 0)),
                  per_bg((ka, tq + t)), per_bg((tq + t, LANES)),
                  per_bg((HEAD_DIM, WINDOW + t)), per_bg((WINDOW + t, LANES)),
                  pl.BlockSpec((1, NSA_QPG, tq, 2 * tq), lambda bi, gi, qi: (gi, 0, 0, 0)),
                  pl.BlockSpec((1, NSA_QPG, tq, WINDOW + tq), lambda bi, gi, qi: (gi, 0, 0, 0))],
        out_specs=pl.BlockSpec((1, tq, NSA_QPG * HEAD_DIM), lambda bi, gi, qi: (bi, qi, gi)),
        scratch_shapes=[pltpu.VMEM((rows, ka), BF16),
                        pltpu.VMEM((rows, LANES), F32), pltpu.VMEM((rows, LANES), F32),
                        pltpu.VMEM((NSA_QPG, tq, HEAD_DIM), F32),
                        pltpu.VMEM((rows, HEAD_DIM), F32), pltpu.VMEM((rows, HEAD_DIM), F32)],
        compiler_params=pltpu.CompilerParams(dimension_semantics=("arbitrary", "arbitrary", "arbitrary"),
                                             vmem_limit_bytes=VMEM_LIMIT),
        name="nsa_prompt",
    )(q4, g4, kcT, vcg, cb, ksT, vs, kwT, vw, sb, wb)


def _gather_pages(pt_ref, cache_ref, sem, layer, feat0, dst_of):
    s = pl.program_id(0)
    n_pages = pt_ref.shape[1]
    slot = lax.rem(s, 2)

    def copy(page, sl, p):
        return pltpu.make_async_copy(cache_ref.at[layer, page, pl.ds(feat0, HALF_WIDTH), :], dst_of(sl, p), sem.at[sl])

    def start(seq, sl):
        for p in range(n_pages):
            copy(pt_ref[seq, p], sl, p).start()

    @pl.when(s == 0)
    def _():
        start(0, 0)

    @pl.when(s + 1 < pl.num_programs(0))
    def _():
        start(s + 1, 1 - slot)

    for p in range(n_pages):
        copy(0, slot, p).wait()
    return slot


def _decode_compress_kernel(pt_ref, cache_ref, w_ref, out_ref, buf, rows_ref, sem, *, layer):
    slot = _gather_pages(pt_ref, cache_ref, sem, layer, 0, lambda sl, p: buf.at[sl, p])
    ncb = out_ref.shape[1]
    n_pages = pt_ref.shape[1]
    pair = 2 * PAGE_SIZE
    blocks_per_pair = pair // L_CMP
    row = lax.broadcasted_iota(jnp.int32, (pair, pair), 0)
    col = lax.broadcasted_iota(jnp.int32, (pair, pair), 1)
    perm = (col == (row % blocks_per_pair) * L_CMP + row // blocks_per_pair).astype(F32).astype(BF16)
    for pp in range(n_pages // 2):
        pages = jnp.concatenate([buf[slot, 2 * pp], buf[slot, 2 * pp + 1]], axis=1).astype(BF16)
        x = _dot_nt(perm, pages)
        for part in range(2):
            for r in range(L_CMP):
                rows_ref[part, r, pp * blocks_per_pair:(pp + 1) * blocks_per_pair, :] = (
                    x[r * blocks_per_pair:(r + 1) * blocks_per_pair, part * KV_WIDTH:(part + 1) * KV_WIDTH])
    for part in range(2):
        acc = jnp.zeros((ncb, KV_WIDTH), F32)
        for r in range(L_CMP):
            acc = acc + jnp.dot(rows_ref[part, r].astype(BF16), w_ref[r, part], preferred_element_type=F32)
        out_ref[0, :, part * KV_WIDTH:(part + 1) * KV_WIDTH] = acc


def decode_compress(page_table, cache, w_bd, layer):
    bs, n_pages = page_table.shape
    past = n_pages * PAGE_SIZE
    ncb = past // L_CMP
    return pl.pallas_call(
        functools.partial(_decode_compress_kernel, layer=layer),
        out_shape=jax.ShapeDtypeStruct((bs, ncb, HALF_WIDTH), F32),
        grid_spec=pltpu.PrefetchScalarGridSpec(
            num_scalar_prefetch=1,
            grid=(bs,),
            in_specs=[pl.BlockSpec(memory_space=pl.ANY),
                      pl.BlockSpec((L_CMP, 2, KV_WIDTH, KV_WIDTH), lambda s, pt: (0, 0, 0, 0))],
            out_specs=pl.BlockSpec((1, ncb, HALF_WIDTH), lambda s, pt: (s, 0, 0)),
            scratch_shapes=[pltpu.VMEM((2, n_pages, HALF_WIDTH, PAGE_SIZE), F32),
                            pltpu.VMEM((2, L_CMP, ncb, KV_WIDTH), F32), pltpu.SemaphoreType.DMA((2,))]),
        compiler_params=pltpu.CompilerParams(dimension_semantics=("arbitrary",), vmem_limit_bytes=VMEM_LIMIT),
        name="decode_compress",
    )(page_table, cache, w_bd)


def _dot_nt(a, b):
    return lax.dot_general(a, b, (((1,), (1,)), ((), ())), preferred_element_type=F32)


def _decode_attend_kernel(pt_ref, cache_ref, q_ref, gate_ref, kcvc_ref, new_ref, win_ref, cb_ref, sb_ref, wb_ref,
                          rel0_ref, onehot_ref, hmask_ref, o_ref, buf, sem, *, layer):
    slot = _gather_pages(pt_ref, cache_ref, sem, layer, HALF_WIDTH,
                         lambda sl, p: buf.at[sl, :, pl.ds(p * PAGE_SIZE, PAGE_SIZE)])
    q = q_ref[0]
    qf = q.astype(F32)
    ncb = kcvc_ref.shape[1]
    rel0 = rel0_ref[:, 0:1]

    def bf_round(x):
        return x.astype(BF16).astype(F32)

    kc = kcvc_ref[0, :, 0:KV_WIDTH].astype(BF16)
    vc = kcvc_ref[0, :, KV_WIDTH:HALF_WIDTH].astype(BF16)
    lg = _dot_nt(q, kc) + cb_ref[...]
    e = jnp.exp(lg - jnp.max(lg, axis=-1, keepdims=True))
    p = e / jnp.sum(e, axis=-1, keepdims=True)
    o_cmp = jnp.dot(p.astype(BF16), vc, preferred_element_type=F32)

    row = lax.broadcasted_iota(jnp.int32, (NSA_HEADS, ncb), 0)
    lane = lax.broadcasted_iota(jnp.int32, (NSA_HEADS, ncb), 1)
    pg0 = jnp.sum(p[0:NSA_QPG], axis=0, keepdims=True)
    pg1 = jnp.sum(p[NSA_QPG:NSA_HEADS], axis=0, keepdims=True)
    imp = jnp.where(row < NSA_QPG, pg0, pg1)
    pair = imp + pltpu.roll(imp, ncb - 1, 1)
    blk = lane >> 1
    forced = (blk == 0) | (blk == ncb // 2 - 1)
    score = jnp.where((lane & 1) == 0, jnp.where(forced, jnp.inf, pair), NEG_INF)
    lane_f = lane.astype(F32)
    chosen = jnp.zeros((NSA_HEADS, ncb), F32)
    for _ in range(N_SEL - 1):
        mx = jnp.max(score, axis=-1, keepdims=True)
        first = jnp.min(jnp.where(score == mx, lane_f, float(ncb)), axis=-1, keepdims=True)
        hit = lane_f == first
        chosen = jnp.where(hit, 1.0, chosen)
        score = jnp.where(hit, NEG_INF, score)
    blockmask = jnp.where(chosen > 0.0, 0.0, MASK_NEG).astype(BF16)

    ks_t = buf[slot, 0:KV_WIDTH, :].astype(BF16)
    vs_t = buf[slot, KV_WIDTH:HALF_WIDTH, :].astype(BF16)
    s = (jnp.dot(q, ks_t, preferred_element_type=F32)
         + jnp.dot(blockmask, onehot_ref[...], preferred_element_type=F32) + sb_ref[...])
    s_new = jnp.sum(qf * bf_round(new_ref[0, 0:1, :]), axis=-1, keepdims=True) + rel0
    m = jnp.maximum(jnp.max(s, axis=-1, keepdims=True), s_new)
    e = jnp.exp(s - m)
    e_new = jnp.exp(s_new - m)
    den = jnp.sum(e, axis=-1, keepdims=True) + e_new
    o_sel = (_dot_nt(e.astype(BF16), vs_t) + e_new * bf_round(new_ref[0, 1:2, :])) / den

    wk_t = win_ref[0, 0, 0:KV_WIDTH, :].astype(BF16)
    wv_t = win_ref[0, 0, KV_WIDTH:HALF_WIDTH, :].astype(BF16)
    sw = jnp.dot(q, wk_t, preferred_element_type=F32) + wb_ref[...]
    sw_new = jnp.sum(qf * bf_round(new_ref[0, 2:3, :]), axis=-1, keepdims=True) + rel0
    mw = jnp.maximum(jnp.max(sw, axis=-1, keepdims=True), sw_new)
    ew = jnp.exp(sw - mw)
    ew_new = jnp.exp(sw_new - mw)
    denw = jnp.sum(ew, axis=-1, keepdims=True) + ew_new
    o_win = (_dot_nt(ew.astype(BF16), wv_t) + ew_new * bf_round(new_ref[0, 3:4, :])) / denw

    g = jax.nn.sigmoid(gate_ref[0])
    o_ref[0] = (g[:, 0:1] * o_cmp + g[:, 1:2] * o_sel + g[:, 2:3] * o_win) * hmask_ref[...]


def decode_tables(rel_bias, past):
    ncb = past // L_CMP
    end = jnp.arange(ncb, dtype=jnp.int32) * L_CMP + (L_CMP - 1)
    cb = rel_bias_lookup(rel_bias, past - end).T
    sb = rel_bias_lookup(rel_bias, past - jnp.arange(past, dtype=jnp.int32)).T
    dist = WINDOW - jnp.arange(WINDOW, dtype=jnp.int32)
    wb = jnp.where((dist < WINDOW)[None, :], rel_bias_lookup(rel_bias, dist).T, MASK_NEG)
    rel0 = jnp.broadcast_to(rel_bias[0][:, None], (NSA_HEADS, LANES))
    onehot = (2 * (jnp.arange(past, dtype=jnp.int32) // L_SEL)[None, :] == jnp.arange(ncb, dtype=jnp.int32)[:, None]).astype(BF16)
    hmask = (jnp.arange(KV_WIDTH, dtype=jnp.int32)[None, :] // HEAD_DIM == jnp.arange(NSA_HEADS, dtype=jnp.int32)[:, None] // NSA_QPG).astype(F32)
    return cb, sb, wb, rel0, onehot, hmask


def compress_block_weights(w_cmp):
    w = w_cmp.reshape(2, L_CMP, HEAD_DIM, HEAD_DIM).transpose(1, 0, 2, 3)
    z = jnp.zeros_like(w)
    return jnp.concatenate([jnp.concatenate([w, z], axis=3), jnp.concatenate([z, w], axis=3)], axis=2).astype(BF16)


def nsa_decode(nq, nkv, ngate, page_table, cache_nsa, cache_win, w_bd, tables, layer):
    bs = nq.shape[0]
    n_pages = page_table.shape[1]
    past = n_pages * PAGE_SIZE
    ncb = past // L_CMP
    cb, sb, wb, rel0, onehot, hmask = tables
    kcvc = decode_compress(page_table, cache_nsa, w_bd, layer)
    qh = nq.reshape(bs, NSA_HEADS, 1, HEAD_DIM) * SCALE
    own_group = jnp.arange(NSA_KV_HEADS)[None, None, :, None] == (jnp.arange(NSA_HEADS) // NSA_QPG)[None, :, None, None]
    qblk = (qh * own_group).reshape(bs, NSA_HEADS, KV_WIDTH).astype(BF16)
    gpad = jnp.pad(ngate.reshape(bs, NSA_HEADS, 3), ((0, 0), (0, 0), (0, LANES - 3)))
    newr = jnp.pad(nkv.reshape(bs, 6, KV_WIDTH)[:, 2:6], ((0, 0), (0, 4), (0, 0)))

    def const2(shape):
        return pl.BlockSpec(shape, lambda s, pt: (0, 0))

    out = pl.pallas_call(
        functools.partial(_decode_attend_kernel, layer=layer),
        out_shape=jax.ShapeDtypeStruct((bs, NSA_HEADS, KV_WIDTH), F32),
        grid_spec=pltpu.PrefetchScalarGridSpec(
            num_scalar_prefetch=1,
            grid=(bs,),
            in_specs=[pl.BlockSpec(memory_space=pl.ANY),
                      pl.BlockSpec((1, NSA_HEADS, KV_WIDTH), lambda s, pt: (s, 0, 0)),
                      pl.BlockSpec((1, NSA_HEADS, LANES), lambda s, pt: (s, 0, 0)),
                      pl.BlockSpec((1, ncb, HALF_WIDTH), lambda s, pt: (s, 0, 0)),
                      pl.BlockSpec((1, 8, KV_WIDTH), lambda s, pt: (s, 0, 0)),
                      pl.BlockSpec((1, 1, HALF_WIDTH, WINDOW), lambda s, pt: (layer, s, 0, 0)),
                      const2((NSA_HEADS, ncb)), const2((NSA_HEADS, past)), const2((NSA_HEADS, WINDOW)),
                      const2((NSA_HEADS, LANES)), const2((ncb, past)), const2((NSA_HEADS, KV_WIDTH))],
            out_specs=pl.BlockSpec((1, NSA_HEADS, KV_WIDTH), lambda s, pt: (s, 0, 0)),
            scratch_shapes=[pltpu.VMEM((2, HALF_WIDTH, past), F32), pltpu.SemaphoreType.DMA((2,))]),
        compiler_params=pltpu.CompilerParams(dimension_semantics=("arbitrary",), vmem_limit_bytes=VMEM_LIMIT),
        name="decode_attend",
    )(page_table, cache_nsa, qblk, gpad, kcvc, newr, cache_win, cb, sb, wb, rel0, onehot, hmask)
    return (out[..., :HEAD_DIM] + out[..., HEAD_DIM:]).reshape(bs, 1, NSA_HEADS * HEAD_DIM)


_PEER_CANDS = [(a, b) for a in range(PEER_TOPK) for b in range(PEER_TOPK) if (a + 1) * (b + 1) <= PEER_TOPK]
_PEER_NCAND = len(_PEER_CANDS)
_PEER_NCAND_PAD = -(-_PEER_NCAND // 8) * 8
_PEER_GROUP_START = [min(c for c, (a, _) in enumerate(_PEER_CANDS) if a == aa) for aa in range(PEER_TOPK)]
_PEER_GROUP_LEN = [sum(1 for (a, _) in _PEER_CANDS if a == aa) for aa in range(PEER_TOPK)]


def _peer_route_kernel(xT_ref, wqT_ref, keys_ref, r2_ref, beta_ref, alpha_ref, lam_ref,
                       qT_ref, s_ref, rk_ref, vals_ref, cand_ref, sel_ref):
    tl = xT_ref.shape[1]
    qT_ref[...] = jnp.dot(wqT_ref[...], xT_ref[...], preferred_element_type=F32).astype(BF16)
    iota_k = lax.broadcasted_iota(jnp.int32, (N_KEYS, tl), 0).astype(F32)

    def rank_top16(p, stable):
        cur = s_ref[p]
        rk = jnp.full((N_KEYS, tl), float(PEER_TOPK), F32)
        for a in range(PEER_TOPK):
            m = jnp.max(cur, axis=0, keepdims=True)
            if stable:
                idx = jnp.min(jnp.where(cur == m, iota_k, float(N_KEYS)), axis=0, keepdims=True)
                hit = iota_k == idx
            else:
                hit = cur == m
            rk = jnp.where(hit, float(a), rk)
            cur = jnp.where(hit, NEG_INF, cur)
            vals_ref[p, a:a + 1, :] = m
        rk_ref[p] = rk
        return jnp.sum(jnp.where(rk < float(PEER_TOPK), 1.0, 0.0), axis=0, keepdims=True)

    def head_body(h, carry):
        taken = jnp.zeros((1, tl), F32)
        for p in range(2):
            row0 = pl.multiple_of(h * PEER_DK + p * (PEER_DK // 2), PEER_DK // 2)
            qs = qT_ref[pl.ds(row0, PEER_DK // 2), :]
            s_ref[p] = jnp.dot(keys_ref[2 * h + p], qs, preferred_element_type=F32)
            taken = jnp.maximum(taken, rank_top16(p, stable=False))

        @pl.when(jnp.max(taken) > float(PEER_TOPK))
        def _():
            for p in range(2):
                rank_top16(p, stable=True)

        for c, (a, b) in enumerate(_PEER_CANDS):
            cand_ref[c:c + 1, :] = vals_ref[0, a:a + 1, :] + vals_ref[1, b:b + 1, :]
        if _PEER_NCAND_PAD > _PEER_NCAND:
            cand_ref[_PEER_NCAND:_PEER_NCAND_PAD, :] = jnp.full((_PEER_NCAND_PAD - _PEER_NCAND, tl), NEG_INF, F32)
        ngrp = _PEER_NCAND_PAD // 8
        iota8 = lax.broadcasted_iota(jnp.int32, (8, tl), 0)
        ranks = [jnp.zeros((8, tl), F32) for _ in range(ngrp)]
        for cp in range(_PEER_NCAND):
            rowb = cand_ref[cp:cp + 1, :]
            for k in range(ngrp):
                blk = cand_ref[8 * k:8 * k + 8, :]
                if 8 * k > cp:
                    inc = jnp.where(rowb >= blk, 1.0, 0.0)
                elif 8 * k + 7 < cp:
                    inc = jnp.where(rowb > blk, 1.0, 0.0)
                else:
                    inc = jnp.where(iota8 + 8 * k > cp, jnp.where(rowb >= blk, 1.0, 0.0), jnp.where(rowb > blk, 1.0, 0.0))
                ranks[k] = ranks[k] + inc
        top = cand_ref[0:1, :]
        z = jnp.zeros((1, tl), F32)
        for k in range(ngrp):
            blk = cand_ref[8 * k:8 * k + 8, :]
            selk = ranks[k] < float(PEER_TOPK)
            sel_ref[8 * k:8 * k + 8, :] = jnp.where(selk, 1.0, 0.0)
            z = z + jnp.sum(jnp.where(selk, jnp.exp(blk - top), 0.0), axis=0, keepdims=True)
        rk1 = rk_ref[0]
        lam = jnp.full((N_KEYS, tl), -1.0, F32)
        for a in range(PEER_TOPK):
            g0, gl = _PEER_GROUP_START[a], _PEER_GROUP_LEN[a]
            la = jnp.sum(sel_ref[g0:g0 + gl, :], axis=0, keepdims=True) - 1.0
            lam = jnp.where(rk1 == float(a), la, lam)
        alpha = jnp.where(rk1 < float(PEER_TOPK), jnp.exp(s_ref[0] - vals_ref[0, 0:1, :]), 0.0)
        rk2 = rk_ref[1]
        beta = jnp.where(rk2 < float(PEER_TOPK), jnp.exp(s_ref[1] - vals_ref[1, 0:1, :]), 0.0) / z
        r2_ref[h] = rk2.astype(BF16)
        beta_ref[h] = beta.astype(BF16)
        alpha_ref[h] = alpha
        lam_ref[h] = lam
        return carry

    lax.fori_loop(0, PEER_HEADS, head_body, 0)


def peer_route(xT, wqT, keys2):
    d, n = xT.shape
    tl = PEER_ROUTE_TL if n % PEER_ROUTE_TL == 0 else LANES
    assert n % tl == 0
    nq = PEER_HEADS * PEER_DK
    out_bf = jax.ShapeDtypeStruct((PEER_HEADS, N_KEYS, n), BF16)
    out_f = jax.ShapeDtypeStruct((PEER_HEADS, N_KEYS, n), F32)
    tab_spec = pl.BlockSpec((PEER_HEADS, N_KEYS, tl), lambda i: (0, 0, i))
    return pl.pallas_call(
        _peer_route_kernel,
        out_shape=(out_bf, out_bf, out_f, out_f),
        grid=(n // tl,),
        in_specs=[pl.BlockSpec((d, tl), lambda i: (0, i)),
                  pl.BlockSpec((nq, d), lambda i: (0, 0)),
                  pl.BlockSpec((2 * PEER_HEADS, N_KEYS, PEER_DK // 2), lambda i: (0, 0, 0))],
        out_specs=(tab_spec, tab_spec, tab_spec, tab_spec),
        scratch_shapes=[pltpu.VMEM((nq, tl), BF16),
                        pltpu.VMEM((2, N_KEYS, tl), F32),
                        pltpu.VMEM((2, N_KEYS, tl), F32),
                        pltpu.VMEM((2, PEER_TOPK, tl), F32),
                        pltpu.VMEM((_PEER_NCAND_PAD, tl), F32),
                        pltpu.VMEM((_PEER_NCAND_PAD, tl), F32)],
        compiler_params=pltpu.CompilerParams(dimension_semantics=("arbitrary",), vmem_limit_bytes=VMEM_LIMIT),
        name="peer_route",
    )(xT, wqT, keys2)


def _gelu_tanh(x):
    return 0.5 * x * (1.0 + jnp.tanh(math.sqrt(2.0 / math.pi) * (x + 0.044715 * (x * x * x))))


def _peer_dense_kernel(xT_ref, u_ref, vT_ref, r2_ref, beta_ref, alpha_ref, lam_ref, yT_ref, a_ref, h_ref):
    j = pl.program_id(1)
    te = u_ref.shape[0]
    tl = xT_ref.shape[1]

    @pl.when(j == 0)
    def _():
        yT_ref[...] = jnp.zeros_like(yT_ref)

    n_sub = te // N_KEYS

    def pre_activation(r):
        rows = slice(r * N_KEYS, (r + 1) * N_KEYS)
        a_ref[rows, :] = jnp.dot(u_ref[rows, :], xT_ref[...], preferred_element_type=F32)

    pre_activation(0)
    for r in range(n_sub):
        rows = slice(r * N_KEYS, (r + 1) * N_KEYS)
        if r + 1 < n_sub:
            pre_activation(r + 1)
        g = jnp.zeros((N_KEYS, tl), BF16)
        for h in range(PEER_HEADS):
            lam = lam_ref[h, r:r + 1, :].astype(BF16)
            alp = alpha_ref[h, r:r + 1, :].astype(BF16)
            g = g + jnp.where(r2_ref[h] <= lam, beta_ref[h], jnp.zeros((), BF16)) * alp
        h_ref[rows, :] = _gelu_tanh(a_ref[rows, :].astype(BF16)) * g
    yT_ref[...] += jnp.dot(vT_ref[...], h_ref[...], preferred_element_type=F32)


def peer_dense(xT, u_bf, vT_bf, r2, beta, alpha, lam, tl, te):
    d, n = xT.shape
    e = u_bf.shape[0]
    assert n % tl == 0 and e % te == 0 and te % (8 * N_KEYS) == 0
    tab_spec = pl.BlockSpec((PEER_HEADS, N_KEYS, tl), lambda i, j: (0, 0, i))
    row_spec = pl.BlockSpec((PEER_HEADS, te // N_KEYS, tl), lambda i, j: (0, j, i))
    return pl.pallas_call(
        _peer_dense_kernel,
        out_shape=jax.ShapeDtypeStruct((d, n), F32),
        grid=(n // tl, e // te),
        in_specs=[pl.BlockSpec((d, tl), lambda i, j: (0, i)),
                  pl.BlockSpec((te, d), lambda i, j: (j, 0)),
                  pl.BlockSpec((d, te), lambda i, j: (0, j)),
                  tab_spec, tab_spec, row_spec, row_spec],
        out_specs=pl.BlockSpec((d, tl), lambda i, j: (0, i)),
        scratch_shapes=[pltpu.VMEM((te, tl), F32), pltpu.VMEM((te, tl), BF16)],
        compiler_params=pltpu.CompilerParams(dimension_semantics=("arbitrary", "arbitrary"), vmem_limit_bytes=VMEM_LIMIT),
        name="peer_dense",
    )(xT, u_bf, vT_bf, r2, beta, alpha, lam)


def peer_ffn_t(xT, peer_w):
    wqT, keys2, u_bf, vT_bf = peer_w
    n = xT.shape[1]
    tl = PEER_TL if n % PEER_TL == 0 else LANES
    r2, beta, alpha, lam = peer_route(xT, wqT, keys2)
    return peer_dense(xT, u_bf, vT_bf, r2, beta, alpha, lam, tl, PEER_TE)


def peer_ffn(h, peer_w):
    b, t, d = h.shape
    n = b * t
    npad = -(-n // LANES) * LANES
    xT = jnp.pad(h.reshape(n, d).astype(BF16).T, ((0, 0), (0, npad - n)))
    return peer_ffn_t(xT, peer_w).T[:n].reshape(b, t, d)


POST_TM = 512
_POST_ROWS = 8


def _post_kernel(x_ref, y_ref, vec_ref, ox_ref, *oh_ref, alpha, y_transposed, h_transposed):
    y = y_ref[...]
    if y_transposed:
        y = y.T
    vec = vec_ref[0]
    r = alpha * x_ref[...] + vec[0:1] * y
    mu = jnp.mean(r, axis=-1, keepdims=True)
    rc = r - mu
    var = jnp.mean(rc * rc, axis=-1, keepdims=True)
    xn = rc * lax.rsqrt(var + LN_EPS) * vec[3:4] + vec[4:5]
    ox_ref[...] = xn
    if oh_ref:
        hm = xn * vec[1:2] + vec[2:3]
        oh_ref[0][...] = (hm.T if h_transposed else hm).astype(BF16)


def residual_norm_modulate(x, y, gate, ln_g, ln_b, scale, shift, alpha, tokens_per_batch, y_transposed, h_transposed):
    n, d = x.shape
    tm = POST_TM
    assert n % tm == 0 and tokens_per_batch % tm == 0
    nb = gate.shape[0]
    want_h = scale is not None
    if not want_h:
        scale = shift = jnp.zeros_like(gate)
    vec = jnp.stack([gate, scale, shift, jnp.broadcast_to(ln_g, (nb, d)), jnp.broadcast_to(ln_b, (nb, d))], axis=1)
    vec = jnp.pad(vec, ((0, 0), (0, _POST_ROWS - vec.shape[1]), (0, 0)))
    tiles_per_batch = tokens_per_batch // tm
    y_spec = pl.BlockSpec((d, tm), lambda i: (0, i)) if y_transposed else pl.BlockSpec((tm, d), lambda i: (i, 0))
    out_shape = [jax.ShapeDtypeStruct((n, d), F32)]
    out_specs = [pl.BlockSpec((tm, d), lambda i: (i, 0))]
    if want_h:
        out_shape.append(jax.ShapeDtypeStruct((d, n) if h_transposed else (n, d), BF16))
        out_specs.append(pl.BlockSpec((d, tm), lambda i: (0, i)) if h_transposed else pl.BlockSpec((tm, d), lambda i: (i, 0)))
    outs = pl.pallas_call(
        functools.partial(_post_kernel, alpha=alpha, y_transposed=y_transposed, h_transposed=h_transposed),
        out_shape=tuple(out_shape),
        grid=(n // tm,),
        in_specs=[pl.BlockSpec((tm, d), lambda i: (i, 0)), y_spec,
                  pl.BlockSpec((1, _POST_ROWS, d), lambda i: (i // tiles_per_batch, 0, 0))],
        out_specs=tuple(out_specs),
        compiler_params=pltpu.CompilerParams(dimension_semantics=("arbitrary",), vmem_limit_bytes=VMEM_LIMIT),
        name="residual_norm_modulate",
    )(x, y, vec)
    return (outs[0], outs[1]) if want_h else (outs[0], None)


def token_mixers(parts, pos, rel_table, w_cmp, conv_w, conv_b, ret_g, past, nsa_tables):
    nq, nkv, ngate, rq, rk, rv, rg, cb, cc, ch = parts
    b, t = nq.shape[:2]
    q = nq.reshape(b, t, NSA_KV_HEADS, NSA_QPG, HEAD_DIM)
    kv = nkv.reshape(b, t, 6, NSA_KV_HEADS, HEAD_DIM)
    if past is None:
        kc = compress(kv[:, :, 0], w_cmp[0])
        vc = compress(kv[:, :, 1], w_cmp[1])
        o_nsa = nsa_prompt(nq, kv, ngate, kc, vc, nsa_tables)
        win_rows = kv[:, t - min(WINDOW, t):, 4:]
        zbuf = jnp.zeros((b, CONV_W - 1, CONV_CH), ch.dtype)
        o_ret, s_new = retention_prompt(rq, rk, rv, rg, ret_g)
    else:
        decode_nsa, win_buf, s0, zbuf = past
        o_nsa = decode_nsa(nq, nkv, ngate)
        wrows = jnp.concatenate([win_buf.astype(kv.dtype), kv[:, :, 4:]], axis=1)
        win_rows = wrows[:, wrows.shape[1] - min(WINDOW, wrows.shape[1]):]
        rqh = rotary(rq.reshape(b, t, RET_HEADS, RET_DK), pos)
        rkh = rotary(rk.reshape(b, t, RET_HEADS, RET_DK), pos) * (RET_DK ** -0.5)
        rvh = rv.reshape(b, t, RET_HEADS, RET_DV)
        o_r, s_new = retention(rqh, rkh, rvh, s0, t)
        mu = jnp.mean(o_r, -1, keepdims=True)
        var = jnp.mean(jnp.square(o_r - mu), -1, keepdims=True)
        on = (o_r - mu) * lax.rsqrt(var + LN_EPS) * ret_g.reshape(RET_HEADS, RET_DV)
        o_ret = (on.reshape(b, t, RET_WIDTH) * jax.nn.silu(rg.astype(jnp.float32))).astype(nq.dtype)
    z = cc * ch
    zp = jnp.concatenate([zbuf.astype(z.dtype), z], axis=1)
    y = conv_b + sum(zp[:, j:j + t] * conv_w[j] for j in range(CONV_W))
    o_conv = (cb * y).astype(nq.dtype)
    mix = jnp.concatenate([o_nsa, o_ret, o_conv], axis=-1)
    return mix, (kv[:, :, :4], win_rows, s_new, zp[:, t:])


def kernel(x_prompt, x_sample, cache_nsa_kv, cache_win_kv, state_ret, state_conv, page_table, c_prompt, c_sample, rel_bias, w_ada, b_ada, w_in, w_cmp, conv_w, conv_b, ret_norm_g, w_out, ln1_g, ln1_b, ln2_g, ln2_b, peer_wq, peer_keys, peer_u, peer_v):
    alpha = (2.0 * DEPTH) ** 0.25
    n_pages = page_table.shape[1]
    past_len = n_pages * PAGE_SIZE
    sp = x_prompt.shape[1]
    bs, ts = x_sample.shape[:2]
    pos_p = jnp.arange(sp, dtype=jnp.int32)
    pos_s = past_len + jnp.arange(ts, dtype=jnp.int32)
    nsa_tables = nsa_bias_tables(rel_bias, sp, NSA_TQ)
    dec_tables = decode_tables(rel_bias, past_len)
    cache_nsa = cache_nsa_kv.transpose(0, 1, 3, 4, 5, 2).reshape(cache_nsa_kv.shape[0], cache_nsa_kv.shape[1], ROW_WIDTH, PAGE_SIZE)
    cache_win = cache_win_kv.transpose(0, 1, 3, 4, 5, 2).reshape(cache_win_kv.shape[0], bs, HALF_WIDTH, cache_win_kv.shape[2])

    def ada(c, l):
        return (jax.nn.silu(c) @ w_ada[l] + b_ada[l]).reshape(c.shape[0], 6, D_MODEL)

    def run_layer(x, c, l, pos, past, peer_w):
        m = ada(c, l)[:, :, None, :]
        h = x * (1.0 + m[:, 1]) + m[:, 0]
        parts = split_proj(h @ w_in[l])
        mix, st = token_mixers(parts, pos, rel_bias, w_cmp[l], conv_w[l], conv_b[l], ret_norm_g[l], past, nsa_tables)
        x = layer_norm(alpha * x + (1.0 + m[:, 2]) * (mix @ w_out[l]), ln1_g[l], ln1_b[l])
        h = x * (1.0 + m[:, 4]) + m[:, 3]
        y = peer_ffn(h, peer_w)
        x = layer_norm(alpha * x + (1.0 + m[:, 5]) * y, ln2_g[l], ln2_b[l])
        return x, st

    bp = x_prompt.shape[0]
    m_prompt = [ada(c_prompt, l) for l in range(DEPTH)]

    def run_prompt_layer(x2, h, l, peer_w):
        m = m_prompt[l]
        parts = split_proj(jnp.matmul(h, w_in[l].astype(h.dtype), preferred_element_type=F32))
        mix, st = token_mixers(parts, pos_p, rel_bias, w_cmp[l], conv_w[l], conv_b[l], ret_norm_g[l], None, nsa_tables)
        y = (mix @ w_out[l]).reshape(bp * sp, D_MODEL)
        x2, h_t = residual_norm_modulate(x2, y, 1.0 + m[:, 2], ln1_g[l], ln1_b[l], 1.0 + m[:, 4], m[:, 3], alpha, sp,
                                         y_transposed=False, h_transposed=True)
        y_t = peer_ffn_t(h_t, peer_w)
        if l + 1 < DEPTH:
            nxt = m_prompt[l + 1]
            x2, h = residual_norm_modulate(x2, y_t, 1.0 + m[:, 5], ln2_g[l], ln2_b[l], 1.0 + nxt[:, 1], nxt[:, 0], alpha, sp,
                                           y_transposed=True, h_transposed=False)
            h = h.reshape(bp, sp, D_MODEL)
        else:
            x2, h = residual_norm_modulate(x2, y_t, 1.0 + m[:, 5], ln2_g[l], ln2_b[l], None, None, alpha, sp,
                                           y_transposed=True, h_transposed=False)
        return x2, h, st

    xp2 = x_prompt.reshape(bp * sp, D_MODEL)
    hp = x_prompt * (1.0 + m_prompt[0][:, 1][:, None, :]) + m_prompt[0][:, 0][:, None, :]
    xs = x_sample
    nkv_p, nkv_s, win_p, win_s, ret_p, ret_s, conv_p, conv_s = [], [], [], [], [], [], [], []
    for l in range(DEPTH):
        peer_w = (peer_wq[l].T.astype(BF16),
                  peer_keys[l].reshape(2 * PEER_HEADS, N_KEYS, PEER_DK // 2).astype(BF16),
                  peer_u[l].astype(BF16),
                  peer_v[l].T.astype(BF16))
        xp2, hp, st_p = run_prompt_layer(xp2, hp, l, peer_w)
        decode_nsa = functools.partial(nsa_decode, page_table=page_table, cache_nsa=cache_nsa, cache_win=cache_win,
                                       w_bd=compress_block_weights(w_cmp[l]), tables=dec_tables, layer=l)
        xs, st_s = run_layer(xs, c_sample, l, pos_s, (decode_nsa, cache_win_kv[l], state_ret[l], state_conv[l]), peer_w)
        nkv_p.append(st_p[0])
        win_p.append(st_p[1])
        ret_p.append(st_p[2])
        conv_p.append(st_p[3])
        nkv_s.append(st_s[0])
        win_s.append(st_s[1])
        ret_s.append(st_s[2])
        conv_s.append(st_s[3])
    xp = xp2.reshape(bp, sp, D_MODEL)
    return (xp, xs, jnp.stack(nkv_p), jnp.stack(nkv_s), jnp.stack(win_p), jnp.stack(win_s), jnp.stack(ret_p), jnp.stack(ret_s), jnp.stack(conv_p), jnp.stack(conv_s))
```

```python
import math
import functools
import jax
import jax.numpy as jnp
from jax import lax
import numpy as np
from jax.experimental import pallas as pl
from jax.experimental.pallas import tpu as pltpu

D_MODEL = 1024
BATCH = 4
SEQ = 4096
DEPTH = 2
DEC_BATCH = 32
DEC_SEQ = 1
PAST_LEN = 8192
PAGE_SIZE = 128

HEAD_DIM = 64
NSA_WIDTH = D_MODEL // 2
NSA_HEADS = NSA_WIDTH // HEAD_DIM
NSA_KV_HEADS = 2
NSA_QPG = NSA_HEADS // NSA_KV_HEADS
KV_WIDTH = NSA_KV_HEADS * HEAD_DIM
SCALE = HEAD_DIM ** -0.5
L_CMP = 32
L_SEL = 64
N_SEL = 16
WINDOW = 512
Q_BLOCK = 128
SEL_Q_BLOCK = 64
N_BUCKETS = 32
REL_MAX_DIST = 128
RET_WIDTH = D_MODEL // 4
RET_DK = 64
RET_DV = 64
RET_HEADS = RET_WIDTH // RET_DV
RET_CHUNK = 128
ROPE_BASE = 10000.0
CONV_CH = D_MODEL // 4
CONV_W = 3
MIX_WIDTH = NSA_WIDTH + RET_WIDTH + CONV_CH
PEER_HEADS = 8
PEER_DK = 256
N_KEYS = 128
N_EXPERTS = N_KEYS * N_KEYS
PEER_TOPK = 16
PEER_CHUNK = 256
LN_EPS = 1e-5
SPLIT_SIZES = (NSA_WIDTH, 6 * KV_WIDTH, 3 * NSA_HEADS, RET_HEADS * RET_DK, RET_HEADS * RET_DK, RET_WIDTH, RET_WIDTH, CONV_CH, CONV_CH, CONV_CH)
N_IN = sum(SPLIT_SIZES)

F32 = jnp.float32
BF16 = jnp.bfloat16
NEG_INF = float('-inf')
MASK_NEG = -1e9
CMP_MASK_NEG = -1e30
NSA_TQ = 256
NSA_RB = 128
NSA_FAR_UNROLL = 4
ROW_WIDTH = 4 * KV_WIDTH
HALF_WIDTH = 2 * KV_WIDTH
PEER_TL = 512
PEER_TE = 4096
PEER_ROUTE_TL = 512
LANES = 128
VMEM_LIMIT = 56 * 1024 * 1024


def _ln_kernel(x_ref, g_ref, b_ref, o_ref):
    x = x_ref[...]
    mu = jnp.mean(x, -1, keepdims=True)
    xc = x - mu
    var = jnp.mean(xc * xc, -1, keepdims=True)
    o_ref[...] = xc * lax.rsqrt(var + LN_EPS) * g_ref[...] + b_ref[...]


def layer_norm(x, g, b):
    shp = x.shape
    x2 = x.reshape(-1, shp[-1])
    n = x2.shape[0]
    tm = min(n, 512)
    out = pl.pallas_call(
        _ln_kernel,
        out_shape=jax.ShapeDtypeStruct(x2.shape, jnp.float32),
        grid=(n // tm,),
        in_specs=[pl.BlockSpec((tm, shp[-1]), lambda i: (i, 0)),
                  pl.BlockSpec((1, shp[-1]), lambda i: (0, 0)),
                  pl.BlockSpec((1, shp[-1]), lambda i: (0, 0))],
        out_specs=pl.BlockSpec((tm, shp[-1]), lambda i: (i, 0)),
        name="layer_norm",
    )(x2, g.reshape(1, -1), b.reshape(1, -1))
    return out.reshape(shp)


def t5_bucket(dist):
    n = jnp.maximum(dist, 0)
    exact = N_BUCKETS // 2
    nf = jnp.maximum(n, exact).astype(jnp.float32)
    big = exact + (jnp.log(nf / exact) / math.log(REL_MAX_DIST / exact) * (N_BUCKETS - exact)).astype(jnp.int32)
    return jnp.where(n < exact, n, jnp.minimum(big, N_BUCKETS - 1))


def rel_bias_lookup(rel_bias, dist):
    onehot = jax.nn.one_hot(t5_bucket(dist), N_BUCKETS, dtype=jnp.float32)
    return jnp.einsum('...k,kh->...h', onehot, rel_bias, precision=lax.Precision.HIGHEST)


def rotary(x, pos):
    half = x.shape[-1] // 2
    inv = ROPE_BASE ** (-jnp.arange(half, dtype=jnp.float32) / half)
    ang = pos.astype(jnp.float32)[:, None] * inv[None, :]
    cos = jnp.cos(ang)[None, :, None, :]
    sin = jnp.sin(ang)[None, :, None, :]
    x1, x2 = x[..., :half], x[..., half:]
    return jnp.concatenate([x1 * cos - x2 * sin, x1 * sin + x2 * cos], axis=-1)


def split_proj(p):
    outs, start = [], 0
    for size in SPLIT_SIZES:
        outs.append(p[..., start:start + size])
        start += size
    return outs


def compress(rows, w):
    b, lp, g, d = rows.shape
    blk = rows.reshape(b, lp // L_CMP, L_CMP, g, d).transpose(0, 1, 3, 2, 4).reshape(b, lp // L_CMP, g, L_CMP * d)
    return blk @ w


def retention(q, k, v, s0, chunk):
    b, t = q.shape[:2]
    nc = t // chunk
    lg = jnp.log(1.0 - 2.0 ** (-5.0 - jnp.arange(RET_HEADS, dtype=jnp.float32)))
    i = jnp.arange(chunk, dtype=jnp.float32)
    diff = i[:, None] - i[None, :]
    dmat = jnp.where(diff >= 0, jnp.exp(jnp.maximum(diff, 0.0)[None] * lg[:, None, None]), 0.0)
    q_dec = jnp.exp((i + 1.0)[:, None] * lg[None, :])[None, :, :, None]
    k_dec = jnp.exp((chunk - 1.0 - i)[:, None] * lg[None, :])[None, :, :, None]
    s_dec = jnp.exp(chunk * lg)[None, :, None, None]

    def to_chunks(a):
        return jnp.moveaxis(a.astype(jnp.float32).reshape(b, nc, chunk, RET_HEADS, a.shape[-1]), 1, 0)

    def step(s, xs):
        qc, kc, vc = xs
        att = jnp.einsum('bihd,bjhd->bhij', qc, kc) * dmat
        o = jnp.einsum('bhij,bjhe->bihe', att, vc) + jnp.einsum('bihd,bhde->bihe', qc * q_dec, s)
        s = s * s_dec + jnp.einsum('bjhd,bjhe->bhde', kc * k_dec, vc)
        return s, o

    s, o = lax.scan(step, s0.astype(jnp.float32), (to_chunks(q), to_chunks(k), to_chunks(v)))
    return jnp.moveaxis(o, 0, 1).reshape(b, t, RET_HEADS, RET_DV), s


def _retention_kernel(q_ref, k_ref, v_ref, g_ref, cos_ref, sin_ref, dmat_ref, qdec_ref, kdec_ref, sdec_ref, gain_ref,
                      o_ref, sout_ref, s_scr):
    c = pl.program_id(1)

    @pl.when(c == 0)
    def _():
        s_scr[...] = jnp.zeros_like(s_scr)

    chunk, width = q_ref.shape[1], q_ref.shape[2]
    half = RET_DK // 2
    lane = lax.broadcasted_iota(jnp.int32, (chunk, width), 1)
    first_half = (lane % RET_DK) < half

    def rot(x):
        swapped = jnp.where(first_half, pltpu.roll(x, width - half, 1), pltpu.roll(x, half, 1))
        return x * cos_ref[...] + swapped * sin_ref[...]

    q = rot(q_ref[0])
    k = rot(k_ref[0]) * (RET_DK ** -0.5)
    v = v_ref[0]
    gate = g_ref[0]
    eye = (lax.broadcasted_iota(jnp.int32, (RET_DK, RET_DK), 0)
           == lax.broadcasted_iota(jnp.int32, (RET_DK, RET_DK), 1)).astype(F32).astype(BF16)
    for h in range(RET_HEADS):
        sl = slice(h * RET_DK, (h + 1) * RET_DK)
        qh, kh, vh = q[:, sl], k[:, sl], v[:, sl].astype(BF16)
        att = _dot_nt(qh.astype(BF16), kh.astype(BF16)) * dmat_ref[h]
        s_old = s_scr[h]
        o = (jnp.dot(att.astype(BF16), vh, preferred_element_type=F32)
             + jnp.dot((qh * qdec_ref[:, sl]).astype(BF16), s_old.astype(BF16), preferred_element_type=F32))
        kd_t = _dot_nt(eye, (kh * kdec_ref[:, sl]).astype(BF16)).astype(BF16)
        s_scr[h] = s_old * sdec_ref[h, 0:1, 0:RET_DV] + jnp.dot(kd_t, vh, preferred_element_type=F32)
        mu = jnp.mean(o, axis=-1, keepdims=True)
        oc = o - mu
        var = jnp.mean(oc * oc, axis=-1, keepdims=True)
        gh = gate[:, sl]
        o_ref[0, :, sl] = oc * lax.rsqrt(var + LN_EPS) * gain_ref[:, sl] * (gh * jax.nn.sigmoid(gh))
    sout_ref[0] = s_scr[...]


def retention_prompt(rq, rk, rv, rg, ret_g):
    b, t, width = rq.shape
    chunk = RET_CHUNK
    half = RET_DK // 2
    pos = jnp.arange(t, dtype=jnp.float32)
    inv = ROPE_BASE ** (-jnp.arange(half, dtype=jnp.float32) / half)
    ang = pos[:, None] * inv[None, :]
    cos = jnp.tile(jnp.cos(ang), (1, 2 * RET_HEADS))
    sin = jnp.tile(jnp.concatenate([-jnp.sin(ang), jnp.sin(ang)], axis=1), (1, RET_HEADS))
    lg = jnp.log(1.0 - 2.0 ** (-5.0 - jnp.arange(RET_HEADS, dtype=jnp.float32)))
    i = jnp.arange(chunk, dtype=jnp.float32)
    diff = i[:, None] - i[None, :]
    dmat = jnp.where(diff >= 0, jnp.exp(jnp.maximum(diff, 0.0)[None] * lg[:, None, None]), 0.0)
    qdec = jnp.repeat(jnp.exp((i + 1.0)[:, None] * lg[None, :]), RET_DK, axis=1)
    kdec = jnp.repeat(jnp.exp((chunk - 1.0 - i)[:, None] * lg[None, :]), RET_DK, axis=1)
    sdec = jnp.broadcast_to(jnp.exp(chunk * lg)[:, None, None], (RET_HEADS, 8, LANES))
    tok = pl.BlockSpec((1, chunk, width), lambda bi, ci: (bi, ci, 0))
    tab = pl.BlockSpec((chunk, width), lambda bi, ci: (ci, 0))

    def const(shape):
        return pl.BlockSpec(shape, lambda bi, ci: (0,) * len(shape))

    return pl.pallas_call(
        _retention_kernel,
        out_shape=(jax.ShapeDtypeStruct((b, t, width), F32),
                   jax.ShapeDtypeStruct((b, RET_HEADS, RET_DK, RET_DV), F32)),
        grid=(b, t // chunk),
        in_specs=[tok, tok, tok, tok, tab, tab, const((RET_HEADS, chunk, chunk)), const((chunk, width)),
                  const((chunk, width)), const((RET_HEADS, 8, LANES)), const((1, width))],
        out_specs=(tok, pl.BlockSpec((1, RET_HEADS, RET_DK, RET_DV), lambda bi, ci: (bi, 0, 0, 0))),
        scratch_shapes=[pltpu.VMEM((RET_HEADS, RET_DK, RET_DV), F32)],
        compiler_params=pltpu.CompilerParams(dimension_semantics=("arbitrary", "arbitrary"), vmem_limit_bytes=VMEM_LIMIT),
        name="retention_prompt",
    )(rq, rk, rv, rg, cos, sin, dmat, qdec, kdec, sdec, ret_g.reshape(1, width))


def _nsa_prompt_kernel(q_ref, gate_ref, kcT_ref, vc_ref, cb_ref, ksT_ref, vs_ref, kwT_ref, vw_ref, sb_ref, wb_ref,
                       o_ref, qa_ref, m_ref, acc_ref, oc_ref, osel_ref, owin_ref):
    qi = pl.program_id(2)
    tq = q_ref.shape[3]
    ncb = kcT_ref.shape[3]
    ka = ksT_ref.shape[2]
    t0 = qi * tq
    rows = NSA_QPG * tq

    imp = jnp.zeros((tq, ncb), F32)
    for h in range(NSA_QPG):
        bias = cb_ref[0, h]
        lg = jnp.dot(q_ref[0, 0, h], kcT_ref[0, 0], preferred_element_type=F32) + bias
        mx = jnp.max(lg, axis=-1, keepdims=True)
        e = jnp.where(bias > 0.5 * CMP_MASK_NEG, jnp.exp(lg - mx), 0.0)
        p = e / jnp.maximum(jnp.sum(e, axis=-1, keepdims=True), 1e-30)
        imp = imp + p
        oc_ref[h] = jnp.dot(p.astype(BF16), vc_ref[0, 0], preferred_element_type=F32)

    nrb = rows // NSA_RB
    per_head = tq // NSA_RB

    def reset():
        m_ref[...] = jnp.full((rows, LANES), CMP_MASK_NEG, F32)
        acc_ref[...] = jnp.zeros((rows, LANES), F32)

    def attn_step(rb, qrb, k_t, v, bias):
        r0 = rb * NSA_RB
        s = jnp.dot(qrb, k_t, preferred_element_type=F32)
        if bias is not None:
            s = s + bias
        parts = [s[:, i * LANES:(i + 1) * LANES] for i in range(s.shape[1] // LANES)]
        red = parts[0]
        for part in parts[1:]:
            red = jnp.maximum(red, part)
        m_old = m_ref[r0:r0 + NSA_RB, :]
        m_new = jnp.maximum(m_old, jnp.max(red, axis=-1, keepdims=True))
        p = jnp.concatenate([jnp.exp(part - m_new) for part in parts], axis=1).astype(BF16)
        acc_ref[r0:r0 + NSA_RB, :] = (jnp.exp(m_old - m_new) * acc_ref[r0:r0 + NSA_RB, :]
                                      + jnp.dot(p, v, preferred_element_type=F32))
        m_ref[r0:r0 + NSA_RB, :] = m_new

    def finish(dst_ref):
        acc = acc_ref[...]
        dst_ref[...] = acc[:, 0:HEAD_DIM] / acc[:, HEAD_DIM:HEAD_DIM + 1]

    col0 = pl.multiple_of(t0, tq)

    reset()
    for c in range((WINDOW + tq) // tq):
        colc = pl.multiple_of(col0 + c * tq, tq)
        k_t = kwT_ref[0, 0, :, pl.ds(colc, tq)]
        v = vw_ref[0, 0, pl.ds(colc, tq), :]
        jw = lax.broadcasted_iota(jnp.int32, (1, tq), 1) + c * tq
        wmask = jnp.where(jw + (t0 - WINDOW) >= 0, 0.0, MASK_NEG)
        for rb in range(nrb):
            h, part = rb // per_head, rb % per_head
            bias = wb_ref[0, h, part * NSA_RB:(part + 1) * NSA_RB, c * tq:(c + 1) * tq] + wmask
            attn_step(rb, q_ref[0, 0, h, part * NSA_RB:(part + 1) * NSA_RB, :], k_t, v, bias)
    finish(owin_ref)

    lane = lax.broadcasted_iota(jnp.int32, (tq, ncb), 1)
    tpos = t0 + lax.broadcasted_iota(jnp.int32, (tq, ncb), 0)
    pair = imp + pltpu.roll(imp, ncb - 1, 1)
    blk = lane >> 1
    cur = tpos >> 6
    forced = (blk == 0) | (blk == cur) | (blk == cur - 1)
    cand = ((lane & 1) == 0) & (blk <= cur)
    score = jnp.where(cand, jnp.where(forced, jnp.inf, pair), NEG_INF)
    lane_f = lane.astype(F32)
    chosen = jnp.zeros((tq, ncb), F32)
    for _ in range(N_SEL):
        mx = jnp.max(score, axis=-1, keepdims=True)
        first = jnp.min(jnp.where(score == mx, lane_f, float(ncb)), axis=-1, keepdims=True)
        hit = (lane_f == first) & (mx > NEG_INF)
        chosen = jnp.where(hit, 1.0, chosen)
        score = jnp.where(hit, NEG_INF, score)
    blockmask = jnp.where(chosen > 0.0, 0.0, MASK_NEG).astype(BF16)

    for h in range(NSA_QPG):
        qa_ref[h * tq:(h + 1) * tq, 0:ncb] = blockmask
        qa_ref[h * tq:(h + 1) * tq, ncb:ncb + HEAD_DIM] = q_ref[0, 0, h]
        if ka > ncb + HEAD_DIM:
            qa_ref[h * tq:(h + 1) * tq, ncb + HEAD_DIM:ka] = jnp.zeros((tq, ka - ncb - HEAD_DIM), BF16)

    reset()

    def far_chunk(c):
        col = pl.multiple_of((c + 1) * tq, tq)
        k_t = ksT_ref[0, 0, :, pl.ds(col, tq)]
        v = vs_ref[0, 0, pl.ds(col, tq), :]
        for rb in range(nrb):
            attn_step(rb, qa_ref[rb * NSA_RB:(rb + 1) * NSA_RB, :], k_t, v, None)

    def far_group(c, carry):
        for i in range(NSA_FAR_UNROLL):
            far_chunk(NSA_FAR_UNROLL * c + i)
        return carry

    n_far = jnp.maximum(qi - 1, 0)
    lax.fori_loop(0, n_far // NSA_FAR_UNROLL, far_group, 0)
    done = (n_far // NSA_FAR_UNROLL) * NSA_FAR_UNROLL
    width = NSA_FAR_UNROLL // 2
    while width >= 1:
        @pl.when(((n_far - done) & width) != 0)
        def _(done=done, width=width):
            for i in range(width):
                far_chunk(done + i)
        done = done + ((n_far - done) & width)
        width //= 2

    for c in range(2):
        colc = pl.multiple_of(col0 + c * tq, tq)
        k_t = ksT_ref[0, 0, :, pl.ds(colc, tq)]
        v = vs_ref[0, 0, pl.ds(colc, tq), :]
        jn = lax.broadcasted_iota(jnp.int32, (1, tq), 1) + c * tq
        colmask = jnp.where(jn + (t0 - tq) >= 0, 0.0, MASK_NEG)
        for rb in range(nrb):
            h, part = rb // per_head, rb % per_head
            bias = sb_ref[0, h, part * NSA_RB:(part + 1) * NSA_RB, c * tq:(c + 1) * tq] + colmask
            attn_step(rb, qa_ref[rb * NSA_RB:(rb + 1) * NSA_RB, :], k_t, v, bias)
    finish(osel_ref)

    g = jax.nn.sigmoid(gate_ref[0, 0])
    for h in range(NSA_QPG):
        o_h = (g[:, 3 * h:3 * h + 1] * oc_ref[h]
               + g[:, 3 * h + 1:3 * h + 2] * osel_ref[h * tq:(h + 1) * tq, :]
               + g[:, 3 * h + 2:3 * h + 3] * owin_ref[h * tq:(h + 1) * tq, :])
        o_ref[0, :, h * HEAD_DIM:(h + 1) * HEAD_DIM] = o_h


def nsa_bias_tables(rel_bias, t, tq):
    ncb = t // L_CMP

    def heads_first(tab):
        return tab.transpose(2, 0, 1).reshape(NSA_KV_HEADS, NSA_QPG, tab.shape[0], tab.shape[1])

    pos = jnp.arange(t, dtype=jnp.int32)
    end = jnp.arange(ncb, dtype=jnp.int32) * L_CMP + (L_CMP - 1)
    dist = pos[:, None] - end[None, :]
    cb = jnp.where((dist >= 0)[..., None], rel_bias_lookup(rel_bias, dist), CMP_MASK_NEG)
    i = jnp.arange(tq, dtype=jnp.int32)
    dist = i[:, None] + tq - jnp.arange(2 * tq, dtype=jnp.int32)[None, :]
    sb = jnp.where((dist >= 0)[..., None], rel_bias_lookup(rel_bias, dist) - rel_bias[N_BUCKETS - 1], MASK_NEG)
    dist = i[:, None] + WINDOW - jnp.arange(WINDOW + tq, dtype=jnp.int32)[None, :]
    wb = jnp.where(((dist >= 0) & (dist < WINDOW))[..., None], rel_bias_lookup(rel_bias, dist), MASK_NEG)
    return heads_first(cb), heads_first(sb), heads_first(wb)


def nsa_prompt(q, kv, gates, kc, vc, tables):
    b, t = q.shape[:2]
    tq = NSA_TQ
    ncb = t // L_CMP
    ka = -(-(ncb + HEAD_DIM) // LANES) * LANES
    cb, sb, wb = tables
    q4 = (q * SCALE).astype(BF16).reshape(b, t, NSA_KV_HEADS, NSA_QPG, HEAD_DIM).transpose(0, 2, 3, 1, 4)
    g4 = gates.reshape(b, t, NSA_KV_HEADS, NSA_QPG * 3).transpose(0, 2, 1, 3)
    kcT = kc.astype(BF16).transpose(0, 2, 3, 1)
    vcg = vc.astype(BF16).transpose(0, 2, 1, 3)
    kvb = kv.astype(BF16)
    onehot = (2 * (jnp.arange(t, dtype=jnp.int32) // L_SEL)[None, :] == jnp.arange(ncb, dtype=jnp.int32)[:, None]).astype(BF16)
    ks_t = kvb[:, :, 2].transpose(0, 2, 3, 1)
    ksT = jnp.concatenate([jnp.broadcast_to(onehot, (b, NSA_KV_HEADS, ncb, t)), ks_t,
                           jnp.zeros((b, NSA_KV_HEADS, ka - ncb - HEAD_DIM, t), BF16)], axis=2)
    ksT = jnp.pad(ksT, ((0, 0), (0, 0), (0, 0), (tq, 0)))
    def with_ones(v):
        one = jnp.ones(v.shape[:-1] + (1,), BF16)
        return jnp.concatenate([v, one, jnp.zeros(v.shape[:-1] + (LANES - HEAD_DIM - 1,), BF16)], axis=-1)

    vs = jnp.pad(with_ones(kvb[:, :, 3].transpose(0, 2, 1, 3)), ((0, 0), (0, 0), (tq, 0), (0, 0)))
    kwT = jnp.pad(kvb[:, :, 4].transpose(0, 2, 3, 1), ((0, 0), (0, 0), (0, 0), (WINDOW, 0)))
    vw = jnp.pad(with_ones(kvb[:, :, 5].transpose(0, 2, 1, 3)), ((0, 0), (0, 0), (WINDOW, 0), (0, 0)))
    rows = NSA_QPG * tq

    def per_bg(shape):
        return pl.BlockSpec((1, 1) + shape, lambda bi, gi, qi: (bi, gi, 0, 0))

    return pl.pallas_call(
        _nsa_prompt_kernel,
        out_shape=jax.ShapeDtypeStruct((b, t, NSA_HEADS * HEAD_DIM), F32),
        grid=(b, NSA_KV_HEADS, t // tq),
        in_specs=[pl.BlockSpec((1, 1, NSA_QPG, tq, HEAD_DIM), lambda bi, gi, qi: (bi, gi, 0, qi, 0)),
                  pl.BlockSpec((1, 1, tq, NSA_QPG * 3), lambda bi, gi, qi: (bi, gi, qi, 0)),
                  per_bg((HEAD_DIM, ncb)), per_bg((ncb, HEAD_DIM)),
                  pl.BlockSpec((1, NSA_QPG, tq, ncb), lambda bi, gi, qi: (gi, 0, qi, 0)),
                  per_bg((ka, tq + t)), per_bg((tq + t, LANES)),
                  per_bg((HEAD_DIM, WINDOW + t)), per_bg((WINDOW + t, LANES)),
                  pl.BlockSpec((1, NSA_QPG, tq, 2 * tq), lambda bi, gi, qi: (gi, 0, 0, 0)),
                  pl.BlockSpec((1, NSA_QPG, tq, WINDOW + tq), lambda bi, gi, qi: (gi, 0, 0, 0))],
        out_specs=pl.BlockSpec((1, tq, NSA_QPG * HEAD_DIM), lambda bi, gi, qi: (bi, qi, gi)),
        scratch_shapes=[pltpu.VMEM((rows, ka), BF16),
                        pltpu.VMEM((rows, LANES), F32), pltpu.VMEM((rows, LANES), F32),
                        pltpu.VMEM((NSA_QPG, tq, HEAD_DIM), F32),
                        pltpu.VMEM((rows, HEAD_DIM), F32), pltpu.VMEM((rows, HEAD_DIM), F32)],
        compiler_params=pltpu.CompilerParams(dimension_semantics=("arbitrary", "arbitrary", "arbitrary"),
                                             vmem_limit_bytes=VMEM_LIMIT),
        name="nsa_prompt",
    )(q4, g4, kcT, vcg, cb, ksT, vs, kwT, vw, sb, wb)


def _gather_pages(pt_ref, cache_ref, sem, layer, feat0, dst_of):
    s = pl.program_id(0)
    n_pages = pt_ref.shape[1]
    slot = lax.rem(s, 2)

    def copy(page, sl, p):
        return pltpu.make_async_copy(cache_ref.at[layer, page, pl.ds(feat0, HALF_WIDTH), :], dst_of(sl, p), sem.at[sl])

    def start(seq, sl):
        for p in range(n_pages):
            copy(pt_ref[seq, p], sl, p).start()

    @pl.when(s == 0)
    def _():
        start(0, 0)

    @pl.when(s + 1 < pl.num_programs(0))
    def _():
        start(s + 1, 1 - slot)

    for p in range(n_pages):
        copy(0, slot, p).wait()
    return slot


def _decode_compress_kernel(pt_ref, cache_ref, w_ref, out_ref, buf, rows_ref, sem, *, layer):
    slot = _gather_pages(pt_ref, cache_ref, sem, layer, 0, lambda sl, p: buf.at[sl, p])
    ncb = out_ref.shape[1]
    n_pages = pt_ref.shape[1]
    pair = 2 * PAGE_SIZE
    blocks_per_pair = pair // L_CMP
    row = lax.broadcasted_iota(jnp.int32, (pair, pair), 0)
    col = lax.broadcasted_iota(jnp.int32, (pair, pair), 1)
    perm = (col == (row % blocks_per_pair) * L_CMP + row // blocks_per_pair).astype(F32).astype(BF16)
    for pp in range(n_pages // 2):
        pages = jnp.concatenate([buf[slot, 2 * pp], buf[slot, 2 * pp + 1]], axis=1).astype(BF16)
        x = _dot_nt(perm, pages)
        for part in range(2):
            for r in range(L_CMP):
                rows_ref[part, r, pp * blocks_per_pair:(pp + 1) * blocks_per_pair, :] = (
                    x[r * blocks_per_pair:(r + 1) * blocks_per_pair, part * KV_WIDTH:(part + 1) * KV_WIDTH])
    for part in range(2):
        acc = jnp.zeros((ncb, KV_WIDTH), F32)
        for r in range(L_CMP):
            acc = acc + jnp.dot(rows_ref[part, r].astype(BF16), w_ref[r, part], preferred_element_type=F32)
        out_ref[0, :, part * KV_WIDTH:(part + 1) * KV_WIDTH] = acc


def decode_compress(page_table, cache, w_bd, layer):
    bs, n_pages = page_table.shape
    past = n_pages * PAGE_SIZE
    ncb = past // L_CMP
    return pl.pallas_call(
        functools.partial(_decode_compress_kernel, layer=layer),
        out_shape=jax.ShapeDtypeStruct((bs, ncb, HALF_WIDTH), F32),
        grid_spec=pltpu.PrefetchScalarGridSpec(
            num_scalar_prefetch=1,
            grid=(bs,),
            in_specs=[pl.BlockSpec(memory_space=pl.ANY),
                      pl.BlockSpec((L_CMP, 2, KV_WIDTH, KV_WIDTH), lambda s, pt: (0, 0, 0, 0))],
            out_specs=pl.BlockSpec((1, ncb, HALF_WIDTH), lambda s, pt: (s, 0, 0)),
            scratch_shapes=[pltpu.VMEM((2, n_pages, HALF_WIDTH, PAGE_SIZE), F32),
                            pltpu.VMEM((2, L_CMP, ncb, KV_WIDTH), F32), pltpu.SemaphoreType.DMA((2,))]),
        compiler_params=pltpu.CompilerParams(dimension_semantics=("arbitrary",), vmem_limit_bytes=VMEM_LIMIT),
        name="decode_compress",
    )(page_table, cache, w_bd)


def _dot_nt(a, b):
    return lax.dot_general(a, b, (((1,), (1,)), ((), ())), preferred_element_type=F32)


def _decode_attend_kernel(pt_ref, cache_ref, q_ref, gate_ref, kcvc_ref, new_ref, win_ref, cb_ref, sb_ref, wb_ref,
                          rel0_ref, onehot_ref, hmask_ref, o_ref, buf, sem, *, layer):
    slot = _gather_pages(pt_ref, cache_ref, sem, layer, HALF_WIDTH,
                         lambda sl, p: buf.at[sl, :, pl.ds(p * PAGE_SIZE, PAGE_SIZE)])
    q = q_ref[0]
    qf = q.astype(F32)
    ncb = kcvc_ref.shape[1]
    rel0 = rel0_ref[:, 0:1]

    def bf_round(x):
        return x.astype(BF16).astype(F32)

    kc = kcvc_ref[0, :, 0:KV_WIDTH].astype(BF16)
    vc = kcvc_ref[0, :, KV_WIDTH:HALF_WIDTH].astype(BF16)
    lg = _dot_nt(q, kc) + cb_ref[...]
    e = jnp.exp(lg - jnp.max(lg, axis=-1, keepdims=True))
    p = e / jnp.sum(e, axis=-1, keepdims=True)
    o_cmp = jnp.dot(p.astype(BF16), vc, preferred_element_type=F32)

    row = lax.broadcasted_iota(jnp.int32, (NSA_HEADS, ncb), 0)
    lane = lax.broadcasted_iota(jnp.int32, (NSA_HEADS, ncb), 1)
    pg0 = jnp.sum(p[0:NSA_QPG], axis=0, keepdims=True)
    pg1 = jnp.sum(p[NSA_QPG:NSA_HEADS], axis=0, keepdims=True)
    imp = jnp.where(row < NSA_QPG, pg0, pg1)
    pair = imp + pltpu.roll(imp, ncb - 1, 1)
    blk = lane >> 1
    forced = (blk == 0) | (blk == ncb // 2 - 1)
    score = jnp.where((lane & 1) == 0, jnp.where(forced, jnp.inf, pair), NEG_INF)
    lane_f = lane.astype(F32)
    chosen = jnp.zeros((NSA_HEADS, ncb), F32)
    for _ in range(N_SEL - 1):
        mx = jnp.max(score, axis=-1, keepdims=True)
        first = jnp.min(jnp.where(score == mx, lane_f, float(ncb)), axis=-1, keepdims=True)
        hit = lane_f == first
        chosen = jnp.where(hit, 1.0, chosen)
        score = jnp.where(hit, NEG_INF, score)
    blockmask = jnp.where(chosen > 0.0, 0.0, MASK_NEG).astype(BF16)

    ks_t = buf[slot, 0:KV_WIDTH, :].astype(BF16)
    vs_t = buf[slot, KV_WIDTH:HALF_WIDTH, :].astype(BF16)
    s = (jnp.dot(q, ks_t, preferred_element_type=F32)
         + jnp.dot(blockmask, onehot_ref[...], preferred_element_type=F32) + sb_ref[...])
    s_new = jnp.sum(qf * bf_round(new_ref[0, 0:1, :]), axis=-1, keepdims=True) + rel0
    m = jnp.maximum(jnp.max(s, axis=-1, keepdims=True), s_new)
    e = jnp.exp(s - m)
    e_new = jnp.exp(s_new - m)
    den = jnp.sum(e, axis=-1, keepdims=True) + e_new
    o_sel = (_dot_nt(e.astype(BF16), vs_t) + e_new * bf_round(new_ref[0, 1:2, :])) / den

    wk_t = win_ref[0, 0, 0:KV_WIDTH, :].astype(BF16)
    wv_t = win_ref[0, 0, KV_WIDTH:HALF_WIDTH, :].astype(BF16)
    sw = jnp.dot(q, wk_t, preferred_element_type=F32) + wb_ref[...]
    sw_new = jnp.sum(qf * bf_round(new_ref[0, 2:3, :]), axis=-1, keepdims=True) + rel0
    mw = jnp.maximum(jnp.max(sw, axis=-1, keepdims=True), sw_new)
    ew = jnp.exp(sw - mw)
    ew_new = jnp.exp(sw_new - mw)
    denw = jnp.sum(ew, axis=-1, keepdims=True) + ew_new
    o_win = (_dot_nt(ew.astype(BF16), wv_t) + ew_new * bf_round(new_ref[0, 3:4, :])) / denw

    g = jax.nn.sigmoid(gate_ref[0])
    o_ref[0] = (g[:, 0:1] * o_cmp + g[:, 1:2] * o_sel + g[:, 2:3] * o_win) * hmask_ref[...]


def decode_tables(rel_bias, past):
    ncb = past // L_CMP
    end = jnp.arange(ncb, dtype=jnp.int32) * L_CMP + (L_CMP - 1)
    cb = rel_bias_lookup(rel_bias, past - end).T
    sb = rel_bias_lookup(rel_bias, past - jnp.arange(past, dtype=jnp.int32)).T
    dist = WINDOW - jnp.arange(WINDOW, dtype=jnp.int32)
    wb = jnp.where((dist < WINDOW)[None, :], rel_bias_lookup(rel_bias, dist).T, MASK_NEG)
    rel0 = jnp.broadcast_to(rel_bias[0][:, None], (NSA_HEADS, LANES))
    onehot = (2 * (jnp.arange(past, dtype=jnp.int32) // L_SEL)[None, :] == jnp.arange(ncb, dtype=jnp.int32)[:, None]).astype(BF16)
    hmask = (jnp.arange(KV_WIDTH, dtype=jnp.int32)[None, :] // HEAD_DIM == jnp.arange(NSA_HEADS, dtype=jnp.int32)[:, None] // NSA_QPG).astype(F32)
    return cb, sb, wb, rel0, onehot, hmask


def compress_block_weights(w_cmp):
    w = w_cmp.reshape(2, L_CMP, HEAD_DIM, HEAD_DIM).transpose(1, 0, 2, 3)
    z = jnp.zeros_like(w)
    return jnp.concatenate([jnp.concatenate([w, z], axis=3), jnp.concatenate([z, w], axis=3)], axis=2).astype(BF16)


def nsa_decode(nq, nkv, ngate, page_table, cache_nsa, cache_win, w_bd, tables, layer):
    bs = nq.shape[0]
    n_pages = page_table.shape[1]
    past = n_pages * PAGE_SIZE
    ncb = past // L_CMP
    cb, sb, wb, rel0, onehot, hmask = tables
    kcvc = decode_compress(page_table, cache_nsa, w_bd, layer)
    qh = nq.reshape(bs, NSA_HEADS, 1, HEAD_DIM) * SCALE
    own_group = jnp.arange(NSA_KV_HEADS)[None, None, :, None] == (jnp.arange(NSA_HEADS) // NSA_QPG)[None, :, None, None]
    qblk = (qh * own_group).reshape(bs, NSA_HEADS, KV_WIDTH).astype(BF16)
    gpad = jnp.pad(ngate.reshape(bs, NSA_HEADS, 3), ((0, 0), (0, 0), (0, LANES - 3)))
    newr = jnp.pad(nkv.reshape(bs, 6, KV_WIDTH)[:, 2:6], ((0, 0), (0, 4), (0, 0)))

    def const2(shape):
        return pl.BlockSpec(shape, lambda s, pt: (0, 0))

    out = pl.pallas_call(
        functools.partial(_decode_attend_kernel, layer=layer),
        out_shape=jax.ShapeDtypeStruct((bs, NSA_HEADS, KV_WIDTH), F32),
        grid_spec=pltpu.PrefetchScalarGridSpec(
            num_scalar_prefetch=1,
            grid=(bs,),
            in_specs=[pl.BlockSpec(memory_space=pl.ANY),
                      pl.BlockSpec((1, NSA_HEADS, KV_WIDTH), lambda s, pt: (s, 0, 0)),
                      pl.BlockSpec((1, NSA_HEADS, LANES), lambda s, pt: (s, 0, 0)),
                      pl.BlockSpec((1, ncb, HALF_WIDTH), lambda s, pt: (s, 0, 0)),
                      pl.BlockSpec((1, 8, KV_WIDTH), lambda s, pt: (s, 0, 0)),
                      pl.BlockSpec((1, 1, HALF_WIDTH, WINDOW), lambda s, pt: (layer, s, 0, 0)),
                      const2((NSA_HEADS, ncb)), const2((NSA_HEADS, past)), const2((NSA_HEADS, WINDOW)),
                      const2((NSA_HEADS, LANES)), const2((ncb, past)), const2((NSA_HEADS, KV_WIDTH))],
            out_specs=pl.BlockSpec((1, NSA_HEADS, KV_WIDTH), lambda s, pt: (s, 0, 0)),
            scratch_shapes=[pltpu.VMEM((2, HALF_WIDTH, past), F32), pltpu.SemaphoreType.DMA((2,))]),
        compiler_params=pltpu.CompilerParams(dimension_semantics=("arbitrary",), vmem_limit_bytes=VMEM_LIMIT),
        name="decode_attend",
    )(page_table, cache_nsa, qblk, gpad, kcvc, newr, cache_win, cb, sb, wb, rel0, onehot, hmask)
    return (out[..., :HEAD_DIM] + out[..., HEAD_DIM:]).reshape(bs, 1, NSA_HEADS * HEAD_DIM)


_PEER_CANDS = [(a, b) for a in range(PEER_TOPK) for b in range(PEER_TOPK) if (a + 1) * (b + 1) <= PEER_TOPK]
_PEER_NCAND = len(_PEER_CANDS)
_PEER_NCAND_PAD = -(-_PEER_NCAND // 8) * 8
_PEER_GROUP_START = [min(c for c, (a, _) in enumerate(_PEER_CANDS) if a == aa) for aa in range(PEER_TOPK)]
_PEER_GROUP_LEN = [sum(1 for (a, _) in _PEER_CANDS if a == aa) for aa in range(PEER_TOPK)]


def _peer_route_kernel(xT_ref, wqT_ref, keys_ref, r2_ref, beta_ref, alpha_ref, lam_ref,
                       qT_ref, s_ref, rk_ref, vals_ref, cand_ref, sel_ref):
    tl = xT_ref.shape[1]
    qT_ref[...] = jnp.dot(wqT_ref[...], xT_ref[...], preferred_element_type=F32).astype(BF16)
    iota_k = lax.broadcasted_iota(jnp.int32, (N_KEYS, tl), 0).astype(F32)

    def rank_top16(p, stable):
        cur = s_ref[p]
        rk = jnp.full((N_KEYS, tl), float(PEER_TOPK), F32)
        for a in range(PEER_TOPK):
            m = jnp.max(cur, axis=0, keepdims=True)
            if stable:
                idx = jnp.min(jnp.where(cur == m, iota_k, float(N_KEYS)), axis=0, keepdims=True)
                hit = iota_k == idx
            else:
                hit = cur == m
            rk = jnp.where(hit, float(a), rk)
            cur = jnp.where(hit, NEG_INF, cur)
            vals_ref[p, a:a + 1, :] = m
        rk_ref[p] = rk
        return jnp.sum(jnp.where(rk < float(PEER_TOPK), 1.0, 0.0), axis=0, keepdims=True)

    def head_body(h, carry):
        taken = jnp.zeros((1, tl), F32)
        for p in range(2):
            row0 = pl.multiple_of(h * PEER_DK + p * (PEER_DK // 2), PEER_DK // 2)
            qs = qT_ref[pl.ds(row0, PEER_DK // 2), :]
            s_ref[p] = jnp.dot(keys_ref[2 * h + p], qs, preferred_element_type=F32)
            taken = jnp.maximum(taken, rank_top16(p, stable=False))

        @pl.when(jnp.max(taken) > float(PEER_TOPK))
        def _():
            for p in range(2):
                rank_top16(p, stable=True)

        for c, (a, b) in enumerate(_PEER_CANDS):
            cand_ref[c:c + 1, :] = vals_ref[0, a:a + 1, :] + vals_ref[1, b:b + 1, :]
        if _PEER_NCAND_PAD > _PEER_NCAND:
            cand_ref[_PEER_NCAND:_PEER_NCAND_PAD, :] = jnp.full((_PEER_NCAND_PAD - _PEER_NCAND, tl), NEG_INF, F32)
        ngrp = _PEER_NCAND_PAD // 8
        iota8 = lax.broadcasted_iota(jnp.int32, (8, tl), 0)
        ranks = [jnp.zeros((8, tl), F32) for _ in range(ngrp)]
        for cp in range(_PEER_NCAND):
            rowb = cand_ref[cp:cp + 1, :]
            for k in range(ngrp):
                blk = cand_ref[8 * k:8 * k + 8, :]
                if 8 * k > cp:
                    inc = jnp.where(rowb >= blk, 1.0, 0.0)
                elif 8 * k + 7 < cp:
                    inc = jnp.where(rowb > blk, 1.0, 0.0)
                else:
                    inc = jnp.where(iota8 + 8 * k > cp, jnp.where(rowb >= blk, 1.0, 0.0), jnp.where(rowb > blk, 1.0, 0.0))
                ranks[k] = ranks[k] + inc
        top = cand_ref[0:1, :]
        z = jnp.zeros((1, tl), F32)
        for k in range(ngrp):
            blk = cand_ref[8 * k:8 * k + 8, :]
            selk = ranks[k] < float(PEER_TOPK)
            sel_ref[8 * k:8 * k + 8, :] = jnp.where(selk, 1.0, 0.0)
            z = z + jnp.sum(jnp.where(selk, jnp.exp(blk - top), 0.0), axis=0, keepdims=True)
        rk1 = rk_ref[0]
        lam = jnp.full((N_KEYS, tl), -1.0, F32)
        for a in range(PEER_TOPK):
            g0, gl = _PEER_GROUP_START[a], _PEER_GROUP_LEN[a]
            la = jnp.sum(sel_ref[g0:g0 + gl, :], axis=0, keepdims=True) - 1.0
            lam = jnp.where(rk1 == float(a), la, lam)
        alpha = jnp.where(rk1 < float(PEER_TOPK), jnp.exp(s_ref[0] - vals_ref[0, 0:1, :]), 0.0)
        rk2 = rk_ref[1]
        beta = jnp.where(rk2 < float(PEER_TOPK), jnp.exp(s_ref[1] - vals_ref[1, 0:1, :]), 0.0) / z
        r2_ref[h] = rk2.astype(BF16)
        beta_ref[h] = beta.astype(BF16)
        alpha_ref[h] = alpha
        lam_ref[h] = lam
        return carry

    lax.fori_loop(0, PEER_HEADS, head_body, 0)


def peer_route(xT, wqT, keys2):
    d, n = xT.shape
    tl = PEER_ROUTE_TL if n % PEER_ROUTE_TL == 0 else LANES
    assert n % tl == 0
    nq = PEER_HEADS * PEER_DK
    out_bf = jax.ShapeDtypeStruct((PEER_HEADS, N_KEYS, n), BF16)
    out_f = jax.ShapeDtypeStruct((PEER_HEADS, N_KEYS, n), F32)
    tab_spec = pl.BlockSpec((PEER_HEADS, N_KEYS, tl), lambda i: (0, 0, i))
    return pl.pallas_call(
        _peer_route_kernel,
        out_shape=(out_bf, out_bf, out_f, out_f),
        grid=(n // tl,),
        in_specs=[pl.BlockSpec((d, tl), lambda i: (0, i)),
                  pl.BlockSpec((nq, d), lambda i: (0, 0)),
                  pl.BlockSpec((2 * PEER_HEADS, N_KEYS, PEER_DK // 2), lambda i: (0, 0, 0))],
        out_specs=(tab_spec, tab_spec, tab_spec, tab_spec),
        scratch_shapes=[pltpu.VMEM((nq, tl), BF16),
                        pltpu.VMEM((2, N_KEYS, tl), F32),
                        pltpu.VMEM((2, N_KEYS, tl), F32),
                        pltpu.VMEM((2, PEER_TOPK, tl), F32),
                        pltpu.VMEM((_PEER_NCAND_PAD, tl), F32),
                        pltpu.VMEM((_PEER_NCAND_PAD, tl), F32)],
        compiler_params=pltpu.CompilerParams(dimension_semantics=("arbitrary",), vmem_limit_bytes=VMEM_LIMIT),
        name="peer_route",
    )(xT, wqT, keys2)


def _gelu_tanh(x):
    return 0.5 * x * (1.0 + jnp.tanh(math.sqrt(2.0 / math.pi) * (x + 0.044715 * (x * x * x))))


def _peer_dense_kernel(xT_ref, u_ref, vT_ref, r2_ref, beta_ref, alpha_ref, lam_ref, yT_ref, a_ref, h_ref):
    j = pl.program_id(1)
    te = u_ref.shape[0]
    tl = xT_ref.shape[1]

    @pl.when(j == 0)
    def _():
        yT_ref[...] = jnp.zeros_like(yT_ref)

    n_sub = te // N_KEYS

    def pre_activation(r):
        rows = slice(r * N_KEYS, (r + 1) * N_KEYS)
        a_ref[rows, :] = jnp.dot(u_ref[rows, :], xT_ref[...], preferred_element_type=F32)

    pre_activation(0)
    for r in range(n_sub):
        rows = slice(r * N_KEYS, (r + 1) * N_KEYS)
        if r + 1 < n_sub:
            pre_activation(r + 1)
        g = jnp.zeros((N_KEYS, tl), BF16)
        for h in range(PEER_HEADS):
            lam = lam_ref[h, r:r + 1, :].astype(BF16)
            alp = alpha_ref[h, r:r + 1, :].astype(BF16)
            g = g + jnp.where(r2_ref[h] <= lam, beta_ref[h], jnp.zeros((), BF16)) * alp
        h_ref[rows, :] = _gelu_tanh(a_ref[rows, :].astype(BF16)) * g
    yT_ref[...] += jnp.dot(vT_ref[...], h_ref[...], preferred_element_type=F32)


def peer_dense(xT, u_bf, vT_bf, r2, beta, alpha, lam, tl, te):
    d, n = xT.shape
    e = u_bf.shape[0]
    assert n % tl == 0 and e % te == 0 and te % (8 * N_KEYS) == 0
    tab_spec = pl.BlockSpec((PEER_HEADS, N_KEYS, tl), lambda i, j: (0, 0, i))
    row_spec = pl.BlockSpec((PEER_HEADS, te // N_KEYS, tl), lambda i, j: (0, j, i))
    return pl.pallas_call(
        _peer_dense_kernel,
        out_shape=jax.ShapeDtypeStruct((d, n), F32),
        grid=(n // tl, e // te),
        in_specs=[pl.BlockSpec((d, tl), lambda i, j: (0, i)),
                  pl.BlockSpec((te, d), lambda i, j: (j, 0)),
                  pl.BlockSpec((d, te), lambda i, j: (0, j)),
                  tab_spec, tab_spec, row_spec, row_spec],
        out_specs=pl.BlockSpec((d, tl), lambda i, j: (0, i)),
        scratch_shapes=[pltpu.VMEM((te, tl), F32), pltpu.VMEM((te, tl), BF16)],
        compiler_params=pltpu.CompilerParams(dimension_semantics=("arbitrary", "arbitrary"), vmem_limit_bytes=VMEM_LIMIT),
        name="peer_dense",
    )(xT, u_bf, vT_bf, r2, beta, alpha, lam)


def peer_ffn_t(xT, peer_w):
    wqT, keys2, u_bf, vT_bf = peer_w
    n = xT.shape[1]
    tl = PEER_TL if n % PEER_TL == 0 else LANES
    r2, beta, alpha, lam = peer_route(xT, wqT, keys2)
    return peer_dense(xT, u_bf, vT_bf, r2, beta, alpha, lam, tl, PEER_TE)


def peer_ffn(h, peer_w):
    b, t, d = h.shape
    n = b * t
    npad = -(-n // LANES) * LANES
    xT = jnp.pad(h.reshape(n, d).astype(BF16).T, ((0, 0), (0, npad - n)))
    return peer_ffn_t(xT, peer_w).T[:n].reshape(b, t, d)


POST_TM = 512
_POST_ROWS = 8


def _post_kernel(x_ref, y_ref, vec_ref, ox_ref, *oh_ref, alpha, y_transposed, h_transposed):
    y = y_ref[...]
    if y_transposed:
        y = y.T
    vec = vec_ref[0]
    r = alpha * x_ref[...] + vec[0:1] * y
    mu = jnp.mean(r, axis=-1, keepdims=True)
    rc = r - mu
    var = jnp.mean(rc * rc, axis=-1, keepdims=True)
    xn = rc * lax.rsqrt(var + LN_EPS) * vec[3:4] + vec[4:5]
    ox_ref[...] = xn
    if oh_ref:
        hm = xn * vec[1:2] + vec[2:3]
        oh_ref[0][...] = (hm.T if h_transposed else hm).astype(BF16)


def residual_norm_modulate(x, y, gate, ln_g, ln_b, scale, shift, alpha, tokens_per_batch, y_transposed, h_transposed):
    n, d = x.shape
    tm = POST_TM
    assert n % tm == 0 and tokens_per_batch % tm == 0
    nb = gate.shape[0]
    want_h = scale is not None
    if not want_h:
        scale = shift = jnp.zeros_like(gate)
    vec = jnp.stack([gate, scale, shift, jnp.broadcast_to(ln_g, (nb, d)), jnp.broadcast_to(ln_b, (nb, d))], axis=1)
    vec = jnp.pad(vec, ((0, 0), (0, _POST_ROWS - vec.shape[1]), (0, 0)))
    tiles_per_batch = tokens_per_batch // tm
    y_spec = pl.BlockSpec((d, tm), lambda i: (0, i)) if y_transposed else pl.BlockSpec((tm, d), lambda i: (i, 0))
    out_shape = [jax.ShapeDtypeStruct((n, d), F32)]
    out_specs = [pl.BlockSpec((tm, d), lambda i: (i, 0))]
    if want_h:
        out_shape.append(jax.ShapeDtypeStruct((d, n) if h_transposed else (n, d), BF16))
        out_specs.append(pl.BlockSpec((d, tm), lambda i: (0, i)) if h_transposed else pl.BlockSpec((tm, d), lambda i: (i, 0)))
    outs = pl.pallas_call(
        functools.partial(_post_kernel, alpha=alpha, y_transposed=y_transposed, h_transposed=h_transposed),
        out_shape=tuple(out_shape),
        grid=(n // tm,),
        in_specs=[pl.BlockSpec((tm, d), lambda i: (i, 0)), y_spec,
                  pl.BlockSpec((1, _POST_ROWS, d), lambda i: (i // tiles_per_batch, 0, 0))],
        out_specs=tuple(out_specs),
        compiler_params=pltpu.CompilerParams(dimension_semantics=("arbitrary",), vmem_limit_bytes=VMEM_LIMIT),
        name="residual_norm_modulate",
    )(x, y, vec)
    return (outs[0], outs[1]) if want_h else (outs[0], None)


def token_mixers(parts, pos, rel_table, w_cmp, conv_w, conv_b, ret_g, past, nsa_tables):
    nq, nkv, ngate, rq, rk, rv, rg, cb, cc, ch = parts
    b, t = nq.shape[:2]
    q = nq.reshape(b, t, NSA_KV_HEADS, NSA_QPG, HEAD_DIM)
    kv = nkv.reshape(b, t, 6, NSA_KV_HEADS, HEAD_DIM)
    if past is None:
        kc = compress(kv[:, :, 0], w_cmp[0])
        vc = compress(kv[:, :, 1], w_cmp[1])
        o_nsa = nsa_prompt(nq, kv, ngate, kc, vc, nsa_tables)
        win_rows = kv[:, t - min(WINDOW, t):, 4:]
        zbuf = jnp.zeros((b, CONV_W - 1, CONV_CH), ch.dtype)
        o_ret, s_new = retention_prompt(rq, rk, rv, rg, ret_g)
    else:
        decode_nsa, win_buf, s0, zbuf = past
        o_nsa = decode_nsa(nq, nkv, ngate)
        wrows = jnp.concatenate([win_buf.astype(kv.dtype), kv[:, :, 4:]], axis=1)
        win_rows = wrows[:, wrows.shape[1] - min(WINDOW, wrows.shape[1]):]
        rqh = rotary(rq.reshape(b, t, RET_HEADS, RET_DK), pos)
        rkh = rotary(rk.reshape(b, t, RET_HEADS, RET_DK), pos) * (RET_DK ** -0.5)
        rvh = rv.reshape(b, t, RET_HEADS, RET_DV)
        o_r, s_new = retention(rqh, rkh, rvh, s0, t)
        mu = jnp.mean(o_r, -1, keepdims=True)
        var = jnp.mean(jnp.square(o_r - mu), -1, keepdims=True)
        on = (o_r - mu) * lax.rsqrt(var + LN_EPS) * ret_g.reshape(RET_HEADS, RET_DV)
        o_ret = (on.reshape(b, t, RET_WIDTH) * jax.nn.silu(rg.astype(jnp.float32))).astype(nq.dtype)
    z = cc * ch
    zp = jnp.concatenate([zbuf.astype(z.dtype), z], axis=1)
    y = conv_b + sum(zp[:, j:j + t] * conv_w[j] for j in range(CONV_W))
    o_conv = (cb * y).astype(nq.dtype)
    mix = jnp.concatenate([o_nsa, o_ret, o_conv], axis=-1)
    return mix, (kv[:, :, :4], win_rows, s_new, zp[:, t:])


def kernel(x_prompt, x_sample, cache_nsa_kv, cache_win_kv, state_ret, state_conv, page_table, c_prompt, c_sample, rel_bias, w_ada, b_ada, w_in, w_cmp, conv_w, conv_b, ret_norm_g, w_out, ln1_g, ln1_b, ln2_g, ln2_b, peer_wq, peer_keys, peer_u, peer_v):
    alpha = (2.0 * DEPTH) ** 0.25
    n_pages = page_table.shape[1]
    past_len = n_pages * PAGE_SIZE
    sp = x_prompt.shape[1]
    bs, ts = x_sample.shape[:2]
    pos_p = jnp.arange(sp, dtype=jnp.int32)
    pos_s = past_len + jnp.arange(ts, dtype=jnp.int32)
    nsa_tables = nsa_bias_tables(rel_bias, sp, NSA_TQ)
    dec_tables = decode_tables(rel_bias, past_len)
    cache_nsa = cache_nsa_kv.transpose(0, 1, 3, 4, 5, 2).reshape(cache_nsa_kv.shape[0], cache_nsa_kv.shape[1], ROW_WIDTH, PAGE_SIZE)
    cache_win = cache_win_kv.transpose(0, 1, 3, 4, 5, 2).reshape(cache_win_kv.shape[0], bs, HALF_WIDTH, cache_win_kv.shape[2])

    def ada(c, l):
        return (jax.nn.silu(c) @ w_ada[l] + b_ada[l]).reshape(c.shape[0], 6, D_MODEL)

    def run_layer(x, c, l, pos, past, peer_w):
        m = ada(c, l)[:, :, None, :]
        h = x * (1.0 + m[:, 1]) + m[:, 0]
        parts = split_proj(h @ w_in[l])
        mix, st = token_mixers(parts, pos, rel_bias, w_cmp[l], conv_w[l], conv_b[l], ret_norm_g[l], past, nsa_tables)
        x = layer_norm(alpha * x + (1.0 + m[:, 2]) * (mix @ w_out[l]), ln1_g[l], ln1_b[l])
        h = x * (1.0 + m[:, 4]) + m[:, 3]
        y = peer_ffn(h, peer_w)
        x = layer_norm(alpha * x + (1.0 + m[:, 5]) * y, ln2_g[l], ln2_b[l])
        return x, st

    bp = x_prompt.shape[0]
    m_prompt = [ada(c_prompt, l) for l in range(DEPTH)]

    def run_prompt_layer(x2, h, l, peer_w):
        m = m_prompt[l]
        parts = split_proj(jnp.matmul(h, w_in[l].astype(h.dtype), preferred_element_type=F32))
        mix, st = token_mixers(parts, pos_p, rel_bias, w_cmp[l], conv_w[l], conv_b[l], ret_norm_g[l], None, nsa_tables)
        y = (mix @ w_out[l]).reshape(bp * sp, D_MODEL)
        x2, h_t = residual_norm_modulate(x2, y, 1.0 + m[:, 2], ln1_g[l], ln1_b[l], 1.0 + m[:, 4], m[:, 3], alpha, sp,
                                         y_transposed=False, h_transposed=True)
        y_t = peer_ffn_t(h_t, peer_w)
        if l + 1 < DEPTH:
            nxt = m_prompt[l + 1]
            x2, h = residual_norm_modulate(x2, y_t, 1.0 + m[:, 5], ln2_g[l], ln2_b[l], 1.0 + nxt[:, 1], nxt[:, 0], alpha, sp,
                                           y_transposed=True, h_transposed=False)
            h = h.reshape(bp, sp, D_MODEL)
        else:
            x2, h = residual_norm_modulate(x2, y_t, 1.0 + m[:, 5], ln2_g[l], ln2_b[l], None, None, alpha, sp,
                                           y_transposed=True, h_transposed=False)
        return x2, h, st

    xp2 = x_prompt.reshape(bp * sp, D_MODEL)
    hp = x_prompt * (1.0 + m_prompt[0][:, 1][:, None, :]) + m_prompt[0][:, 0][:, None, :]
    xs = x_sample
    nkv_p, nkv_s, win_p, win_s, ret_p, ret_s, conv_p, conv_s = [], [], [], [], [], [], [], []
    for l in range(DEPTH):
        peer_w = (peer_wq[l].T.astype(BF16),
                  peer_keys[l].reshape(2 * PEER_HEADS, N_KEYS, PEER_DK // 2).astype(BF16),
                  peer_u[l].astype(BF16),
                  peer_v[l].T.astype(BF16))
        xp2, hp, st_p = run_prompt_layer(xp2, hp, l, peer_w)
        decode_nsa = functools.partial(nsa_decode, page_table=page_table, cache_nsa=cache_nsa, cache_win=cache_win,
                                       w_bd=compress_block_weights(w_cmp[l]), tables=dec_tables, layer=l)
        xs, st_s = run_layer(xs, c_sample, l, pos_s, (decode_nsa, cache_win_kv[l], state_ret[l], state_conv[l]), peer_w)
        nkv_p.append(st_p[0])
        win_p.append(st_p[1])
        ret_p.append(st_p[2])
        conv_p.append(st_p[3])
        nkv_s.append(st_s[0])
        win_s.append(st_s[1])
        ret_s.append(st_s[2])
        conv_s.append(st_s[3])
    xp = xp2.reshape(bp, sp, D_MODEL)
    return (xp, xs, jnp.stack(nkv_p), jnp.stack(nkv_s), jnp.stack(win_p), jnp.stack(win_s), jnp.stack(ret_p), jnp.stack(ret_s), jnp.stack(conv_p), jnp.stack(conv_s))
```
